```python
import math
import jax, jax.numpy as jnp
from jax import lax
import numpy as np

D_MODEL = 1024
BATCH = 16
SEQ = 4096
DEPTH = 2
DEC_BATCH = 16
DEC_SEQ = 2048
PAST_LEN = 128

D_MIX = D_MODEL
N_GROUPS = 4
W_GROUP = D_MIX // N_GROUPS
H_A = 4
NOPE_A = 64
ROPE_A = 32
V_A = W_GROUP // H_A
Q_LORA = 192
KV_LORA = 128
H_B = 4
DK_B = 32
DV_B = W_GROUP // H_B
W_C = W_GROUP
H_C = 4
BW_C = W_C // H_C
CONV_W = 4
CONV_PAD_L = CONV_W // 2
LRU_C = 8.0
H_D = 4
KV_H_D = 2
HD_D = W_GROUP // H_D
GRID_W = 64
ROPE_THETA = 500000.0
ROPE_FRAC = 4
MLA_ROPE_THETA = 10000.0
AXIAL_THETA = 10000.0
Q_BLOCK = 128
EPS = 1e-6
IN_SIZES = (Q_LORA, KV_LORA, ROPE_A, W_GROUP,
            H_B * 2 * DK_B, H_B * 2 * DK_B, H_B * DV_B, W_GROUP,
            W_C, W_GROUP,
            H_D * HD_D, KV_H_D * HD_D, KV_H_D * HD_D, W_GROUP)
N_IN = sum(IN_SIZES)

kernel_name = 'hybrid_hymba_encoder_two_groups'


def rms_norm(x, g):
    xf = x.astype(jnp.float32)
    y = xf * lax.rsqrt(jnp.mean(xf * xf, axis=-1, keepdims=True) + EPS)
    return (y * g.astype(jnp.float32)).astype(x.dtype)


def rope(x, pos, theta):
    half = x.shape[-1] // 2
    inv = jnp.power(jnp.float32(theta), -jnp.arange(half, dtype=jnp.float32) / half)
    ang = pos[:, None] * inv[None, :]
    cos = jnp.cos(ang)[None, :, None, :]
    sin = jnp.sin(ang)[None, :, None, :]
    xf = x.astype(jnp.float32)
    x1, x2 = xf[..., :half], xf[..., half:]
    return jnp.concatenate([x1 * cos - x2 * sin, x2 * cos + x1 * sin], axis=-1).astype(x.dtype)


def partial_rope(x, pos):
    r = x.shape[-1] // ROPE_FRAC
    return jnp.concatenate([rope(x[..., :r], pos, ROPE_THETA), x[..., r:]], axis=-1)


def axial_rope(x, row, col):
    half = x.shape[-1] // 2
    return jnp.concatenate([rope(x[..., :half], row, AXIAL_THETA),
                            rope(x[..., half:], col, AXIAL_THETA)], axis=-1)


def blocked_attention(q, k, v, scale):
    b, s, h, dk = q.shape
    g = k.shape[2]
    rep = h // g
    dv = v.shape[-1]
    nb = s // Q_BLOCK
    qb = q.reshape(b, nb, Q_BLOCK, g, rep, dk).transpose(1, 0, 2, 3, 4, 5)

    def one_block(qblk):
        sc = jnp.einsum('bqgrd,bkgd->bgrqk', qblk, k, preferred_element_type=jnp.float32) * scale
        p = jax.nn.softmax(sc, axis=-1).astype(v.dtype)
        return jnp.einsum('bgrqk,bkgv->bqgrv', p, v)

    out = lax.map(one_block, qb)
    return out.transpose(1, 0, 2, 3, 4, 5).reshape(b, s, h, dv)


def mla_branch(q_lat, kv_lat, k_rope, pos, q_norm, w_uq, kv_norm, w_ukv):
    b, s, _ = q_lat.shape
    q = (rms_norm(q_lat, q_norm) @ w_uq).reshape(b, s, H_A, NOPE_A + ROPE_A)
    kv = (rms_norm(kv_lat, kv_norm) @ w_ukv).reshape(b, s, H_A, NOPE_A + V_A)
    q_nope, q_pe = q[..., :NOPE_A], q[..., NOPE_A:]
    k_nope, v = kv[..., :NOPE_A], kv[..., NOPE_A:]
    q_pe = rope(q_pe, pos, MLA_ROPE_THETA)
    k_pe = rope(k_rope[:, :, None, :], pos, MLA_ROPE_THETA)
    qf = jnp.concatenate([q_nope, q_pe], axis=-1)
    kf = jnp.concatenate([k_nope, jnp.broadcast_to(k_pe, (b, s, H_A, ROPE_A))], axis=-1)
    o = blocked_attention(qf, kf, v, (NOPE_A + ROPE_A) ** -0.5)
    return o.reshape(b, s, H_A * V_A)


def diff_branch(q, k, v, pos, lam, subln, layer_idx):
    b, s, _ = q.shape
    q = partial_rope(q.reshape(b, s, 2 * H_B, DK_B), pos).reshape(b, s, H_B, 2, DK_B)
    k = partial_rope(k.reshape(b, s, 2 * H_B, DK_B), pos).reshape(b, s, H_B, 2, DK_B)
    v = v.reshape(b, s, H_B, DV_B)
    lam_init = 0.8 - 0.6 * math.exp(-0.3 * layer_idx)
    lf = lam.astype(jnp.float32)
    lam_full = jnp.exp(jnp.sum(lf[0] * lf[1])) - jnp.exp(jnp.sum(lf[2] * lf[3])) + lam_init
    o1 = blocked_attention(q[..., 0, :], k[..., 0, :], v, DK_B ** -0.5)
    o2 = blocked_attention(q[..., 1, :], k[..., 1, :], v, DK_B ** -0.5)
    o = o1 - lam_full.astype(o1.dtype) * o2
    o = rms_norm(o, subln) * (1.0 - lam_init)
    return o.reshape(b, s, H_B * DV_B)


def blockdiag(x, w):
    b, s, _ = x.shape
    return jnp.einsum('bshi,hij->bshj', x.reshape(b, s, H_C, BW_C), w).reshape(b, s, W_C)


def scan_combine(e1, e2):
    a1, b1 = e1
    a2, b2 = e2
    return a1 * a2, a2 * b1 + b2


def rglru_branch(xc, conv_w, conv_b, wa, ba, wx, bx, lam):
    b, s, _ = xc.shape
    xp = jnp.pad(xc, ((0, 0), (CONV_PAD_L, CONV_W - 1 - CONV_PAD_L), (0, 0)))
    xconv = conv_b + xp[:, 0:s] * conv_w[0]
    for j in range(1, CONV_W):
        xconv = xconv + xp[:, j:j + s] * conv_w[j]

    def direction(d, reverse):
        r = jax.nn.sigmoid(blockdiag(xconv, wa[d]) + ba[d]).astype(jnp.float32)
        i = jax.nn.sigmoid(blockdiag(xconv, wx[d]) + bx[d])
        log_a = -LRU_C * r * jax.nn.softplus(-lam[d].astype(jnp.float32))
        a = jnp.exp(log_a)
        gx = jnp.sqrt(-jnp.expm1(2.0 * log_a)) * (i * xconv).astype(jnp.float32)
        _, h = lax.associative_scan(scan_combine, (a, gx), axis=1, reverse=reverse)
        return h

    h = direction(0, False) + direction(1, True)
    return h.astype(xc.dtype)


def gqa_branch(q, k, v, row, col, q_norm, k_norm):
    b, s, _ = q.shape
    q = axial_rope(rms_norm(q.reshape(b, s, H_D, HD_D), q_norm), row, col)
    k = axial_rope(rms_norm(k.reshape(b, s, KV_H_D, HD_D), k_norm), row, col)
    v = v.reshape(b, s, KV_H_D, HD_D)
    o = blocked_attention(q, k, v, HD_D ** -0.5)
    return o.reshape(b, s, H_D * HD_D)


def layer(x, c, l, p, pos, row, col):
    mod = jax.nn.silu(c) @ p['ada_w'][l] + p['ada_b'][l]
    shift, scale, gate = jnp.split(mod, 3, axis=-1)
    h = rms_norm(x, p['norm_g'][l]) * (1.0 + scale[:, None, :]) + shift[:, None, :]
    z = h @ p['w_in'][l]
    points = np.cumsum(IN_SIZES)[:-1].tolist()
    (qa, kva, kra, ga, qb, kb, vb, gb, xc, gc, qd, kd, vd, gd) = jnp.split(z, points, axis=-1)
    oa = mla_branch(qa, kva, kra, pos, p['mla_q_norm'][l], p['mla_w_uq'][l],
                    p['mla_kv_norm'][l], p['mla_w_ukv'][l])
    ob = diff_branch(qb, kb, vb, pos, p['diff_lambda'][l], p['diff_subln'][l], l)
    oc = rglru_branch(xc, p['lru_conv_w'][l], p['lru_conv_b'][l], p['lru_wa'][l], p['lru_ba'][l],
                      p['lru_wx'][l], p['lru_bx'][l], p['lru_lambda'][l])
    od = gqa_branch(qd, kd, vd, row, col, p['gqa_q_norm'][l], p['gqa_k_norm'][l])
    o = jnp.concatenate([oa * jax.nn.silu(ga), ob * jax.nn.silu(gb),
                         oc * jax.nn.silu(gc), od * jax.nn.silu(gd)], axis=-1)
    return x + gate[:, None, :] * (o @ p['w_out'][l])


def trunk(x, c, p):
    s = x.shape[1]
    rows = s // GRID_W
    pos = jnp.arange(s, dtype=jnp.float32)
    row = jnp.repeat(jnp.arange(rows, dtype=jnp.float32), GRID_W)
    col = jnp.tile(jnp.arange(GRID_W, dtype=jnp.float32), rows)
    for l in range(DEPTH):
        x = layer(x, c, l, p, pos, row, col)
    return rms_norm(x, p['final_norm'])


def setup_inputs(seed: int = 0) -> dict:
    key = jax.random.key(seed)
    ks = jax.random.split(key, 25)
    f32 = jnp.float32

    def nrm(k, shape, scale):
        return jax.random.normal(k, shape, f32) * scale

    def gain(k, shape):
        return 1.0 + 0.02 * jax.random.normal(k, shape, f32)

    u = jax.random.uniform(ks[20], (DEPTH, 2, W_C), f32, 0.9, 0.999)
    a_base = u ** (1.0 / LRU_C)
    lru_lambda = jnp.log(a_base) - jnp.log1p(-a_base)
    return {
        'x_prompt': nrm(ks[0], (BATCH, SEQ, D_MODEL), 1.0),
        'x_sample': nrm(ks[1], (DEC_BATCH, DEC_SEQ, D_MODEL), 1.0),
        'c_prompt': nrm(ks[2], (BATCH, D_MODEL), 1.0),
        'c_sample': nrm(ks[3], (DEC_BATCH, D_MODEL), 1.0),
        'ada_w': nrm(ks[4], (DEPTH, D_MODEL, 3 * D_MODEL), 0.5 * D_MODEL ** -0.5),
        'ada_b': nrm(ks[5], (DEPTH, 3 * D_MODEL), 0.02),
        'norm_g': gain(ks[6], (DEPTH, D_MODEL)),
        'w_in': nrm(ks[7], (DEPTH, D_MODEL, N_IN), D_MODEL ** -0.5),
        'mla_q_norm': gain(ks[8], (DEPTH, Q_LORA)),
        'mla_w_uq': nrm(ks[9], (DEPTH, Q_LORA, H_A * (NOPE_A + ROPE_A)), Q_LORA ** -0.5),
        'mla_kv_norm': gain(ks[10], (DEPTH, KV_LORA)),
        'mla_w_ukv': nrm(ks[11], (DEPTH, KV_LORA, H_A * (NOPE_A + V_A)), KV_LORA ** -0.5),
        'diff_lambda': nrm(ks[12], (DEPTH, 4, DK_B), 0.1),
        'diff_subln': gain(ks[13], (DEPTH, DV_B)),
        'lru_conv_w': nrm(ks[14], (DEPTH, CONV_W, W_C), CONV_W ** -0.5),
        'lru_conv_b': nrm(ks[15], (DEPTH, W_C), 0.02),
        'lru_wa': nrm(ks[16], (DEPTH, 2, H_C, BW_C, BW_C), BW_C ** -0.5),
        'lru_ba': nrm(ks[17], (DEPTH, 2, W_C), 0.02),
        'lru_wx': nrm(ks[18], (DEPTH, 2, H_C, BW_C, BW_C), BW_C ** -0.5),
        'lru_bx': nrm(ks[19], (DEPTH, 2, W_C), 0.02),
        'lru_lambda': lru_lambda,
        'gqa_q_norm': gain(ks[21], (DEPTH, HD_D)),
        'gqa_k_norm': gain(ks[22], (DEPTH, HD_D)),
        'w_out': nrm(ks[23], (DEPTH, D_MIX, D_MODEL), D_MIX ** -0.5),
        'final_norm': gain(ks[24], (D_MODEL,)),
    }


def reference(x_prompt, x_sample, c_prompt, c_sample, ada_w, ada_b, norm_g, w_in,
              mla_q_norm, mla_w_uq, mla_kv_norm, mla_w_ukv, diff_lambda, diff_subln,
              lru_conv_w, lru_conv_b, lru_wa, lru_ba, lru_wx, lru_bx, lru_lambda,
              gqa_q_norm, gqa_k_norm, w_out, final_norm):
    p = dict(ada_w=ada_w, ada_b=ada_b, norm_g=norm_g, w_in=w_in,
             mla_q_norm=mla_q_norm, mla_w_uq=mla_w_uq, mla_kv_norm=mla_kv_norm, mla_w_ukv=mla_w_ukv,
             diff_lambda=diff_lambda, diff_subln=diff_subln,
             lru_conv_w=lru_conv_w, lru_conv_b=lru_conv_b, lru_wa=lru_wa, lru_ba=lru_ba,
             lru_wx=lru_wx, lru_bx=lru_bx, lru_lambda=lru_lambda,
             gqa_q_norm=gqa_q_norm, gqa_k_norm=gqa_k_norm, w_out=w_out, final_norm=final_norm)
    y_prompt = trunk(x_prompt, c_prompt, p)
    y_sample = trunk(x_sample, c_sample, p)
    return (y_prompt, y_sample)
```

```python
import functools
import math

import jax
import jax.numpy as jnp
import numpy as np
from jax import lax
from jax.experimental import pallas as pl
from jax.experimental.pallas import tpu as pltpu

F32 = jnp.float32
BF16 = jnp.bfloat16

D_MODEL = 1024
DEPTH = 2
W_GROUP = 256
H_A, NOPE_A, ROPE_A, V_A = 4, 64, 32, 64
Q_LORA, KV_LORA = 192, 128
H_B, DK_B, DV_B = 4, 32, 64
H_C, BW_C, CONV_W = 4, 64, 4
LRU_C = 8.0
H_D, KV_H_D, HD_D = 4, 2, 64
GRID_W = 64
ROPE_THETA = 500000.0
MLA_ROPE_THETA = 10000.0
AXIAL_THETA = 10000.0
EPS = 1e-6
LOG2E = 1.4426950408889634

IN_SIZES = (Q_LORA, KV_LORA, ROPE_A, W_GROUP,
            H_B * 2 * DK_B, H_B * 2 * DK_B, H_B * DV_B, W_GROUP,
            W_GROUP, W_GROUP,
            H_D * HD_D, KV_H_D * HD_D, KV_H_D * HD_D, W_GROUP)
_OFF = np.concatenate([[0], np.cumsum(IN_SIZES)]).tolist()
(O_QA, O_KVA, O_KRA, O_GA, O_QB, O_KB, O_VB, O_GB, O_XC, O_GC, O_QD, O_KD, O_VD, O_GD, _) = _OFF

LANES = 128
SUBLANES = 8

TOK_TILE = 512
Q_TILE = 256
LRU_CHUNK = 512
VMEM_LIMIT = 56 * 1024 * 1024

ZS_KVL, ZS_KPE, ZS_KPE_SW, ZS_KB, ZS_KB_SW, ZS_KD, ZS_KD_SW, ZS_XC, ZS_G, ZS_END = (
    0, 128, 256, 384, 640, 896, 1024, 1152, 1408, 2432)
ZT_QA, ZT_KVL, ZT_QB, ZT_QB_SW, ZT_VB, ZT_QD, ZT_QD_SW, ZT_VD, ZT_END = (
    0, 192, 320, 576, 832, 1088, 1344, 1600, 1728)


def _partner(width, block, rot, half):
    idx = np.arange(width)
    d = idx % block
    out = idx.copy()
    in_rot = d < rot
    first = in_rot & ((d % (2 * half)) < half)
    second = in_rot & ~((d % (2 * half)) < half)
    out[first] = idx[first] + half
    out[second] = idx[second] - half
    return out


P_MLA = _partner(ROPE_A, ROPE_A, ROPE_A, ROPE_A // 2)
P_DIFF = _partner(2 * H_B * DK_B, DK_B, DK_B // 4, DK_B // 8)
P_GQA_Q = _partner(H_D * HD_D, HD_D // 2, HD_D // 2, HD_D // 4)
P_GQA_K = _partner(KV_H_D * HD_D, HD_D // 2, HD_D // 2, HD_D // 4)
P_GQA_HEAD = _partner(HD_D, HD_D // 2, HD_D // 2, HD_D // 4)


def _cparams(n_axes):
    return pltpu.CompilerParams(dimension_semantics=("arbitrary",) * n_axes,
                                vmem_limit_bytes=VMEM_LIMIT)


def _mod_kernel(c_ref, w_ref, b_ref, out_ref):
    c = c_ref[...]
    sc = (c * jax.nn.sigmoid(c)).astype(BF16)
    out_ref[0] = jnp.dot(sc, w_ref[0], preferred_element_type=F32) + b_ref[0]


def _modulation(c_all, ada_w_bf, ada_b):
    n = c_all.shape[0]
    return pl.pallas_call(
        _mod_kernel,
        out_shape=jax.ShapeDtypeStruct((DEPTH, n, 3 * D_MODEL), F32),
        grid=(DEPTH, 3),
        in_specs=[
            pl.BlockSpec((n, D_MODEL), lambda l, j: (0, 0)),
            pl.BlockSpec((1, D_MODEL, D_MODEL), lambda l, j: (l, 0, j)),
            pl.BlockSpec((1, 1, D_MODEL), lambda l, j: (l, 0, j)),
        ],
        out_specs=pl.BlockSpec((1, n, D_MODEL), lambda l, j: (l, 0, j)),
        compiler_params=_cparams(2),
        name="adaln_mod",
    )(c_all, ada_w_bf, ada_b.reshape(DEPTH, 1, 3 * D_MODEL))


def _pre_kernel(x_ref, shift_ref, scale_ref, ng_ref, wstd_ref, wt_ref,
                gqa_col_ref, wuqT_ref, wuqswT_ref, gkv_row_ref, gkv_col_ref, wukp_ref, wuvT_ref,
                gq_col_ref, gqsw_col_ref, gk_row_ref, gksw_row_ref,
                mkc_ref, mks_ref, mqc_ref, mqs_ref,
                dkc_ref, dks_ref, dqc_ref, dqs_ref,
                gkc_ref, gks_ref, gqc_ref, gqs_ref,
                qt_mla_ref, k_mla_ref, vt_mla_ref,
                qt_diff_ref, k_diff_ref, vt_diff_ref,
                qt_gqa_ref, k_gqa_ref, vt_gqa_ref,
                xc_ref, g_ref):
    ts = x_ref.shape[1]
    x = x_ref[0]
    ms = jnp.mean(x * x, axis=-1, keepdims=True)
    h = (x * lax.rsqrt(ms + EPS)) * ng_ref[...]
    h = h * (1.0 + scale_ref[0]) + shift_ref[0]
    hb = h.astype(BF16)
    z = jnp.dot(hb, wstd_ref[...], preferred_element_type=F32)
    zt = lax.dot_general(wt_ref[...], hb, (((1,), (1,)), ((), ())),
                         preferred_element_type=F32)

    qlt = zt[ZT_QA:ZT_QA + Q_LORA]
    msq = jnp.mean(qlt * qlt, axis=0, keepdims=True)
    qnb = ((qlt * lax.rsqrt(msq + EPS)) * gqa_col_ref[...]).astype(BF16)
    qat = jnp.dot(wuqT_ref[...], qnb, preferred_element_type=F32)
    qbt = jnp.dot(wuqswT_ref[...], qnb, preferred_element_type=F32)
    sc_mla = (NOPE_A + ROPE_A) ** -0.5 * LOG2E
    dq = NOPE_A + ROPE_A
    pad = jnp.zeros((LANES - dq, ts), F32)
    for hh in range(H_A):
        nope = qat[dq * hh:dq * hh + NOPE_A]
        pe = qat[dq * hh + NOPE_A:dq * (hh + 1)]
        sw = qbt[ROPE_A * hh:ROPE_A * (hh + 1)]
        pe = pe * mqc_ref[...] + sw * mqs_ref[...]
        blk = jnp.concatenate([nope, pe, pad], axis=0) * sc_mla
        qt_mla_ref[0, LANES * hh:LANES * (hh + 1), :] = blk.astype(BF16)

    kvl = z[:, ZS_KVL:ZS_KVL + KV_LORA]
    msk = jnp.mean(kvl * kvl, axis=-1, keepdims=True)
    kvn = ((kvl * lax.rsqrt(msk + EPS)) * gkv_row_ref[...]).astype(BF16)
    knope = jnp.dot(kvn, wukp_ref[...], preferred_element_type=F32)
    kpe = z[:, ZS_KPE:ZS_KPE + LANES] * mkc_ref[...] + z[:, ZS_KPE_SW:ZS_KPE_SW + LANES] * mks_ref[...]
    for hh in range(H_A):
        k_mla_ref[0, :, LANES * hh:LANES * (hh + 1)] = (
            knope[:, LANES * hh:LANES * (hh + 1)] + kpe).astype(BF16)
    kvlt = zt[ZT_KVL:ZT_KVL + KV_LORA]
    mst = jnp.mean(kvlt * kvlt, axis=0, keepdims=True)
    kvnt = ((kvlt * lax.rsqrt(mst + EPS)) * gkv_col_ref[...]).astype(BF16)
    vt_mla_ref[0, 0] = jnp.dot(wuvT_ref[...], kvnt, preferred_element_type=F32).astype(BF16)

    kdf = z[:, ZS_KB:ZS_KB + 256] * dkc_ref[...] + z[:, ZS_KB_SW:ZS_KB_SW + 256] * dks_ref[...]
    k_diff_ref[0] = kdf.astype(BF16)
    sc_diff = DK_B ** -0.5 * LOG2E
    for m in range(2 * H_B):
        q = zt[ZT_QB + DK_B * m:ZT_QB + DK_B * (m + 1)]
        sw = zt[ZT_QB_SW + DK_B * m:ZT_QB_SW + DK_B * (m + 1)]
        qq = (q * dqc_ref[...] + sw * dqs_ref[...]) * sc_diff
        qt_diff_ref[0, DK_B * m:DK_B * (m + 1), :] = qq.astype(BF16)
    vt_diff_ref[0, 0] = zt[ZT_VB:ZT_VB + 256].astype(BF16)

    kg = z[:, ZS_KD:ZS_KD + LANES]
    kgsw = z[:, ZS_KD_SW:ZS_KD_SW + LANES]
    k2 = kg * kg
    lo = lax.broadcasted_iota(jnp.int32, (ts, LANES), 1) < HD_D
    s_lo = jnp.sum(jnp.where(lo, k2, 0.0), axis=-1, keepdims=True)
    s_hi = jnp.sum(jnp.where(lo, 0.0, k2), axis=-1, keepdims=True)
    rk = lax.rsqrt(jnp.where(lo, s_lo, s_hi) / HD_D + EPS)
    kn = (kg * rk) * gk_row_ref[...]
    knsw = (kgsw * rk) * gksw_row_ref[...]
    k_gqa_ref[0] = (kn * gkc_ref[...] + knsw * gks_ref[...]).astype(BF16)
    sc_gqa = HD_D ** -0.5 * LOG2E
    for hh in range(H_D):
        q = zt[ZT_QD + HD_D * hh:ZT_QD + HD_D * (hh + 1)]
        sw = zt[ZT_QD_SW + HD_D * hh:ZT_QD_SW + HD_D * (hh + 1)]
        r = lax.rsqrt(jnp.mean(q * q, axis=0, keepdims=True) + EPS)
        qn = (q * r) * gq_col_ref[...]
        qsw = (sw * r) * gqsw_col_ref[...]
        qq = (qn * gqc_ref[...] + qsw * gqs_ref[...]) * sc_gqa
        qt_gqa_ref[0, HD_D * hh:HD_D * (hh + 1), :] = qq.astype(BF16)
    vt_gqa_ref[0, 0] = zt[ZT_VD:ZT_VD + 128].astype(BF16)

    xc_ref[0] = z[:, ZS_XC:ZS_XC + W_GROUP]
    g = z[:, ZS_G:ZS_END]
    g_ref[0] = (g * jax.nn.sigmoid(g)).astype(BF16)


def _pre_call(x, shift, scale, lw, tb):
    b, s, d = x.shape
    ts = TOK_TILE
    nt = s // ts
    full2 = lambda a: pl.BlockSpec(a.shape, lambda i, bb: (0, 0))
    tok_rows = lambda w: pl.BlockSpec((ts, w), lambda i, bb: (i, 0))
    tok_cols = lambda r: pl.BlockSpec((r, ts), lambda i, bb: (0, i))
    params = [lw["norm_g"], lw["w_std"], lw["w_t"],
              lw["gqa_col"], lw["w_uqT"], lw["w_uqswT"], lw["gkv_row"], lw["gkv_col"], lw["w_ukp"], lw["w_uvT"],
              lw["gq_col"], lw["gqsw_col"], lw["gk_row"], lw["gksw_row"]]
    in_specs = [pl.BlockSpec((1, ts, d), lambda i, bb: (bb, i, 0)),
                pl.BlockSpec((1, 1, d), lambda i, bb: (bb, 0, 0)),
                pl.BlockSpec((1, 1, d), lambda i, bb: (bb, 0, 0))]
    in_specs += [full2(p) for p in params]
    in_specs += [tok_rows(128), tok_rows(128), tok_cols(32), tok_cols(32),
                 tok_rows(256), tok_rows(256), tok_cols(32), tok_cols(32),
                 tok_rows(128), tok_rows(128), tok_cols(64), tok_cols(64)]
    tables = [tb["mkc"], tb["mks"], tb["mqc"], tb["mqs"],
              tb["dkc"], tb["dks"], tb["dqc"], tb["dqs"],
              tb["gkc"], tb["gks"], tb["gqc"], tb["gqs"]]

    def qt_spec(rows):
        return pl.BlockSpec((1, rows, ts), lambda i, bb: (bb, 0, i))

    def k_spec(w):
        return pl.BlockSpec((1, ts, w), lambda i, bb: (bb, i, 0))

    def vt_spec(rows):
        return pl.BlockSpec((1, 1, rows, ts), lambda i, bb: (bb, i, 0, 0))

    out_shape = [
        jax.ShapeDtypeStruct((b, 512, s), BF16), jax.ShapeDtypeStruct((b, s, 512), BF16),
        jax.ShapeDtypeStruct((b, nt, 256, ts), BF16),
        jax.ShapeDtypeStruct((b, 256, s), BF16), jax.ShapeDtypeStruct((b, s, 256), BF16),
        jax.ShapeDtypeStruct((b, nt, 256, ts), BF16),
        jax.ShapeDtypeStruct((b, 256, s), BF16), jax.ShapeDtypeStruct((b, s, 128), BF16),
        jax.ShapeDtypeStruct((b, nt, 128, ts), BF16),
        jax.ShapeDtypeStruct((b, s, 256), F32), jax.ShapeDtypeStruct((b, s, 1024), BF16),
    ]
    out_specs = [qt_spec(512), k_spec(512), vt_spec(256),
                 qt_spec(256), k_spec(256), vt_spec(256),
                 qt_spec(256), k_spec(128), vt_spec(128),
                 k_spec(256), k_spec(1024)]
    return pl.pallas_call(
        _pre_kernel, out_shape=out_shape, grid=(nt, b), in_specs=in_specs, out_specs=out_specs,
        compiler_params=_cparams(2), name="pre_proj",
    )(x, shift, scale, *params, *tables)


def _attn_kernel(*refs, maps, n_chunks, tk, kind, lam_init):
    if kind == "diff":
        qt_ref, k_ref, vt_ref, lam_ref, subln_ref, out_ref = refs
    else:
        qt_ref, k_ref, vt_ref, out_ref = refs
    tq = qt_ref.shape[2]
    dv = 64

    ws = []
    for (kg, q0, qn, woff, v0) in maps:
        q = qt_ref[0, q0:q0 + qn, :]
        pieces = []
        if woff > 0:
            pieces.append(jnp.zeros((woff, tq), BF16))
        pieces.append(q)
        if LANES - woff - qn > 0:
            pieces.append(jnp.zeros((LANES - woff - qn, tq), BF16))
        ws.append(pieces[0] if len(pieces) == 1 else jnp.concatenate(pieces, axis=0))

    def body(c, carry):
        r0 = pl.multiple_of(c * tk, tk)
        new = []
        for i, (kg, q0, qn, woff, v0) in enumerate(maps):
            m, l, acc = carry[i]
            kc = k_ref[0, pl.ds(r0, tk), LANES * kg:LANES * (kg + 1)]
            s = jnp.dot(kc, ws[i], preferred_element_type=F32)
            mn = jnp.maximum(m, jnp.max(s, axis=0, keepdims=True))
            p = jnp.exp2(s - mn)
            alpha = jnp.exp2(m - mn)
            l = alpha * l + jnp.sum(p, axis=0, keepdims=True)
            vc = vt_ref[0, c, v0:v0 + dv, :]
            acc = alpha * acc + jnp.dot(vc, p.astype(BF16), preferred_element_type=F32)
            new.append((mn, l, acc))
        return tuple(new)

    init = tuple((jnp.full((1, tq), -jnp.inf, F32), jnp.zeros((1, tq), F32), jnp.zeros((dv, tq), F32))
                 for _ in maps)
    fin = lax.fori_loop(0, n_chunks, body, init)
    os_ = [acc / l for (_, l, acc) in fin]

    if kind == "diff":
        lf = lam_ref[...]
        s1 = jnp.sum(lf[0:1] * lf[1:2], axis=-1, keepdims=True)
        s2 = jnp.sum(lf[2:3] * lf[3:4], axis=-1, keepdims=True)
        lam_full = jnp.exp(s1) - jnp.exp(s2) + lam_init
        heads = []
        for hh in range(H_B):
            o = os_[2 * hh] - lam_full * os_[2 * hh + 1]
            r = lax.rsqrt(jnp.mean(o * o, axis=0, keepdims=True) + EPS)
            heads.append(((o * r) * subln_ref[...]) * (1.0 - lam_init))
        ot = jnp.concatenate(heads, axis=0)
    else:
        ot = jnp.concatenate(os_, axis=0)
    out_ref[0] = ot.T


def _attn_call(qt, k, vt, maps, kind, extra=(), lam_init=0.0, name="attn"):
    b, fq, s = qt.shape
    fk = k.shape[2]
    _, nt, fv, tk = vt.shape
    tq = Q_TILE
    in_specs = [pl.BlockSpec((1, fq, tq), lambda bb, qi: (bb, 0, qi)),
                pl.BlockSpec((1, s, fk), lambda bb, qi: (bb, 0, 0)),
                pl.BlockSpec((1, nt, fv, tk), lambda bb, qi: (bb, 0, 0, 0))]
    in_specs += [pl.BlockSpec(e.shape, lambda bb, qi: (0, 0)) for e in extra]
    kern = functools.partial(_attn_kernel, maps=tuple(maps), n_chunks=nt, tk=tk, kind=kind,
                             lam_init=lam_init)
    return pl.pallas_call(
        kern, out_shape=jax.ShapeDtypeStruct((b, s, W_GROUP), F32),
        grid=(b, s // tq), in_specs=in_specs,
        out_specs=pl.BlockSpec((1, tq, W_GROUP), lambda bb, qi: (bb, qi, 0)),
        compiler_params=_cparams(2), name=name,
    )(qt, k, vt, *extra)


MAPS_MLA = [(hh, LANES * hh, LANES, 0, V_A * hh) for hh in range(H_A)]
MAPS_DIFF = [(m // 4, DK_B * m, DK_B, DK_B * (m % 4), DV_B * (m // 2)) for m in range(2 * H_B)]
MAPS_GQA = [(0, HD_D * hh, HD_D, HD_D * (hh // (H_D // KV_H_D)), HD_D * (hh // (H_D // KV_H_D)))
            for hh in range(H_D)]


def _softplus(x):
    return jnp.maximum(x, 0.0) + jnp.log1p(jnp.exp(-jnp.abs(x)))


def _lru_kernel(xc_ref, cw_ref, cb_ref, wbd_ref, bias_ref, lam_ref, out_ref,
                xpad, af, gf, ar, gr, hr, *, s, tc):
    zero8 = jnp.zeros((SUBLANES, LANES), F32)
    xpad[0:SUBLANES, :] = zero8
    xpad[s + SUBLANES:s + 2 * SUBLANES, :] = zero8
    xpad[SUBLANES:s + SUBLANES, :] = xc_ref[0]
    lam = lam_ref[0]
    sp_f = _softplus(-lam[0:1])
    sp_r = _softplus(-lam[1:2])
    cw = cw_ref[...]
    cb = cb_ref[...]
    wbd = wbd_ref[0]
    bias = bias_ref[0]
    n = tc + 2 * SUBLANES

    def gates(pre_r, pre_i, sp, xconv):
        r = jax.nn.sigmoid(pre_r)
        i = jax.nn.sigmoid(pre_i)
        log_a = (-LRU_C * r) * sp
        a = jnp.exp(log_a)
        gx = jnp.sqrt(-jnp.tanh(log_a) * (a * a + 1.0)) * (i * xconv)
        return a, gx

    def chunk(ci, carry):
        t0 = pl.multiple_of(ci * tc, tc)
        xw = xpad[pl.ds(t0, n), :]
        taps = [pltpu.roll(xw, 2, 0), pltpu.roll(xw, 1, 0), xw, pltpu.roll(xw, n - 1, 0)]
        xconv = cb + taps[0][SUBLANES:SUBLANES + tc] * cw[0:1]
        for j in range(1, CONV_W):
            xconv = xconv + taps[j][SUBLANES:SUBLANES + tc] * cw[j:j + 1]
        pre = jnp.dot(xconv.astype(BF16), wbd, preferred_element_type=F32) + bias
        a, gx = gates(pre[:, 0:128], pre[:, 128:256], sp_f, xconv)
        af[pl.ds(t0, tc), :] = a
        gf[pl.ds(t0, tc), :] = gx
        a, gx = gates(pre[:, 256:384], pre[:, 384:512], sp_r, xconv)
        ar[pl.ds(t0, tc), :] = a
        gr[pl.ds(t0, tc), :] = gx
        return carry

    lax.fori_loop(0, s // tc, chunk, 0)

    row = lax.broadcasted_iota(jnp.int32, (SUBLANES, LANES), 0)
    ng = s // SUBLANES

    def scan8(a, bv, reverse):
        for dd in (1, 2, 4):
            if reverse:
                ok = row < SUBLANES - dd
                sh = SUBLANES - dd
            else:
                ok = row >= dd
                sh = dd
            a_s = jnp.where(ok, pltpu.roll(a, sh, 0), 1.0)
            b_s = jnp.where(ok, pltpu.roll(bv, sh, 0), 0.0)
            bv = a * b_s + bv
            a = a * a_s
        return a, bv

    def scan_body(gi, carry):
        hf_prev, hr_prev = carry
        t0 = pl.multiple_of(gi * SUBLANES, SUBLANES)
        a, bv = scan8(af[pl.ds(t0, SUBLANES), :], gf[pl.ds(t0, SUBLANES), :], False)
        hf = bv + a * hf_prev
        out_ref[0, pl.ds(t0, SUBLANES), :] = hf
        t1 = pl.multiple_of((ng - 1 - gi) * SUBLANES, SUBLANES)
        a, bv = scan8(ar[pl.ds(t1, SUBLANES), :], gr[pl.ds(t1, SUBLANES), :], True)
        hrv = bv + a * hr_prev
        hr[pl.ds(t1, SUBLANES), :] = hrv
        return (jnp.broadcast_to(hf[SUBLANES - 1:SUBLANES], (SUBLANES, LANES)),
                jnp.broadcast_to(hrv[0:1], (SUBLANES, LANES)))

    lax.fori_loop(0, ng, scan_body, (zero8, zero8), unroll=4)
    out_ref[0] = out_ref[0] + hr[...]


def _lru_call(xc, lw):
    b, s, _ = xc.shape
    kern = functools.partial(_lru_kernel, s=s, tc=LRU_CHUNK)
    seq = pltpu.VMEM((s, LANES), F32)
    return pl.pallas_call(
        kern, out_shape=jax.ShapeDtypeStruct((b, s, W_GROUP), F32),
        grid=(b, W_GROUP // LANES),
        in_specs=[pl.BlockSpec((1, s, LANES), lambda bb, hf: (bb, 0, hf)),
                  pl.BlockSpec((CONV_W, LANES), lambda bb, hf: (0, hf)),
                  pl.BlockSpec((1, LANES), lambda bb, hf: (0, hf)),
                  pl.BlockSpec((1, LANES, 4 * LANES), lambda bb, hf: (hf, 0, 0)),
                  pl.BlockSpec((1, 1, 4 * LANES), lambda bb, hf: (hf, 0, 0)),
                  pl.BlockSpec((1, 2, LANES), lambda bb, hf: (hf, 0, 0))],
        out_specs=pl.BlockSpec((1, s, LANES), lambda bb, hf: (bb, 0, hf)),
        scratch_shapes=[pltpu.VMEM((s + 2 * SUBLANES, LANES), F32), seq, seq, seq, seq, seq],
        compiler_params=_cparams(2), name="rglru",
    )(xc, lw["conv_w"], lw["conv_b"], lw["lru_wbd"], lw["lru_bias"], lw["lru_lam"])


def _post_kernel(x_ref, oa_ref, ob_ref, oc_ref, od_ref, g_ref, gate_ref, wout_ref, fn_ref, out_ref, *, final):
    y = None
    for i, o_ref in enumerate((oa_ref, ob_ref, oc_ref, od_ref)):
        og = (o_ref[0] * g_ref[0, :, W_GROUP * i:W_GROUP * (i + 1)].astype(F32)).astype(BF16)
        part = jnp.dot(og, wout_ref[W_GROUP * i:W_GROUP * (i + 1), :], preferred_element_type=F32)
        y = part if y is None else y + part
    xn = x_ref[0] + gate_ref[0] * y
    if final:
        ms = jnp.mean(xn * xn, axis=-1, keepdims=True)
        xn = (xn * lax.rsqrt(ms + EPS)) * fn_ref[...]
    out_ref[0] = xn


def _post_call(x, oa, ob, oc, od, g, gate, w_out_bf, final_norm, final):
    b, s, d = x.shape
    ts = TOK_TILE
    tok = lambda w: pl.BlockSpec((1, ts, w), lambda bb, i: (bb, i, 0))
    return pl.pallas_call(
        functools.partial(_post_kernel, final=final),
        out_shape=jax.ShapeDtypeStruct((b, s, d), F32),
        grid=(b, s // ts),
        in_specs=[tok(d), tok(W_GROUP), tok(W_GROUP), tok(W_GROUP), tok(W_GROUP), tok(4 * W_GROUP),
                  pl.BlockSpec((1, 1, d), lambda bb, i: (bb, 0, 0)),
                  pl.BlockSpec((d, d), lambda bb, i: (0, 0)),
                  pl.BlockSpec((1, d), lambda bb, i: (0, 0))],
        out_specs=tok(d),
        compiler_params=_cparams(2), name="post_proj",
    )(x, oa, ob, oc, od, g, gate, w_out_bf, final_norm)


def _angles(pos, half, theta):
    inv = jnp.power(jnp.float32(theta), -jnp.arange(half, dtype=F32) / half)
    return pos[:, None] * inv[None, :]


def _tables(s):
    pos = jnp.arange(s, dtype=F32)
    rows = s // GRID_W
    row = jnp.repeat(jnp.arange(rows, dtype=F32), GRID_W)
    col = jnp.tile(jnp.arange(GRID_W, dtype=F32), rows)
    a = _angles(pos, ROPE_A // 2, MLA_ROPE_THETA)
    c, sn = jnp.cos(a), jnp.sin(a)
    z64, z32 = jnp.zeros((s, 64), F32), jnp.zeros((s, 32), F32)
    tb = {"mkc": jnp.concatenate([z64, c, c, z32], axis=1),
          "mks": jnp.concatenate([z64, -sn, sn, z32], axis=1),
          "mqc": jnp.concatenate([c, c], axis=1).T,
          "mqs": jnp.concatenate([-sn, sn], axis=1).T}
    a = _angles(pos, DK_B // 8, ROPE_THETA)
    c, sn = jnp.cos(a), jnp.sin(a)
    c32 = jnp.concatenate([c, c, jnp.ones((s, DK_B - DK_B // 4), F32)], axis=1)
    s32 = jnp.concatenate([-sn, sn, jnp.zeros((s, DK_B - DK_B // 4), F32)], axis=1)
    tb.update({"dkc": jnp.tile(c32, (1, 2 * H_B)), "dks": jnp.tile(s32, (1, 2 * H_B)),
               "dqc": c32.T, "dqs": s32.T})
    ar_ = _angles(row, HD_D // 4, AXIAL_THETA)
    ac_ = _angles(col, HD_D // 4, AXIAL_THETA)
    cr, sr, cc, sc = jnp.cos(ar_), jnp.sin(ar_), jnp.cos(ac_), jnp.sin(ac_)
    c64 = jnp.concatenate([cr, cr, cc, cc], axis=1)
    s64 = jnp.concatenate([-sr, sr, -sc, sc], axis=1)
    tb.update({"gkc": jnp.tile(c64, (1, KV_H_D)), "gks": jnp.tile(s64, (1, KV_H_D)),
               "gqc": c64.T, "gqs": s64.T})
    return tb


def _blockdiag_pair(w, half):
    z = jnp.zeros((BW_C, BW_C), F32)
    top = jnp.concatenate([w[2 * half], z], axis=1)
    bot = jnp.concatenate([z, w[2 * half + 1]], axis=1)
    return jnp.concatenate([top, bot], axis=0)


def _layer_weights(l, p):
    w = p["w_in"][l]
    col = lambda o, n: w[:, o:o + n]
    z64 = jnp.zeros((D_MODEL, 64), F32)
    z32 = jnp.zeros((D_MODEL, 32), F32)
    kra, kb, kd = col(O_KRA, ROPE_A), col(O_KB, 256), col(O_KD, 128)
    qb, qd = col(O_QB, 256), col(O_QD, 256)
    w_std = jnp.concatenate([
        col(O_KVA, KV_LORA),
        z64, kra, z32,
        z64, kra[:, P_MLA], z32,
        kb, kb[:, P_DIFF],
        kd, kd[:, P_GQA_K],
        col(O_XC, W_GROUP),
        col(O_GA, W_GROUP), col(O_GB, W_GROUP), col(O_GC, W_GROUP), col(O_GD, W_GROUP)], axis=1)
    w_t = jnp.concatenate([
        col(O_QA, Q_LORA), col(O_KVA, KV_LORA),
        qb, qb[:, P_DIFF], col(O_VB, 256),
        qd, qd[:, P_GQA_Q], col(O_VD, 128)], axis=1).T
    wuq = p["mla_w_uq"][l]
    dq = NOPE_A + ROPE_A
    pe_sw = jnp.concatenate([wuq[:, dq * h + NOPE_A:dq * (h + 1)][:, P_MLA] for h in range(H_A)], axis=1)
    wukv = p["mla_w_ukv"][l]
    dkv = NOPE_A + V_A
    z_k = jnp.zeros((KV_LORA, LANES - NOPE_A), F32)
    w_ukp = jnp.concatenate(
        [jnp.concatenate([wukv[:, dkv * h:dkv * h + NOPE_A], z_k], axis=1) for h in range(H_A)], axis=1)
    w_uv = jnp.concatenate([wukv[:, dkv * h + NOPE_A:dkv * (h + 1)] for h in range(H_A)], axis=1)
    gq, gk = p["gqa_q_norm"][l], p["gqa_k_norm"][l]
    wa, wx = p["lru_wa"][l], p["lru_wx"][l]
    ba, bx = p["lru_ba"][l], p["lru_bx"][l]
    wbd = jnp.stack([jnp.concatenate([_blockdiag_pair(wa[0], hf), _blockdiag_pair(wx[0], hf),
                                      _blockdiag_pair(wa[1], hf), _blockdiag_pair(wx[1], hf)], axis=1)
                     for hf in range(2)])
    sl = lambda v, hf: v[LANES * hf:LANES * (hf + 1)]
    bias = jnp.stack([jnp.concatenate([sl(ba[0], hf), sl(bx[0], hf), sl(ba[1], hf), sl(bx[1], hf)])[None]
                      for hf in range(2)])
    lam = p["lru_lambda"][l]
    lam_h = jnp.stack([jnp.stack([sl(lam[0], hf), sl(lam[1], hf)]) for hf in range(2)])
    return {
        "norm_g": p["norm_g"][l][None], "w_std": w_std.astype(BF16), "w_t": w_t.astype(BF16),
        "gqa_col": p["mla_q_norm"][l][:, None], "w_uqT": wuq.T.astype(BF16), "w_uqswT": pe_sw.T.astype(BF16),
        "gkv_row": p["mla_kv_norm"][l][None], "gkv_col": p["mla_kv_norm"][l][:, None],
        "w_ukp": w_ukp.astype(BF16), "w_uvT": w_uv.T.astype(BF16),
        "gq_col": gq[:, None], "gqsw_col": gq[P_GQA_HEAD][:, None],
        "gk_row": jnp.tile(gk, KV_H_D)[None], "gksw_row": jnp.tile(gk[P_GQA_HEAD], KV_H_D)[None],
        "diff_lambda": p["diff_lambda"][l], "subln_col": p["diff_subln"][l][:, None],
        "conv_w": p["lru_conv_w"][l], "conv_b": p["lru_conv_b"][l][None],
        "lru_wbd": wbd.astype(BF16), "lru_bias": bias, "lru_lam": lam_h,
        "w_out": p["w_out"][l].astype(BF16),
    }


def _trunk(x, mods, lws, final_norm):
    b, s, d = x.shape
    tb = _tables(s)
    for l in range(DEPTH):
        lw = lws[l]
        shift, scale, gate = mods[l]
        (qt_mla, k_mla, vt_mla, qt_diff, k_diff, vt_diff, qt_gqa, k_gqa, vt_gqa, xc, g) = _pre_call(
            x, shift, scale, lw, tb)
        lam_init = 0.8 - 0.6 * math.exp(-0.3 * l)
        oa = _attn_call(qt_mla, k_mla, vt_mla, MAPS_MLA, "mla", name="attn_mla")
        ob = _attn_call(qt_diff, k_diff, vt_diff, MAPS_DIFF, "diff",
                        extra=(lw["diff_lambda"], lw["subln_col"]), lam_init=lam_init, name="attn_diff")
        od = _attn_call(qt_gqa, k_gqa, vt_gqa, MAPS_GQA, "gqa", name="attn_gqa")
        oc = _lru_call(xc, lw)
        x = _post_call(x, oa, ob, oc, od, g, gate, lw["w_out"], final_norm, final=(l == DEPTH - 1))
    return x


def kernel(x_prompt, x_sample, c_prompt, c_sample, ada_w, ada_b, norm_g, w_in, mla_q_norm, mla_w_uq, mla_kv_norm, mla_w_ukv, diff_lambda, diff_subln, lru_conv_w, lru_conv_b, lru_wa, lru_ba, lru_wx, lru_bx, lru_lambda, gqa_q_norm, gqa_k_norm, w_out, final_norm):
    p = dict(norm_g=norm_g, w_in=w_in, mla_q_norm=mla_q_norm, mla_w_uq=mla_w_uq, mla_kv_norm=mla_kv_norm,
             mla_w_ukv=mla_w_ukv, diff_lambda=diff_lambda, diff_subln=diff_subln,
             lru_conv_w=lru_conv_w, lru_conv_b=lru_conv_b, lru_wa=lru_wa, lru_ba=lru_ba,
             lru_wx=lru_wx, lru_bx=lru_bx, lru_lambda=lru_lambda,
             gqa_q_norm=gqa_q_norm, gqa_k_norm=gqa_k_norm, w_out=w_out)
    lws = [_layer_weights(l, p) for l in range(DEPTH)]
    nb = x_prompt.shape[0]
    mod = _modulation(jnp.concatenate([c_prompt, c_sample], axis=0), ada_w.astype(BF16), ada_b)

    def mods_for(lo, hi):
        out = []
        for l in range(DEPTH):
            m = mod[l, lo:hi]
            out.append(tuple(m[:, D_MODEL * j:D_MODEL * (j + 1)][:, None, :] for j in range(3)))
        return out

    fn = final_norm[None]
    y_prompt = _trunk(x_prompt, mods_for(0, nb), lws, fn)
    y_sample = _trunk(x_sample, mods_for(nb, nb + x_sample.shape[0]), lws, fn)
    return (y_prompt, y_sample)
```

```python
import functools
import math

import jax
import jax.numpy as jnp
import numpy as np
from jax import lax
from jax.experimental import pallas as pl
from jax.experimental.pallas import tpu as pltpu

F32 = jnp.float32
BF16 = jnp.bfloat16

D_MODEL = 1024
DEPTH = 2
W_GROUP = 256
H_A, NOPE_A, ROPE_A, V_A = 4, 64, 32, 64
Q_LORA, KV_LORA = 192, 128
H_B, DK_B, DV_B = 4, 32, 64
H_C, BW_C, CONV_W = 4, 64, 4
LRU_C = 8.0
H_D, KV_H_D, HD_D = 4, 2, 64
GRID_W = 64
ROPE_THETA = 500000.0
MLA_ROPE_THETA = 10000.0
AXIAL_THETA = 10000.0
EPS = 1e-6
LOG2E = 1.4426950408889634

IN_SIZES = (Q_LORA, KV_LORA, ROPE_A, W_GROUP,
            H_B * 2 * DK_B, H_B * 2 * DK_B, H_B * DV_B, W_GROUP,
            W_GROUP, W_GROUP,
            H_D * HD_D, KV_H_D * HD_D, KV_H_D * HD_D, W_GROUP)
_OFF = np.concatenate([[0], np.cumsum(IN_SIZES)]).tolist()
(O_QA, O_KVA, O_KRA, O_GA, O_QB, O_KB, O_VB, O_GB, O_XC, O_GC, O_QD, O_KD, O_VD, O_GD, _) = _OFF

LANES = 128
SUBLANES = 8

TOK_TILE = 512
Q_TILE = 256
LRU_CHUNK = 512
VMEM_LIMIT = 56 * 1024 * 1024

ZS_KVL, ZS_KPE, ZS_KPE_SW, ZS_KB, ZS_KB_SW, ZS_KD, ZS_KD_SW, ZS_XC, ZS_G, ZS_END = (
    0, 128, 256, 384, 640, 896, 1024, 1152, 1408, 2432)
ZT_QA, ZT_KVL, ZT_QB, ZT_QB_SW, ZT_VB, ZT_QD, ZT_QD_SW, ZT_VD, ZT_END = (
    0, 192, 320, 576, 832, 1088, 1344, 1600, 1728)


def _partner(width, block, rot, half):
    idx = np.arange(width)
    d = idx % block
    out = idx.copy()
    in_rot = d < rot
    first = in_rot & ((d % (2 * half)) < half)
    second = in_rot & ~((d % (2 * half)) < half)
    out[first] = idx[first] + half
    out[second] = idx[second] - half
    return out


P_MLA = _partner(ROPE_A, ROPE_A, ROPE_A, ROPE_A // 2)
P_DIFF = _partner(2 * H_B * DK_B, DK_B, DK_B // 4, DK_B // 8)
P_GQA_Q = _partner(H_D * HD_D, HD_D // 2, HD_D // 2, HD_D // 4)
P_GQA_K = _partner(KV_H_D * HD_D, HD_D // 2, HD_D // 2, HD_D // 4)
P_GQA_HEAD = _partner(HD_D, HD_D // 2, HD_D // 2, HD_D // 4)


def _cparams(n_axes):
    return pltpu.CompilerParams(dimension_semantics=("arbitrary",) * n_axes,
                                vmem_limit_bytes=VMEM_LIMIT)


def _mod_kernel(c_ref, w_ref, b_ref, out_ref):
    c = c_ref[...]
    sc = (c * jax.nn.sigmoid(c)).astype(BF16)
    out_ref[0] = jnp.dot(sc, w_ref[0], preferred_element_type=F32) + b_ref[0]


def _modulation(c_all, ada_w_bf, ada_b):
    n = c_all.shape[0]
    return pl.pallas_call(
        _mod_kernel,
        out_shape=jax.ShapeDtypeStruct((DEPTH, n, 3 * D_MODEL), F32),
        grid=(DEPTH, 3),
        in_specs=[
            pl.BlockSpec((n, D_MODEL), lambda l, j: (0, 0)),
            pl.BlockSpec((1, D_MODEL, D_MODEL), lambda l, j: (l, 0, j)),
            pl.BlockSpec((1, 1, D_MODEL), lambda l, j: (l, 0, j)),
        ],
        out_specs=pl.BlockSpec((1, n, D_MODEL), lambda l, j: (l, 0, j)),
        compiler_params=_cparams(2),
        name="adaln_mod",
    )(c_all, ada_w_bf, ada_b.reshape(DEPTH, 1, 3 * D_MODEL))


def _pre_kernel(x_ref, shift_ref, scale_ref, ng_ref, wstd_ref, wt_ref,
                gqa_col_ref, wuqT_ref, wuqswT_ref, gkv_row_ref, gkv_col_ref, wukp_ref, wuvT_ref,
                gq_col_ref, gqsw_col_ref, gk_row_ref, gksw_row_ref,
                mkc_ref, mks_ref, mqc_ref, mqs_ref,
                dkc_ref, dks_ref, dqc_ref, dqs_ref,
                gkc_ref, gks_ref, gqc_ref, gqs_ref,
                qt_mla_ref, k_mla_ref, vt_mla_ref,
                qt_diff_ref, k_diff_ref, vt_diff_ref,
                qt_gqa_ref, k_gqa_ref, vt_gqa_ref,
                xc_ref, g_ref):
    ts = x_ref.shape[1]
    x = x_ref[0]
    ms = jnp.mean(x * x, axis=-1, keepdims=True)
    h = (x * lax.rsqrt(ms + EPS)) * ng_ref[...]
    h = h * (1.0 + scale_ref[0]) + shift_ref[0]
    hb = h.astype(BF16)
    z = jnp.dot(hb, wstd_ref[...], preferred_element_type=F32)
    zt = lax.dot_general(wt_ref[...], hb, (((1,), (1,)), ((), ())),
                         preferred_element_type=F32)

    qlt = zt[ZT_QA:ZT_QA + Q_LORA]
    msq = jnp.mean(qlt * qlt, axis=0, keepdims=True)
    qnb = ((qlt * lax.rsqrt(msq + EPS)) * gqa_col_ref[...]).astype(BF16)
    qat = jnp.dot(wuqT_ref[...], qnb, preferred_element_type=F32)
    qbt = jnp.dot(wuqswT_ref[...], qnb, preferred_element_type=F32)
    sc_mla = (NOPE_A + ROPE_A) ** -0.5 * LOG2E
    dq = NOPE_A + ROPE_A
    pad = jnp.zeros((LANES - dq, ts), F32)
    for hh in range(H_A):
        nope = qat[dq * hh:dq * hh + NOPE_A]
        pe = qat[dq * hh + NOPE_A:dq * (hh + 1)]
        sw = qbt[ROPE_A * hh:ROPE_A * (hh + 1)]
        pe = pe * mqc_ref[...] + sw * mqs_ref[...]
        blk = jnp.concatenate([nope, pe, pad], axis=0) * sc_mla
        qt_mla_ref[0, LANES * hh:LANES * (hh + 1), :] = blk.astype(BF16)

    kvl = z[:, ZS_KVL:ZS_KVL + KV_LORA]
    msk = jnp.mean(kvl * kvl, axis=-1, keepdims=True)
    kvn = ((kvl * lax.rsqrt(msk + EPS)) * gkv_row_ref[...]).astype(BF16)
    knope = jnp.dot(kvn, wukp_ref[...], preferred_element_type=F32)
    kpe = z[:, ZS_KPE:ZS_KPE + LANES] * mkc_ref[...] + z[:, ZS_KPE_SW:ZS_KPE_SW + LANES] * mks_ref[...]
    for hh in range(H_A):
        k_mla_ref[0, :, LANES * hh:LANES * (hh + 1)] = (
            knope[:, LANES * hh:LANES * (hh + 1)] + kpe).astype(BF16)
    kvlt = zt[ZT_KVL:ZT_KVL + KV_LORA]
    mst = jnp.mean(kvlt * kvlt, axis=0, keepdims=True)
    kvnt = ((kvlt * lax.rsqrt(mst + EPS)) * gkv_col_ref[...]).astype(BF16)
    vt_mla_ref[0, 0] = jnp.dot(wuvT_ref[...], kvnt, preferred_element_type=F32).astype(BF16)

    kdf = z[:, ZS_KB:ZS_KB + 256] * dkc_ref[...] + z[:, ZS_KB_SW:ZS_KB_SW + 256] * dks_ref[...]
    k_diff_ref[0] = kdf.astype(BF16)
    sc_diff = DK_B ** -0.5 * LOG2E
    for m in range(2 * H_B):
        q = zt[ZT_QB + DK_B * m:ZT_QB + DK_B * (m + 1)]
        sw = zt[ZT_QB_SW + DK_B * m:ZT_QB_SW + DK_B * (m + 1)]
        qq = (q * dqc_ref[...] + sw * dqs_ref[...]) * sc_diff
        qt_diff_ref[0, DK_B * m:DK_B * (m + 1), :] = qq.astype(BF16)
    vt_diff_ref[0, 0] = zt[ZT_VB:ZT_VB + 256].astype(BF16)

    kg = z[:, ZS_KD:ZS_KD + LANES]
    kgsw = z[:, ZS_KD_SW:ZS_KD_SW + LANES]
    k2 = kg * kg
    lo = lax.broadcasted_iota(jnp.int32, (ts, LANES), 1) < HD_D
    s_lo = jnp.sum(jnp.where(lo, k2, 0.0), axis=-1, keepdims=True)
    s_hi = jnp.sum(jnp.where(lo, 0.0, k2), axis=-1, keepdims=True)
    rk = lax.rsqrt(jnp.where(lo, s_lo, s_hi) / HD_D + EPS)
    kn = (kg * rk) * gk_row_ref[...]
    knsw = (kgsw * rk) * gksw_row_ref[...]
    k_gqa_ref[0] = (kn * gkc_ref[...] + knsw * gks_ref[...]).astype(BF16)
    sc_gqa = HD_D ** -0.5 * LOG2E
    for hh in range(H_D):
        q = zt[ZT_QD + HD_D * hh:ZT_QD + HD_D * (hh + 1)]
        sw = zt[ZT_QD_SW + HD_D * hh:ZT_QD_SW + HD_D * (hh + 1)]
        r = lax.rsqrt(jnp.mean(q * q, axis=0, keepdims=True) + EPS)
        qn = (q * r) * gq_col_ref[...]
        qsw = (sw * r) * gqsw_col_ref[...]
        qq = (qn * gqc_ref[...] + qsw * gqs_ref[...]) * sc_gqa
        qt_gqa_ref[0, HD_D * hh:HD_D * (hh + 1), :] = qq.astype(BF16)
    vt_gqa_ref[0, 0] = zt[ZT_VD:ZT_VD + 128].astype(BF16)

    xc_ref[0] = z[:, ZS_XC:ZS_XC + W_GROUP]
    g = z[:, ZS_G:ZS_END]
    g_ref[0] = (g * jax.nn.sigmoid(g)).astype(BF16)


def _pre_call(x, shift, scale, lw, tb):
    b, s, d = x.shape
    ts = TOK_TILE
    nt = s // ts
    full2 = lambda a: pl.BlockSpec(a.shape, lambda i, bb: (0, 0))
    tok_rows = lambda w: pl.BlockSpec((ts, w), lambda i, bb: (i, 0))
    tok_cols = lambda r: pl.BlockSpec((r, ts), lambda i, bb: (0, i))
    params = [lw["norm_g"], lw["w_std"], lw["w_t"],
              lw["gqa_col"], lw["w_uqT"], lw["w_uqswT"], lw["gkv_row"], lw["gkv_col"], lw["w_ukp"], lw["w_uvT"],
              lw["gq_col"], lw["gqsw_col"], lw["gk_row"], lw["gksw_row"]]
    in_specs = [pl.BlockSpec((1, ts, d), lambda i, bb: (bb, i, 0)),
                pl.BlockSpec((1, 1, d), lambda i, bb: (bb, 0, 0)),
                pl.BlockSpec((1, 1, d), lambda i, bb: (bb, 0, 0))]
    in_specs += [full2(p) for p in params]
    in_specs += [tok_rows(128), tok_rows(128), tok_cols(32), tok_cols(32),
                 tok_rows(256), tok_rows(256), tok_cols(32), tok_cols(32),
                 tok_rows(128), tok_rows(128), tok_cols(64), tok_cols(64)]
    tables = [tb["mkc"], tb["mks"], tb["mqc"], tb["mqs"],
              tb["dkc"], tb["dks"], tb["dqc"], tb["dqs"],
              tb["gkc"], tb["gks"], tb["gqc"], tb["gqs"]]

    def qt_spec(rows):
        return pl.BlockSpec((1, rows, ts), lambda i, bb: (bb, 0, i))

    def k_spec(w):
        return pl.BlockSpec((1, ts, w), lambda i, bb: (bb, i, 0))

    def vt_spec(rows):
        return pl.BlockSpec((1, 1, rows, ts), lambda i, bb: (bb, i, 0, 0))

    out_shape = [
        jax.ShapeDtypeStruct((b, 512, s), BF16), jax.ShapeDtypeStruct((b, s, 512), BF16),
        jax.ShapeDtypeStruct((b, nt, 256, ts), BF16),
        jax.ShapeDtypeStruct((b, 256, s), BF16), jax.ShapeDtypeStruct((b, s, 256), BF16),
        jax.ShapeDtypeStruct((b, nt, 256, ts), BF16),
        jax.ShapeDtypeStruct((b, 256, s), BF16), jax.ShapeDtypeStruct((b, s, 128), BF16),
        jax.ShapeDtypeStruct((b, nt, 128, ts), BF16),
        jax.ShapeDtypeStruct((b, s, 256), F32), jax.ShapeDtypeStruct((b, s, 1024), BF16),
    ]
    out_specs = [qt_spec(512), k_spec(512), vt_spec(256),
                 qt_spec(256), k_spec(256), vt_spec(256),
                 qt_spec(256), k_spec(128), vt_spec(128),
                 k_spec(256), k_spec(1024)]
    return pl.pallas_call(
        _pre_kernel, out_shape=out_shape, grid=(nt, b), in_specs=in_specs, out_specs=out_specs,
        compiler_params=_cparams(2), name="pre_proj",
    )(x, shift, scale, *params, *tables)


def _attn_kernel(*refs, maps, n_chunks, tk, kind, lam_init):
    if kind == "diff":
        qt_ref, k_ref, vt_ref, lam_ref, subln_ref, out_ref, w_scr, s_scr, p_scr = refs
    else:
        qt_ref, k_ref, vt_ref, out_ref, w_scr, s_scr, p_scr = refs
    tq = qt_ref.shape[2]
    dv = 64
    nm = len(maps)
    assert n_chunks >= 2 and n_chunks % 2 == 0

    for i, (kg, q0, qn, woff, v0) in enumerate(maps):
        q = qt_ref[0, q0:q0 + qn, :]
        pieces = []
        if woff > 0:
            pieces.append(jnp.zeros((woff, tq), BF16))
        pieces.append(q)
        if LANES - woff - qn > 0:
            pieces.append(jnp.zeros((LANES - woff - qn, tq), BF16))
        w_scr[i] = pieces[0] if len(pieces) == 1 else jnp.concatenate(pieces, axis=0)

    def stage_scores(i, c, slot, m):
        kg = maps[i][0]
        r0 = c * tk if isinstance(c, int) else pl.multiple_of(c * tk, tk)
        s = jnp.dot(k_ref[0, pl.ds(r0, tk), LANES * kg:LANES * (kg + 1)], w_scr[i],
                    preferred_element_type=F32)
        s_scr[i, slot] = s
        cmax = jnp.max(s, axis=0, keepdims=True)
        return cmax if m is None else jnp.maximum(m, cmax)

    def stage_probs(i, slot, m, a, l):
        p = jnp.exp2(s_scr[i, slot] - m)
        p_scr[i, slot] = p.astype(BF16)
        return a * l + jnp.sum(p, axis=0, keepdims=True)

    def stage_pv(i, c, slot, a_old, acc):
        v0 = maps[i][4]
        return a_old * acc + jnp.dot(vt_ref[0, c, v0:v0 + dv, :], p_scr[i, slot],
                                     preferred_element_type=F32)

    def step(state, c, slot, with_pv=True, with_scores=True):
        new = []
        for i in range(nm):
            m_old, m, l, acc, a_old = state[i]
            a = jnp.exp2(m_old - m)
            l = stage_probs(i, slot, m, a, l)
            if with_pv:
                acc = stage_pv(i, c - 1, 1 - slot, a_old, acc)
            m_next = stage_scores(i, c + 1, 1 - slot, m) if with_scores else m
            new.append((m, m_next, l, acc, a))
        return tuple(new)

    state = []
    for i in range(nm):
        m0 = stage_scores(i, 0, 0, None)
        state.append((m0, m0, jnp.zeros((1, tq), F32), jnp.zeros((dv, tq), F32), jnp.ones((1, tq), F32)))
    state = step(tuple(state), 0, 0, with_pv=False)

    def pair(j, st):
        c = 1 + 2 * j
        return step(step(st, c, 1), c + 1, 0)

    state = lax.fori_loop(0, (n_chunks - 2) // 2, pair, state)
    state = step(state, n_chunks - 1, 1, with_scores=False)
    ls, accs = [], []
    for i in range(nm):
        _, _, l, acc, a_old = state[i]
        ls.append(l)
        accs.append(stage_pv(i, n_chunks - 1, 1, a_old, acc))

    os_ = [acc / l for l, acc in zip(ls, accs)]
    if kind == "diff":
        lf = lam_ref[...]
        s1 = jnp.sum(lf[0:1] * lf[1:2], axis=-1, keepdims=True)
        s2 = jnp.sum(lf[2:3] * lf[3:4], axis=-1, keepdims=True)
        lam_full = jnp.exp(s1) - jnp.exp(s2) + lam_init
        heads = []
        for hh in range(H_B):
            o = os_[2 * hh] - lam_full * os_[2 * hh + 1]
            r = lax.rsqrt(jnp.mean(o * o, axis=0, keepdims=True) + EPS)
            heads.append(((o * r) * subln_ref[...]) * (1.0 - lam_init))
        ot = jnp.concatenate(heads, axis=0)
    else:
        ot = jnp.concatenate(os_, axis=0)
    out_ref[0] = ot.T


def _attn_call(qt, k, vt, maps, kind, extra=(), lam_init=0.0, name="attn"):
    b, fq, s = qt.shape
    fk = k.shape[2]
    _, nt, fv, tk = vt.shape
    tq = Q_TILE
    in_specs = [pl.BlockSpec((1, fq, tq), lambda bb, qi: (bb, 0, qi)),
                pl.BlockSpec((1, s, fk), lambda bb, qi: (bb, 0, 0)),
                pl.BlockSpec((1, nt, fv, tk), lambda bb, qi: (bb, 0, 0, 0))]
    in_specs += [pl.BlockSpec(e.shape, lambda bb, qi: (0, 0)) for e in extra]
    kern = functools.partial(_attn_kernel, maps=tuple(maps), n_chunks=nt, tk=tk, kind=kind,
                             lam_init=lam_init)
    return pl.pallas_call(
        kern, out_shape=jax.ShapeDtypeStruct((b, s, W_GROUP), F32),
        grid=(b, s // tq), in_specs=in_specs,
        out_specs=pl.BlockSpec((1, tq, W_GROUP), lambda bb, qi: (bb, qi, 0)),
        scratch_shapes=[pltpu.VMEM((len(maps), LANES, tq), BF16),
                        pltpu.VMEM((len(maps), 2, tk, tq), F32),
                        pltpu.VMEM((len(maps), 2, tk, tq), BF16)],
        compiler_params=_cparams(2), name=name,
    )(qt, k, vt, *extra)


MAPS_MLA = [(hh, LANES * hh, LANES, 0, V_A * hh) for hh in range(H_A)]
MAPS_DIFF = [(m // 4, DK_B * m, DK_B, DK_B * (m % 4), DV_B * (m // 2)) for m in range(2 * H_B)]
MAPS_GQA = [(0, HD_D * hh, HD_D, HD_D * (hh // (H_D // KV_H_D)), HD_D * (hh // (H_D // KV_H_D)))
            for hh in range(H_D)]


def _softplus(x):
    return jnp.maximum(x, 0.0) + jnp.log1p(jnp.exp(-jnp.abs(x)))


def _lru_kernel(xc_ref, cw_ref, cb_ref, wbd_ref, bias_ref, lam_ref, out_ref,
                xpad, af, gf, ar, gr, hr, *, s, tc):
    zero8 = jnp.zeros((SUBLANES, LANES), F32)
    xpad[0:SUBLANES, :] = zero8
    xpad[s + SUBLANES:s + 2 * SUBLANES, :] = zero8
    xpad[SUBLANES:s + SUBLANES, :] = xc_ref[0]
    lam = lam_ref[0]
    sp_f = _softplus(-lam[0:1])
    sp_r = _softplus(-lam[1:2])
    cw = cw_ref[...]
    cb = cb_ref[...]
    wbd = wbd_ref[0]
    bias = bias_ref[0]
    n = tc + 2 * SUBLANES

    def gates(pre_r, pre_i, sp, xconv):
        r = jax.nn.sigmoid(pre_r)
        i = jax.nn.sigmoid(pre_i)
        log_a = (-LRU_C * r) * sp
        a = jnp.exp(log_a)
        gx = jnp.sqrt(-jnp.tanh(log_a) * (a * a + 1.0)) * (i * xconv)
        return a, gx

    def chunk(ci, carry):
        t0 = pl.multiple_of(ci * tc, tc)
        xw = xpad[pl.ds(t0, n), :]
        taps = [pltpu.roll(xw, 2, 0), pltpu.roll(xw, 1, 0), xw, pltpu.roll(xw, n - 1, 0)]
        xconv = cb + taps[0][SUBLANES:SUBLANES + tc] * cw[0:1]
        for j in range(1, CONV_W):
            xconv = xconv + taps[j][SUBLANES:SUBLANES + tc] * cw[j:j + 1]
        pre = jnp.dot(xconv.astype(BF16), wbd, preferred_element_type=F32) + bias
        a, gx = gates(pre[:, 0:128], pre[:, 128:256], sp_f, xconv)
        af[pl.ds(t0, tc), :] = a
        gf[pl.ds(t0, tc), :] = gx
        a, gx = gates(pre[:, 256:384], pre[:, 384:512], sp_r, xconv)
        ar[pl.ds(t0, tc), :] = a
        gr[pl.ds(t0, tc), :] = gx
        return carry

    lax.fori_loop(0, s // tc, chunk, 0)

    row = lax.broadcasted_iota(jnp.int32, (SUBLANES, LANES), 0)
    ng = s // SUBLANES

    def scan8(a, bv, reverse):
        for dd in (1, 2, 4):
            if reverse:
                ok = row < SUBLANES - dd
                sh = SUBLANES - dd
            else:
                ok = row >= dd
                sh = dd
            a_s = jnp.where(ok, pltpu.roll(a, sh, 0), 1.0)
            b_s = jnp.where(ok, pltpu.roll(bv, sh, 0), 0.0)
            bv = a * b_s + bv
            a = a * a_s
        return a, bv

    def scan_body(gi, carry):
        hf_prev, hr_prev = carry
        t0 = pl.multiple_of(gi * SUBLANES, SUBLANES)
        a, bv = scan8(af[pl.ds(t0, SUBLANES), :], gf[pl.ds(t0, SUBLANES), :], False)
        hf = bv + a * hf_prev
        out_ref[0, pl.ds(t0, SUBLANES), :] = hf
        t1 = pl.multiple_of((ng - 1 - gi) * SUBLANES, SUBLANES)
        a, bv = scan8(ar[pl.ds(t1, SUBLANES), :], gr[pl.ds(t1, SUBLANES), :], True)
        hrv = bv + a * hr_prev
        hr[pl.ds(t1, SUBLANES), :] = hrv
        return (jnp.broadcast_to(hf[SUBLANES - 1:SUBLANES], (SUBLANES, LANES)),
                jnp.broadcast_to(hrv[0:1], (SUBLANES, LANES)))

    lax.fori_loop(0, ng, scan_body, (zero8, zero8), unroll=4)
    out_ref[0] = out_ref[0] + hr[...]


def _lru_call(xc, lw):
    b, s, _ = xc.shape
    kern = functools.partial(_lru_kernel, s=s, tc=LRU_CHUNK)
    seq = pltpu.VMEM((s, LANES), F32)
    return pl.pallas_call(
        kern, out_shape=jax.ShapeDtypeStruct((b, s, W_GROUP), F32),
        grid=(b, W_GROUP // LANES),
        in_specs=[pl.BlockSpec((1, s, LANES), lambda bb, hf: (bb, 0, hf)),
                  pl.BlockSpec((CONV_W, LANES), lambda bb, hf: (0, hf)),
                  pl.BlockSpec((1, LANES), lambda bb, hf: (0, hf)),
                  pl.BlockSpec((1, LANES, 4 * LANES), lambda bb, hf: (hf, 0, 0)),
                  pl.BlockSpec((1, 1, 4 * LANES), lambda bb, hf: (hf, 0, 0)),
                  pl.BlockSpec((1, 2, LANES), lambda bb, hf: (hf, 0, 0))],
        out_specs=pl.BlockSpec((1, s, LANES), lambda bb, hf: (bb, 0, hf)),
        scratch_shapes=[pltpu.VMEM((s + 2 * SUBLANES, LANES), F32), seq, seq, seq, seq, seq],
        compiler_params=_cparams(2), name="rglru",
    )(xc, lw["conv_w"], lw["conv_b"], lw["lru_wbd"], lw["lru_bias"], lw["lru_lam"])


def _post_kernel(x_ref, oa_ref, ob_ref, oc_ref, od_ref, g_ref, gate_ref, wout_ref, fn_ref, out_ref, *, final):
    y = None
    for i, o_ref in enumerate((oa_ref, ob_ref, oc_ref, od_ref)):
        og = (o_ref[0] * g_ref[0, :, W_GROUP * i:W_GROUP * (i + 1)].astype(F32)).astype(BF16)
        part = jnp.dot(og, wout_ref[W_GROUP * i:W_GROUP * (i + 1), :], preferred_element_type=F32)
        y = part if y is None else y + part
    xn = x_ref[0] + gate_ref[0] * y
    if final:
        ms = jnp.mean(xn * xn, axis=-1, keepdims=True)
        xn = (xn * lax.rsqrt(ms + EPS)) * fn_ref[...]
    out_ref[0] = xn


def _post_call(x, oa, ob, oc, od, g, gate, w_out_bf, final_norm, final):
    b, s, d = x.shape
    ts = TOK_TILE
    tok = lambda w: pl.BlockSpec((1, ts, w), lambda bb, i: (bb, i, 0))
    return pl.pallas_call(
        functools.partial(_post_kernel, final=final),
        out_shape=jax.ShapeDtypeStruct((b, s, d), F32),
        grid=(b, s // ts),
        in_specs=[tok(d), tok(W_GROUP), tok(W_GROUP), tok(W_GROUP), tok(W_GROUP), tok(4 * W_GROUP),
                  pl.BlockSpec((1, 1, d), lambda bb, i: (bb, 0, 0)),
                  pl.BlockSpec((d, d), lambda bb, i: (0, 0)),
                  pl.BlockSpec((1, d), lambda bb, i: (0, 0))],
        out_specs=tok(d),
        compiler_params=_cparams(2), name="post_proj",
    )(x, oa, ob, oc, od, g, gate, w_out_bf, final_norm)


def _angles(pos, half, theta):
    inv = jnp.power(jnp.float32(theta), -jnp.arange(half, dtype=F32) / half)
    return pos[:, None] * inv[None, :]


def _tables(s):
    pos = jnp.arange(s, dtype=F32)
    rows = s // GRID_W
    row = jnp.repeat(jnp.arange(rows, dtype=F32), GRID_W)
    col = jnp.tile(jnp.arange(GRID_W, dtype=F32), rows)
    a = _angles(pos, ROPE_A // 2, MLA_ROPE_THETA)
    c, sn = jnp.cos(a), jnp.sin(a)
    z64, z32 = jnp.zeros((s, 64), F32), jnp.zeros((s, 32), F32)
    tb = {"mkc": jnp.concatenate([z64, c, c, z32], axis=1),
          "mks": jnp.concatenate([z64, -sn, sn, z32], axis=1),
          "mqc": jnp.concatenate([c, c], axis=1).T,
          "mqs": jnp.concatenate([-sn, sn], axis=1).T}
    a = _angles(pos, DK_B // 8, ROPE_THETA)
    c, sn = jnp.cos(a), jnp.sin(a)
    c32 = jnp.concatenate([c, c, jnp.ones((s, DK_B - DK_B // 4), F32)], axis=1)
    s32 = jnp.concatenate([-sn, sn, jnp.zeros((s, DK_B - DK_B // 4), F32)], axis=1)
    tb.update({"dkc": jnp.tile(c32, (1, 2 * H_B)), "dks": jnp.tile(s32, (1, 2 * H_B)),
               "dqc": c32.T, "dqs": s32.T})
    ar_ = _angles(row, HD_D // 4, AXIAL_THETA)
    ac_ = _angles(col, HD_D // 4, AXIAL_THETA)
    cr, sr, cc, sc = jnp.cos(ar_), jnp.sin(ar_), jnp.cos(ac_), jnp.sin(ac_)
    c64 = jnp.concatenate([cr, cr, cc, cc], axis=1)
    s64 = jnp.concatenate([-sr, sr, -sc, sc], axis=1)
    tb.update({"gkc": jnp.tile(c64, (1, KV_H_D)), "gks": jnp.tile(s64, (1, KV_H_D)),
               "gqc": c64.T, "gqs": s64.T})
    return tb


def _blockdiag_pair(w, half):
    z = jnp.zeros((BW_C, BW_C), F32)
    top = jnp.concatenate([w[2 * half], z], axis=1)
    bot = jnp.concatenate([z, w[2 * half + 1]], axis=1)
    return jnp.concatenate([top, bot], axis=0)


def _layer_weights(l, p):
    w = p["w_in"][l]
    col = lambda o, n: w[:, o:o + n]
    z64 = jnp.zeros((D_MODEL, 64), F32)
    z32 = jnp.zeros((D_MODEL, 32), F32)
    kra, kb, kd = col(O_KRA, ROPE_A), col(O_KB, 256), col(O_KD, 128)
    qb, qd = col(O_QB, 256), col(O_QD, 256)
    w_std = jnp.concatenate([
        col(O_KVA, KV_LORA),
        z64, kra, z32,
        z64, kra[:, P_MLA], z32,
        kb, kb[:, P_DIFF],
        kd, kd[:, P_GQA_K],
        col(O_XC, W_GROUP),
        col(O_GA, W_GROUP), col(O_GB, W_GROUP), col(O_GC, W_GROUP), col(O_GD, W_GROUP)], axis=1)
    w_t = jnp.concatenate([
        col(O_QA, Q_LORA), col(O_KVA, KV_LORA),
        qb, qb[:, P_DIFF], col(O_VB, 256),
        qd, qd[:, P_GQA_Q], col(O_VD, 128)], axis=1).T
    wuq = p["mla_w_uq"][l]
    dq = NOPE_A + ROPE_A
    pe_sw = jnp.concatenate([wuq[:, dq * h + NOPE_A:dq * (h + 1)][:, P_MLA] for h in range(H_A)], axis=1)
    wukv = p["mla_w_ukv"][l]
    dkv = NOPE_A + V_A
    z_k = jnp.zeros((KV_LORA, LANES - NOPE_A), F32)
    w_ukp = jnp.concatenate(
        [jnp.concatenate([wukv[:, dkv * h:dkv * h + NOPE_A], z_k], axis=1) for h in range(H_A)], axis=1)
    w_uv = jnp.concatenate([wukv[:, dkv * h + NOPE_A:dkv * (h + 1)] for h in range(H_A)], axis=1)
    gq, gk = p["gqa_q_norm"][l], p["gqa_k_norm"][l]
    wa, wx = p["lru_wa"][l], p["lru_wx"][l]
    ba, bx = p["lru_ba"][l], p["lru_bx"][l]
    wbd = jnp.stack([jnp.concatenate([_blockdiag_pair(wa[0], hf), _blockdiag_pair(wx[0], hf),
                                      _blockdiag_pair(wa[1], hf), _blockdiag_pair(wx[1], hf)], axis=1)
                     for hf in range(2)])
    sl = lambda v, hf: v[LANES * hf:LANES * (hf + 1)]
    bias = jnp.stack([jnp.concatenate([sl(ba[0], hf), sl(bx[0], hf), sl(ba[1], hf), sl(bx[1], hf)])[None]
                      for hf in range(2)])
    lam = p["lru_lambda"][l]
    lam_h = jnp.stack([jnp.stack([sl(lam[0], hf), sl(lam[1], hf)]) for hf in range(2)])
    return {
        "norm_g": p["norm_g"][l][None], "w_std": w_std.astype(BF16), "w_t": w_t.astype(BF16),
        "gqa_col": p["mla_q_norm"][l][:, None], "w_uqT": wuq.T.astype(BF16), "w_uqswT": pe_sw.T.astype(BF16),
        "gkv_row": p["mla_kv_norm"][l][None], "gkv_col": p["mla_kv_norm"][l][:, None],
        "w_ukp": w_ukp.astype(BF16), "w_uvT": w_uv.T.astype(BF16),
        "gq_col": gq[:, None], "gqsw_col": gq[P_GQA_HEAD][:, None],
        "gk_row": jnp.tile(gk, KV_H_D)[None], "gksw_row": jnp.tile(gk[P_GQA_HEAD], KV_H_D)[None],
        "diff_lambda": p["diff_lambda"][l], "subln_col": p["diff_subln"][l][:, None],
        "conv_w": p["lru_conv_w"][l], "conv_b": p["lru_conv_b"][l][None],
        "lru_wbd": wbd.astype(BF16), "lru_bias": bias, "lru_lam": lam_h,
        "w_out": p["w_out"][l].astype(BF16),
    }


def _trunk(x, mods, lws, final_norm):
    b, s, d = x.shape
    tb = _tables(s)
    for l in range(DEPTH):
        lw = lws[l]
        shift, scale, gate = mods[l]
        (qt_mla, k_mla, vt_mla, qt_diff, k_diff, vt_diff, qt_gqa, k_gqa, vt_gqa, xc, g) = _pre_call(
            x, shift, scale, lw, tb)
        lam_init = 0.8 - 0.6 * math.exp(-0.3 * l)
        oa = _attn_call(qt_mla, k_mla, vt_mla, MAPS_MLA, "mla", name="attn_mla")
        ob = _attn_call(qt_diff, k_diff, vt_diff, MAPS_DIFF, "diff",
                        extra=(lw["diff_lambda"], lw["subln_col"]), lam_init=lam_init, name="attn_diff")
        od = _attn_call(qt_gqa, k_gqa, vt_gqa, MAPS_GQA, "gqa", name="attn_gqa")
        oc = _lru_call(xc, lw)
        x = _post_call(x, oa, ob, oc, od, g, gate, lw["w_out"], final_norm, final=(l == DEPTH - 1))
    return x


def kernel(x_prompt, x_sample, c_prompt, c_sample, ada_w, ada_b, norm_g, w_in, mla_q_norm, mla_w_uq, mla_kv_norm, mla_w_ukv, diff_lambda, diff_subln, lru_conv_w, lru_conv_b, lru_wa, lru_ba, lru_wx, lru_bx, lru_lambda, gqa_q_norm, gqa_k_norm, w_out, final_norm):
    p = dict(norm_g=norm_g, w_in=w_in, mla_q_norm=mla_q_norm, mla_w_uq=mla_w_uq, mla_kv_norm=mla_kv_norm,
             mla_w_ukv=mla_w_ukv, diff_lambda=diff_lambda, diff_subln=diff_subln,
             lru_conv_w=lru_conv_w, lru_conv_b=lru_conv_b, lru_wa=lru_wa, lru_ba=lru_ba,
             lru_wx=lru_wx, lru_bx=lru_bx, lru_lambda=lru_lambda,
             gqa_q_norm=gqa_q_norm, gqa_k_norm=gqa_k_norm, w_out=w_out)
    lws = [_layer_weights(l, p) for l in range(DEPTH)]
    nb = x_prompt.shape[0]
    mod = _modulation(jnp.concatenate([c_prompt, c_sample], axis=0), ada_w.astype(BF16), ada_b)

    def mods_for(lo, hi):
        out = []
        for l in range(DEPTH):
            m = mod[l, lo:hi]
            out.append(tuple(m[:, D_MODEL * j:D_MODEL * (j + 1)][:, None, :] for j in range(3)))
        return out

    fn = final_norm[None]
    y_prompt = _trunk(x_prompt, mods_for(0, nb), lws, fn)
    y_sample = _trunk(x_sample, mods_for(nb, nb + x_sample.shape[0]), lws, fn)
    return (y_prompt, y_sample)
```

```python
import functools
import math

import jax
import jax.numpy as jnp
import numpy as np
from jax import lax
from jax.experimental import pallas as pl
from jax.experimental.pallas import tpu as pltpu

F32 = jnp.float32
BF16 = jnp.bfloat16

D_MODEL = 1024
DEPTH = 2
W_GROUP = 256
H_A, NOPE_A, ROPE_A, V_A = 4, 64, 32, 64
Q_LORA, KV_LORA = 192, 128
H_B, DK_B, DV_B = 4, 32, 64
H_C, BW_C, CONV_W = 4, 64, 4
LRU_C = 8.0
H_D, KV_H_D, HD_D = 4, 2, 64
GRID_W = 64
ROPE_THETA = 500000.0
MLA_ROPE_THETA = 10000.0
AXIAL_THETA = 10000.0
EPS = 1e-6
LOG2E = 1.4426950408889634

IN_SIZES = (Q_LORA, KV_LORA, ROPE_A, W_GROUP,
            H_B * 2 * DK_B, H_B * 2 * DK_B, H_B * DV_B, W_GROUP,
            W_GROUP, W_GROUP,
            H_D * HD_D, KV_H_D * HD_D, KV_H_D * HD_D, W_GROUP)
_OFF = np.concatenate([[0], np.cumsum(IN_SIZES)]).tolist()
(O_QA, O_KVA, O_KRA, O_GA, O_QB, O_KB, O_VB, O_GB, O_XC, O_GC, O_QD, O_KD, O_VD, O_GD, _) = _OFF

LANES = 128
SUBLANES = 8

TOK_TILE = 512
Q_TILE = 512
LRU_CHUNK = 512
ONES_ROWS = 2 * SUBLANES
DENOM_LOG2_RANGE = 80
SUB_KEYS = 256
SCORE_LEAD = 2
VMEM_LIMIT = 56 * 1024 * 1024

ZS_KVL, ZS_KPE, ZS_KPE_SW, ZS_KB, ZS_KB_SW, ZS_KD, ZS_KD_SW, ZS_XC, ZS_G, ZS_END = (
    0, 128, 256, 384, 640, 896, 1024, 1152, 1408, 2432)
ZT_QA, ZT_KVL, ZT_QB, ZT_QB_SW, ZT_VB, ZT_QD, ZT_QD_SW, ZT_VD, ZT_END = (
    0, 192, 320, 576, 832, 1088, 1344, 1600, 1728)


def _partner(width, block, rot, half):
    idx = np.arange(width)
    d = idx % block
    out = idx.copy()
    in_rot = d < rot
    first = in_rot & ((d % (2 * half)) < half)
    second = in_rot & ~((d % (2 * half)) < half)
    out[first] = idx[first] + half
    out[second] = idx[second] - half
    return out


P_MLA = _partner(ROPE_A, ROPE_A, ROPE_A, ROPE_A // 2)
P_DIFF = _partner(2 * H_B * DK_B, DK_B, DK_B // 4, DK_B // 8)
P_GQA_Q = _partner(H_D * HD_D, HD_D // 2, HD_D // 2, HD_D // 4)
P_GQA_K = _partner(KV_H_D * HD_D, HD_D // 2, HD_D // 2, HD_D // 4)
P_GQA_HEAD = _partner(HD_D, HD_D // 2, HD_D // 2, HD_D // 4)


def _indicator(width, seg):
    m = np.zeros((width, LANES), np.float32)
    m[np.arange(width), np.arange(width) // seg] = 1.0
    return m


IND_MLA = _indicator(H_A * LANES, LANES)
IND_DIFF = _indicator(2 * H_B * DK_B, DK_B)
IND_GQA = _indicator(KV_H_D * HD_D, HD_D)


def _cparams(n_axes):
    return pltpu.CompilerParams(dimension_semantics=("arbitrary",) * n_axes,
                                vmem_limit_bytes=VMEM_LIMIT)


def _mod_kernel(c_ref, w_ref, b_ref, out_ref):
    c = c_ref[...]
    sc = (c * jax.nn.sigmoid(c)).astype(BF16)
    out_ref[0] = jnp.dot(sc, w_ref[0], preferred_element_type=F32) + b_ref[0]


def _modulation(c_all, ada_w_bf, ada_b):
    n = c_all.shape[0]
    return pl.pallas_call(
        _mod_kernel,
        out_shape=jax.ShapeDtypeStruct((DEPTH, n, 3 * D_MODEL), F32),
        grid=(DEPTH, 3),
        in_specs=[
            pl.BlockSpec((n, D_MODEL), lambda l, j: (0, 0)),
            pl.BlockSpec((1, D_MODEL, D_MODEL), lambda l, j: (l, 0, j)),
            pl.BlockSpec((1, 1, D_MODEL), lambda l, j: (l, 0, j)),
        ],
        out_specs=pl.BlockSpec((1, n, D_MODEL), lambda l, j: (l, 0, j)),
        compiler_params=_cparams(2),
        name="adaln_mod",
    )(c_all, ada_w_bf, ada_b.reshape(DEPTH, 1, 3 * D_MODEL))


def _pre_kernel(x_ref, shift_ref, scale_ref, ng_ref, wstd_ref, wt_ref,
                gqa_col_ref, wuqT_ref, wuqswT_ref, gkv_row_ref, gkv_col_ref, wukp_ref, wuvT_ref,
                gq_col_ref, gqsw_col_ref, gk_row_ref, gksw_row_ref,
                mkc_ref, mks_ref, mqc_ref, mqs_ref,
                dkc_ref, dks_ref, dqc_ref, dqs_ref,
                gkc_ref, gks_ref, gqc_ref, gqs_ref,
                ind_mla_ref, ind_diff_ref, ind_gqa_ref,
                qt_mla_ref, k_mla_ref, vt_mla_ref,
                qt_diff_ref, k_diff_ref, vt_diff_ref,
                qt_gqa_ref, k_gqa_ref, vt_gqa_ref,
                xc_ref, g_ref, kn_mla_ref, kn_diff_ref, kn_gqa_ref):
    ts = x_ref.shape[1]

    def key_norms(kb, ind_ref, out_ref):
        kf = kb.astype(F32)
        r = jnp.dot((kf * kf).astype(BF16), ind_ref[...], preferred_element_type=F32)
        out_ref[0, 0] = jnp.broadcast_to(jnp.max(r, axis=0, keepdims=True), (SUBLANES, LANES))

    x = x_ref[0]
    ms = jnp.mean(x * x, axis=-1, keepdims=True)
    h = (x * lax.rsqrt(ms + EPS)) * ng_ref[...]
    h = h * (1.0 + scale_ref[0]) + shift_ref[0]
    hb = h.astype(BF16)
    z = jnp.dot(hb, wstd_ref[...], preferred_element_type=F32)
    zt = lax.dot_general(wt_ref[...], hb, (((1,), (1,)), ((), ())),
                         preferred_element_type=F32)

    qlt = zt[ZT_QA:ZT_QA + Q_LORA]
    msq = jnp.mean(qlt * qlt, axis=0, keepdims=True)
    qnb = ((qlt * lax.rsqrt(msq + EPS)) * gqa_col_ref[...]).astype(BF16)
    qat = jnp.dot(wuqT_ref[...], qnb, preferred_element_type=F32)
    qbt = jnp.dot(wuqswT_ref[...], qnb, preferred_element_type=F32)
    sc_mla = (NOPE_A + ROPE_A) ** -0.5 * LOG2E
    dq = NOPE_A + ROPE_A
    pad = jnp.zeros((LANES - dq, ts), F32)
    for hh in range(H_A):
        nope = qat[dq * hh:dq * hh + NOPE_A]
        pe = qat[dq * hh + NOPE_A:dq * (hh + 1)]
        sw = qbt[ROPE_A * hh:ROPE_A * (hh + 1)]
        pe = pe * mqc_ref[...] + sw * mqs_ref[...]
        blk = jnp.concatenate([nope, pe, pad], axis=0) * sc_mla
        qt_mla_ref[0, LANES * hh:LANES * (hh + 1), :] = blk.astype(BF16)

    kvl = z[:, ZS_KVL:ZS_KVL + KV_LORA]
    msk = jnp.mean(kvl * kvl, axis=-1, keepdims=True)
    kvn = ((kvl * lax.rsqrt(msk + EPS)) * gkv_row_ref[...]).astype(BF16)
    knope = jnp.dot(kvn, wukp_ref[...], preferred_element_type=F32)
    kpe = z[:, ZS_KPE:ZS_KPE + LANES] * mkc_ref[...] + z[:, ZS_KPE_SW:ZS_KPE_SW + LANES] * mks_ref[...]
    kb = jnp.concatenate([(knope[:, LANES * hh:LANES * (hh + 1)] + kpe).astype(BF16) for hh in range(H_A)],
                         axis=1)
    k_mla_ref[0] = kb
    key_norms(kb, ind_mla_ref, kn_mla_ref)
    kvlt = zt[ZT_KVL:ZT_KVL + KV_LORA]
    mst = jnp.mean(kvlt * kvlt, axis=0, keepdims=True)
    kvnt = ((kvlt * lax.rsqrt(mst + EPS)) * gkv_col_ref[...]).astype(BF16)
    vt_mla_ref[0, 0] = jnp.dot(wuvT_ref[...], kvnt, preferred_element_type=F32).astype(BF16)

    kdf = z[:, ZS_KB:ZS_KB + 256] * dkc_ref[...] + z[:, ZS_KB_SW:ZS_KB_SW + 256] * dks_ref[...]
    kb = kdf.astype(BF16)
    k_diff_ref[0] = kb
    key_norms(kb, ind_diff_ref, kn_diff_ref)
    sc_diff = DK_B ** -0.5 * LOG2E
    for m in range(2 * H_B):
        q = zt[ZT_QB + DK_B * m:ZT_QB + DK_B * (m + 1)]
        sw = zt[ZT_QB_SW + DK_B * m:ZT_QB_SW + DK_B * (m + 1)]
        qq = (q * dqc_ref[...] + sw * dqs_ref[...]) * sc_diff
        qt_diff_ref[0, DK_B * m:DK_B * (m + 1), :] = qq.astype(BF16)
    vt_diff_ref[0, 0] = zt[ZT_VB:ZT_VB + 256].astype(BF16)

    kg = z[:, ZS_KD:ZS_KD + LANES]
    kgsw = z[:, ZS_KD_SW:ZS_KD_SW + LANES]
    k2 = kg * kg
    lo = lax.broadcasted_iota(jnp.int32, (ts, LANES), 1) < HD_D
    s_lo = jnp.sum(jnp.where(lo, k2, 0.0), axis=-1, keepdims=True)
    s_hi = jnp.sum(jnp.where(lo, 0.0, k2), axis=-1, keepdims=True)
    rk = lax.rsqrt(jnp.where(lo, s_lo, s_hi) / HD_D + EPS)
    kn = (kg * rk) * gk_row_ref[...]
    knsw = (kgsw * rk) * gksw_row_ref[...]
    kb = (kn * gkc_ref[...] + knsw * gks_ref[...]).astype(BF16)
    k_gqa_ref[0] = kb
    key_norms(kb, ind_gqa_ref, kn_gqa_ref)
    sc_gqa = HD_D ** -0.5 * LOG2E
    for hh in range(H_D):
        q = zt[ZT_QD + HD_D * hh:ZT_QD + HD_D * (hh + 1)]
        sw = zt[ZT_QD_SW + HD_D * hh:ZT_QD_SW + HD_D * (hh + 1)]
        r = lax.rsqrt(jnp.mean(q * q, axis=0, keepdims=True) + EPS)
        qn = (q * r) * gq_col_ref[...]
        qsw = (sw * r) * gqsw_col_ref[...]
        qq = (qn * gqc_ref[...] + qsw * gqs_ref[...]) * sc_gqa
        qt_gqa_ref[0, HD_D * hh:HD_D * (hh + 1), :] = qq.astype(BF16)
    vt_gqa_ref[0, 0] = zt[ZT_VD:ZT_VD + 128].astype(BF16)

    xc_ref[0] = z[:, ZS_XC:ZS_XC + W_GROUP]
    g = z[:, ZS_G:ZS_END]
    g_ref[0] = (g * jax.nn.sigmoid(g)).astype(BF16)


def _pre_call(x, shift, scale, lw, tb):
    b, s, d = x.shape
    ts = TOK_TILE
    nt = s // ts
    full2 = lambda a: pl.BlockSpec(a.shape, lambda i, bb: (0, 0))
    tok_rows = lambda w: pl.BlockSpec((ts, w), lambda i, bb: (i, 0))
    tok_cols = lambda r: pl.BlockSpec((r, ts), lambda i, bb: (0, i))
    params = [lw["norm_g"], lw["w_std"], lw["w_t"],
              lw["gqa_col"], lw["w_uqT"], lw["w_uqswT"], lw["gkv_row"], lw["gkv_col"], lw["w_ukp"], lw["w_uvT"],
              lw["gq_col"], lw["gqsw_col"], lw["gk_row"], lw["gksw_row"]]
    in_specs = [pl.BlockSpec((1, ts, d), lambda i, bb: (bb, i, 0)),
                pl.BlockSpec((1, 1, d), lambda i, bb: (bb, 0, 0)),
                pl.BlockSpec((1, 1, d), lambda i, bb: (bb, 0, 0))]
    in_specs += [full2(p) for p in params]
    in_specs += [tok_rows(128), tok_rows(128), tok_cols(32), tok_cols(32),
                 tok_rows(256), tok_rows(256), tok_cols(32), tok_cols(32),
                 tok_rows(128), tok_rows(128), tok_cols(64), tok_cols(64)]
    tables = [tb["mkc"], tb["mks"], tb["mqc"], tb["mqs"],
              tb["dkc"], tb["dks"], tb["dqc"], tb["dqs"],
              tb["gkc"], tb["gks"], tb["gqc"], tb["gqs"]]
    inds = [jnp.asarray(IND_MLA, BF16), jnp.asarray(IND_DIFF, BF16), jnp.asarray(IND_GQA, BF16)]
    in_specs += [full2(a) for a in inds]
    kn_shape = jax.ShapeDtypeStruct((b, nt, SUBLANES, LANES), F32)
    kn_spec = pl.BlockSpec((1, 1, SUBLANES, LANES), lambda i, bb: (bb, i, 0, 0))

    def qt_spec(rows):
        return pl.BlockSpec((1, rows, ts), lambda i, bb: (bb, 0, i))

    def k_spec(w):
        return pl.BlockSpec((1, ts, w), lambda i, bb: (bb, i, 0))

    def vt_spec(rows):
        return pl.BlockSpec((1, 1, rows, ts), lambda i, bb: (bb, i, 0, 0))

    out_shape = [
        jax.ShapeDtypeStruct((b, 512, s), BF16), jax.ShapeDtypeStruct((b, s, 512), BF16),
        jax.ShapeDtypeStruct((b, nt, 256, ts), BF16),
        jax.ShapeDtypeStruct((b, 256, s), BF16), jax.ShapeDtypeStruct((b, s, 256), BF16),
        jax.ShapeDtypeStruct((b, nt, 256, ts), BF16),
        jax.ShapeDtypeStruct((b, 256, s), BF16), jax.ShapeDtypeStruct((b, s, 128), BF16),
        jax.ShapeDtypeStruct((b, nt, 128, ts), BF16),
        jax.ShapeDtypeStruct((b, s, 256), F32), jax.ShapeDtypeStruct((b, s, 1024), BF16),
        kn_shape, kn_shape, kn_shape,
    ]
    out_specs = [qt_spec(512), k_spec(512), vt_spec(256),
                 qt_spec(256), k_spec(256), vt_spec(256),
                 qt_spec(256), k_spec(128), vt_spec(128),
                 k_spec(256), k_spec(1024), kn_spec, kn_spec, kn_spec]
    return pl.pallas_call(
        _pre_kernel, out_shape=out_shape, grid=(nt, b), in_specs=in_specs, out_specs=out_specs,
        compiler_params=_cparams(2), name="pre_proj",
    )(x, shift, scale, *params, *tables, *inds)


def _attn_kernel(*refs, maps, n_chunks, tk, kind, lam_init):
    if kind == "diff":
        qt_ref, k_ref, vt_ref, kn_ref, lam_ref, subln_ref, out_ref, w_scr = refs
    else:
        qt_ref, k_ref, vt_ref, kn_ref, out_ref, w_scr = refs
        lam_ref = subln_ref = None
    tq = qt_ref.shape[2]
    dv = 64
    nm = len(maps)
    finish = functools.partial(_attn_finish, out_ref=out_ref, lam_ref=lam_ref, subln_ref=subln_ref,
                               kind=kind, lam_init=lam_init)

    for i, (kg, q0, qn, woff, v0, _) in enumerate(maps):
        q = qt_ref[0, q0:q0 + qn, :]
        pieces = []
        if woff > 0:
            pieces.append(jnp.zeros((woff, tq), BF16))
        pieces.append(q)
        if LANES - woff - qn > 0:
            pieces.append(jnp.zeros((LANES - woff - qn, tq), BF16))
        w_scr[i] = pieces[0] if len(pieces) == 1 else jnp.concatenate(pieces, axis=0)

    def scores(i, c, r1=0, rows=tk):
        kg = maps[i][0]
        r0 = c * tk + r1 if isinstance(c, int) else pl.multiple_of(c * tk, tk)
        return jnp.dot(k_ref[0, pl.ds(r0, rows), LANES * kg:LANES * (kg + 1)], w_scr[i],
                       preferred_element_type=F32)

    def pv(i, c, p, r1=0, rows=tk):
        v0 = maps[i][4]
        v_ext = jnp.concatenate([vt_ref[0, c, v0:v0 + dv, r1:r1 + rows], jnp.ones((ONES_ROWS, rows), BF16)],
                                axis=0)
        return jnp.dot(v_ext, p.astype(BF16), preferred_element_type=F32)

    kn2 = jnp.max(kn_ref[0], axis=0)[0:1]
    lane = lax.broadcasted_iota(jnp.int32, (1, LANES), 1)
    bounds = []
    for i in range(nm):
        kmax = jnp.sqrt(jnp.max(jnp.where(lane == maps[i][5], kn2, 0.0), axis=-1, keepdims=True))
        w = w_scr[i].astype(F32)
        bounds.append(jnp.sqrt(jnp.sum(w * w, axis=0, keepdims=True)) * kmax)
    units = [(i, c, r1) for c in range(n_chunks) for r1 in range(0, tk, SUB_KEYS) for i in range(nm)]
    accs = [None] * nm
    pending = []
    for n in range(len(units) + SCORE_LEAD):
        if n < len(units):
            i, c, r1 = units[n]
            pending.append(scores(i, c, r1, SUB_KEYS))
        if n >= SCORE_LEAD:
            i, c, r1 = units[n - SCORE_LEAD]
            part = pv(i, c, jnp.exp2(pending.pop(0) - bounds[i]), r1, SUB_KEYS)
            accs[i] = part if accs[i] is None else accs[i] + part
    finish(accs)

    lo, hi = 2.0 ** -DENOM_LOG2_RANGE, 2.0 ** DENOM_LOG2_RANGE
    bad = None
    for acc in accs:
        l = acc[dv:dv + 1]
        b = jnp.where((l > lo) & (l < hi), 0.0, 1.0)
        bad = b if bad is None else jnp.maximum(bad, b)

    @pl.when(jnp.max(bad) > 0.0)
    def _():
        def safe_body(c, carry):
            new = []
            for i in range(nm):
                m, acc = carry[i]
                s = scores(i, c)
                mn = jnp.maximum(m, jnp.max(s, axis=0, keepdims=True))
                acc = jnp.exp2(m - mn) * acc + pv(i, c, jnp.exp2(s - mn))
                new.append((mn, acc))
            return tuple(new)

        init = tuple((jnp.full((1, tq), -jnp.inf, F32), jnp.zeros((dv + ONES_ROWS, tq), F32))
                     for _ in range(nm))
        fs = lax.fori_loop(0, n_chunks, safe_body, init)
        finish([f[1] for f in fs])


def _attn_finish(accs, out_ref, lam_ref, subln_ref, kind, lam_init):
    dv = 64
    os_ = [acc[0:dv] / acc[dv:dv + 1] for acc in accs]

    if kind == "diff":
        lf = lam_ref[...]
        s1 = jnp.sum(lf[0:1] * lf[1:2], axis=-1, keepdims=True)
        s2 = jnp.sum(lf[2:3] * lf[3:4], axis=-1, keepdims=True)
        lam_full = jnp.exp(s1) - jnp.exp(s2) + lam_init
        heads = []
        for hh in range(H_B):
            o = os_[2 * hh] - lam_full * os_[2 * hh + 1]
            r = lax.rsqrt(jnp.mean(o * o, axis=0, keepdims=True) + EPS)
            heads.append(((o * r) * subln_ref[...]) * (1.0 - lam_init))
        ot = jnp.concatenate(heads, axis=0)
    else:
        ot = jnp.concatenate(os_, axis=0)
    out_ref[0] = ot.T


def _attn_call(qt, k, vt, kn, maps, kind, extra=(), lam_init=0.0, name="attn"):
    b, fq, s = qt.shape
    fk = k.shape[2]
    _, nt, fv, tk = vt.shape
    tq = Q_TILE
    in_specs = [pl.BlockSpec((1, fq, tq), lambda bb, qi: (bb, 0, qi)),
                pl.BlockSpec((1, s, fk), lambda bb, qi: (bb, 0, 0)),
                pl.BlockSpec((1, nt, fv, tk), lambda bb, qi: (bb, 0, 0, 0)),
                pl.BlockSpec((1, nt, SUBLANES, LANES), lambda bb, qi: (bb, 0, 0, 0))]
    in_specs += [pl.BlockSpec(e.shape, lambda bb, qi: (0, 0)) for e in extra]
    kern = functools.partial(_attn_kernel, maps=tuple(maps), n_chunks=nt, tk=tk, kind=kind,
                             lam_init=lam_init)
    return pl.pallas_call(
        kern, out_shape=jax.ShapeDtypeStruct((b, s, W_GROUP), F32),
        grid=(b, s // tq), in_specs=in_specs,
        out_specs=pl.BlockSpec((1, tq, W_GROUP), lambda bb, qi: (bb, qi, 0)),
        scratch_shapes=[pltpu.VMEM((len(maps), LANES, tq), BF16)],
        compiler_params=_cparams(2), name=name,
    )(qt, k, vt, kn, *extra)


MAPS_MLA = [(hh, LANES * hh, LANES, 0, V_A * hh, hh) for hh in range(H_A)]
MAPS_DIFF = [(m // 4, DK_B * m, DK_B, DK_B * (m % 4), DV_B * (m // 2), m) for m in range(2 * H_B)]
MAPS_GQA = [(0, HD_D * hh, HD_D, HD_D * (hh // (H_D // KV_H_D)), HD_D * (hh // (H_D // KV_H_D)),
             hh // (H_D // KV_H_D)) for hh in range(H_D)]


def _softplus(x):
    return jnp.maximum(x, 0.0) + jnp.log1p(jnp.exp(-jnp.abs(x)))


def _lru_kernel(xc_ref, cw_ref, cb_ref, wbd_ref, bias_ref, lam_ref, out_ref,
                xpad, af, gf, ar, gr, hr, *, s, tc):
    zero8 = jnp.zeros((SUBLANES, LANES), F32)
    xpad[0:SUBLANES, :] = zero8
    xpad[s + SUBLANES:s + 2 * SUBLANES, :] = zero8
    xpad[SUBLANES:s + SUBLANES, :] = xc_ref[0]
    lam = lam_ref[0]
    sp_f = _softplus(-lam[0:1])
    sp_r = _softplus(-lam[1:2])
    cw = cw_ref[...]
    cb = cb_ref[...]
    wbd = wbd_ref[0]
    bias = bias_ref[0]
    n = tc + 2 * SUBLANES

    def gates(pre_r, pre_i, sp, xconv):
        r = jax.nn.sigmoid(pre_r)
        i = jax.nn.sigmoid(pre_i)
        log_a = (-LRU_C * r) * sp
        a = jnp.exp(log_a)
        gx = jnp.sqrt(-jnp.tanh(log_a) * (a * a + 1.0)) * (i * xconv)
        return a, gx

    def chunk(ci, carry):
        t0 = pl.multiple_of(ci * tc, tc)
        xw = xpad[pl.ds(t0, n), :]
        taps = [pltpu.roll(xw, 2, 0), pltpu.roll(xw, 1, 0), xw, pltpu.roll(xw, n - 1, 0)]
        xconv = cb + taps[0][SUBLANES:SUBLANES + tc] * cw[0:1]
        for j in range(1, CONV_W):
            xconv = xconv + taps[j][SUBLANES:SUBLANES + tc] * cw[j:j + 1]
        pre = jnp.dot(xconv.astype(BF16), wbd, preferred_element_type=F32) + bias
        a, gx = gates(pre[:, 0:128], pre[:, 128:256], sp_f, xconv)
        af[pl.ds(t0, tc), :] = a
        gf[pl.ds(t0, tc), :] = gx
        a, gx = gates(pre[:, 256:384], pre[:, 384:512], sp_r, xconv)
        ar[pl.ds(t0, tc), :] = a
        gr[pl.ds(t0, tc), :] = gx
        return carry

    lax.fori_loop(0, s // tc, chunk, 0)

    row = lax.broadcasted_iota(jnp.int32, (SUBLANES, LANES), 0)
    ng = s // SUBLANES

    def scan8(a, bv, reverse):
        for dd in (1, 2, 4):
            if reverse:
                ok = row < SUBLANES - dd
                sh = SUBLANES - dd
            else:
                ok = row >= dd
                sh = dd
            a_s = jnp.where(ok, pltpu.roll(a, sh, 0), 1.0)
            b_s = jnp.where(ok, pltpu.roll(bv, sh, 0), 0.0)
            bv = a * b_s + bv
            a = a * a_s
        return a, bv

    def scan_body(gi, carry):
        hf_prev, hr_prev = carry
        t0 = pl.multiple_of(gi * SUBLANES, SUBLANES)
        a, bv = scan8(af[pl.ds(t0, SUBLANES), :], gf[pl.ds(t0, SUBLANES), :], False)
        hf = bv + a * hf_prev
        out_ref[0, pl.ds(t0, SUBLANES), :] = hf
        t1 = pl.multiple_of((ng - 1 - gi) * SUBLANES, SUBLANES)
        a, bv = scan8(ar[pl.ds(t1, SUBLANES), :], gr[pl.ds(t1, SUBLANES), :], True)
        hrv = bv + a * hr_prev
        hr[pl.ds(t1, SUBLANES), :] = hrv
        return (jnp.broadcast_to(hf[SUBLANES - 1:SUBLANES], (SUBLANES, LANES)),
                jnp.broadcast_to(hrv[0:1], (SUBLANES, LANES)))

    lax.fori_loop(0, ng, scan_body, (zero8, zero8), unroll=4)
    out_ref[0] = out_ref[0] + hr[...]


def _lru_call(xc, lw):
    b, s, _ = xc.shape
    kern = functools.partial(_lru_kernel, s=s, tc=LRU_CHUNK)
    seq = pltpu.VMEM((s, LANES), F32)
    return pl.pallas_call(
        kern, out_shape=jax.ShapeDtypeStruct((b, s, W_GROUP), F32),
        grid=(b, W_GROUP // LANES),
        in_specs=[pl.BlockSpec((1, s, LANES), lambda bb, hf: (bb, 0, hf)),
                  pl.BlockSpec((CONV_W, LANES), lambda bb, hf: (0, hf)),
                  pl.BlockSpec((1, LANES), lambda bb, hf: (0, hf)),
                  pl.BlockSpec((1, LANES, 4 * LANES), lambda bb, hf: (hf, 0, 0)),
                  pl.BlockSpec((1, 1, 4 * LANES), lambda bb, hf: (hf, 0, 0)),
                  pl.BlockSpec((1, 2, LANES), lambda bb, hf: (hf, 0, 0))],
        out_specs=pl.BlockSpec((1, s, LANES), lambda bb, hf: (bb, 0, hf)),
        scratch_shapes=[pltpu.VMEM((s + 2 * SUBLANES, LANES), F32), seq, seq, seq, seq, seq],
        compiler_params=_cparams(2), name="rglru",
    )(xc, lw["conv_w"], lw["conv_b"], lw["lru_wbd"], lw["lru_bias"], lw["lru_lam"])


def _post_kernel(x_ref, oa_ref, ob_ref, oc_ref, od_ref, g_ref, gate_ref, wout_ref, fn_ref, out_ref, *, final):
    y = None
    for i, o_ref in enumerate((oa_ref, ob_ref, oc_ref, od_ref)):
        og = (o_ref[0] * g_ref[0, :, W_GROUP * i:W_GROUP * (i + 1)].astype(F32)).astype(BF16)
        part = jnp.dot(og, wout_ref[W_GROUP * i:W_GROUP * (i + 1), :], preferred_element_type=F32)
        y = part if y is None else y + part
    xn = x_ref[0] + gate_ref[0] * y
    if final:
        ms = jnp.mean(xn * xn, axis=-1, keepdims=True)
        xn = (xn * lax.rsqrt(ms + EPS)) * fn_ref[...]
    out_ref[0] = xn


def _post_call(x, oa, ob, oc, od, g, gate, w_out_bf, final_norm, final):
    b, s, d = x.shape
    ts = TOK_TILE
    tok = lambda w: pl.BlockSpec((1, ts, w), lambda bb, i: (bb, i, 0))
    return pl.pallas_call(
        functools.partial(_post_kernel, final=final),
        out_shape=jax.ShapeDtypeStruct((b, s, d), F32),
        grid=(b, s // ts),
        in_specs=[tok(d), tok(W_GROUP), tok(W_GROUP), tok(W_GROUP), tok(W_GROUP), tok(4 * W_GROUP),
                  pl.BlockSpec((1, 1, d), lambda bb, i: (bb, 0, 0)),
                  pl.BlockSpec((d, d), lambda bb, i: (0, 0)),
                  pl.BlockSpec((1, d), lambda bb, i: (0, 0))],
        out_specs=tok(d),
        compiler_params=_cparams(2), name="post_proj",
    )(x, oa, ob, oc, od, g, gate, w_out_bf, final_norm)


def _angles(pos, half, theta):
    inv = jnp.power(jnp.float32(theta), -jnp.arange(half, dtype=F32) / half)
    return pos[:, None] * inv[None, :]


def _tables(s):
    pos = jnp.arange(s, dtype=F32)
    rows = s // GRID_W
    row = jnp.repeat(jnp.arange(rows, dtype=F32), GRID_W)
    col = jnp.tile(jnp.arange(GRID_W, dtype=F32), rows)
    a = _angles(pos, ROPE_A // 2, MLA_ROPE_THETA)
    c, sn = jnp.cos(a), jnp.sin(a)
    z64, z32 = jnp.zeros((s, 64), F32), jnp.zeros((s, 32), F32)
    tb = {"mkc": jnp.concatenate([z64, c, c, z32], axis=1),
          "mks": jnp.concatenate([z64, -sn, sn, z32], axis=1),
          "mqc": jnp.concatenate([c, c], axis=1).T,
          "mqs": jnp.concatenate([-sn, sn], axis=1).T}
    a = _angles(pos, DK_B // 8, ROPE_THETA)
    c, sn = jnp.cos(a), jnp.sin(a)
    c32 = jnp.concatenate([c, c, jnp.ones((s, DK_B - DK_B // 4), F32)], axis=1)
    s32 = jnp.concatenate([-sn, sn, jnp.zeros((s, DK_B - DK_B // 4), F32)], axis=1)
    tb.update({"dkc": jnp.tile(c32, (1, 2 * H_B)), "dks": jnp.tile(s32, (1, 2 * H_B)),
               "dqc": c32.T, "dqs": s32.T})
    ar_ = _angles(row, HD_D // 4, AXIAL_THETA)
    ac_ = _angles(col, HD_D // 4, AXIAL_THETA)
    cr, sr, cc, sc = jnp.cos(ar_), jnp.sin(ar_), jnp.cos(ac_), jnp.sin(ac_)
    c64 = jnp.concatenate([cr, cr, cc, cc], axis=1)
    s64 = jnp.concatenate([-sr, sr, -sc, sc], axis=1)
    tb.update({"gkc": jnp.tile(c64, (1, KV_H_D)), "gks": jnp.tile(s64, (1, KV_H_D)),
               "gqc": c64.T, "gqs": s64.T})
    return tb


def _blockdiag_pair(w, half):
    z = jnp.zeros((BW_C, BW_C), F32)
    top = jnp.concatenate([w[2 * half], z], axis=1)
    bot = jnp.concatenate([z, w[2 * half + 1]], axis=1)
    return jnp.concatenate([top, bot], axis=0)


def _layer_weights(l, p):
    w = p["w_in"][l]
    col = lambda o, n: w[:, o:o + n]
    z64 = jnp.zeros((D_MODEL, 64), F32)
    z32 = jnp.zeros((D_MODEL, 32), F32)
    kra, kb, kd = col(O_KRA, ROPE_A), col(O_KB, 256), col(O_KD, 128)
    qb, qd = col(O_QB, 256), col(O_QD, 256)
    w_std = jnp.concatenate([
        col(O_KVA, KV_LORA),
        z64, kra, z32,
        z64, kra[:, P_MLA], z32,
        kb, kb[:, P_DIFF],
        kd, kd[:, P_GQA_K],
        col(O_XC, W_GROUP),
        col(O_GA, W_GROUP), col(O_GB, W_GROUP), col(O_GC, W_GROUP), col(O_GD, W_GROUP)], axis=1)
    w_t = jnp.concatenate([
        col(O_QA, Q_LORA), col(O_KVA, KV_LORA),
        qb, qb[:, P_DIFF], col(O_VB, 256),
        qd, qd[:, P_GQA_Q], col(O_VD, 128)], axis=1).T
    wuq = p["mla_w_uq"][l]
    dq = NOPE_A + ROPE_A
    pe_sw = jnp.concatenate([wuq[:, dq * h + NOPE_A:dq * (h + 1)][:, P_MLA] for h in range(H_A)], axis=1)
    wukv = p["mla_w_ukv"][l]
    dkv = NOPE_A + V_A
    z_k = jnp.zeros((KV_LORA, LANES - NOPE_A), F32)
    w_ukp = jnp.concatenate(
        [jnp.concatenate([wukv[:, dkv * h:dkv * h + NOPE_A], z_k], axis=1) for h in range(H_A)], axis=1)
    w_uv = jnp.concatenate([wukv[:, dkv * h + NOPE_A:dkv * (h + 1)] for h in range(H_A)], axis=1)
    gq, gk = p["gqa_q_norm"][l], p["gqa_k_norm"][l]
    wa, wx = p["lru_wa"][l], p["lru_wx"][l]
    ba, bx = p["lru_ba"][l], p["lru_bx"][l]
    wbd = jnp.stack([jnp.concatenate([_blockdiag_pair(wa[0], hf), _blockdiag_pair(wx[0], hf),
                                      _blockdiag_pair(wa[1], hf), _blockdiag_pair(wx[1], hf)], axis=1)
                     for hf in range(2)])
    sl = lambda v, hf: v[LANES * hf:LANES * (hf + 1)]
    bias = jnp.stack([jnp.concatenate([sl(ba[0], hf), sl(bx[0], hf), sl(ba[1], hf), sl(bx[1], hf)])[None]
                      for hf in range(2)])
    lam = p["lru_lambda"][l]
    lam_h = jnp.stack([jnp.stack([sl(lam[0], hf), sl(lam[1], hf)]) for hf in range(2)])
    return {
        "norm_g": p["norm_g"][l][None], "w_std": w_std.astype(BF16), "w_t": w_t.astype(BF16),
        "gqa_col": p["mla_q_norm"][l][:, None], "w_uqT": wuq.T.astype(BF16), "w_uqswT": pe_sw.T.astype(BF16),
        "gkv_row": p["mla_kv_norm"][l][None], "gkv_col": p["mla_kv_norm"][l][:, None],
        "w_ukp": w_ukp.astype(BF16), "w_uvT": w_uv.T.astype(BF16),
        "gq_col": gq[:, None], "gqsw_col": gq[P_GQA_HEAD][:, None],
        "gk_row": jnp.tile(gk, KV_H_D)[None], "gksw_row": jnp.tile(gk[P_GQA_HEAD], KV_H_D)[None],
        "diff_lambda": p["diff_lambda"][l], "subln_col": p["diff_subln"][l][:, None],
        "conv_w": p["lru_conv_w"][l], "conv_b": p["lru_conv_b"][l][None],
        "lru_wbd": wbd.astype(BF16), "lru_bias": bias, "lru_lam": lam_h,
        "w_out": p["w_out"][l].astype(BF16),
    }


def _trunk(x, mods, lws, final_norm):
    b, s, d = x.shape
    tb = _tables(s)
    for l in range(DEPTH):
        lw = lws[l]
        shift, scale, gate = mods[l]
        (qt_mla, k_mla, vt_mla, qt_diff, k_diff, vt_diff, qt_gqa, k_gqa, vt_gqa, xc, g,
         kn_mla, kn_diff, kn_gqa) = _pre_call(x, shift, scale, lw, tb)
        lam_init = 0.8 - 0.6 * math.exp(-0.3 * l)
        oa = _attn_call(qt_mla, k_mla, vt_mla, kn_mla, MAPS_MLA, "mla", name="attn_mla")
        ob = _attn_call(qt_diff, k_diff, vt_diff, kn_diff, MAPS_DIFF, "diff",
                        extra=(lw["diff_lambda"], lw["subln_col"]), lam_init=lam_init, name="attn_diff")
        od = _attn_call(qt_gqa, k_gqa, vt_gqa, kn_gqa, MAPS_GQA, "gqa", name="attn_gqa")
        oc = _lru_call(xc, lw)
        x = _post_call(x, oa, ob, oc, od, g, gate, lw["w_out"], final_norm, final=(l == DEPTH - 1))
    return x


def kernel(x_prompt, x_sample, c_prompt, c_sample, ada_w, ada_b, norm_g, w_in, mla_q_norm, mla_w_uq, mla_kv_norm, mla_w_ukv, diff_lambda, diff_subln, lru_conv_w, lru_conv_b, lru_wa, lru_ba, lru_wx, lru_bx, lru_lambda, gqa_q_norm, gqa_k_norm, w_out, final_norm):
    p = dict(norm_g=norm_g, w_in=w_in, mla_q_norm=mla_q_norm, mla_w_uq=mla_w_uq, mla_kv_norm=mla_kv_norm,
             mla_w_ukv=mla_w_ukv, diff_lambda=diff_lambda, diff_subln=diff_subln,
             lru_conv_w=lru_conv_w, lru_conv_b=lru_conv_b, lru_wa=lru_wa, lru_ba=lru_ba,
             lru_wx=lru_wx, lru_bx=lru_bx, lru_lambda=lru_lambda,
             gqa_q_norm=gqa_q_norm, gqa_k_norm=gqa_k_norm, w_out=w_out)
    lws = [_layer_weights(l, p) for l in range(DEPTH)]
    nb = x_prompt.shape[0]
    mod = _modulation(jnp.concatenate([c_prompt, c_sample], axis=0), ada_w.astype(BF16), ada_b)

    def mods_for(lo, hi):
        out = []
        for l in range(DEPTH):
            m = mod[l, lo:hi]
            out.append(tuple(m[:, D_MODEL * j:D_MODEL * (j + 1)][:, None, :] for j in range(3)))
        return out

    fn = final_norm[None]
    y_prompt = _trunk(x_prompt, mods_for(0, nb), lws, fn)
    y_sample = _trunk(x_sample, mods_for(nb, nb + x_sample.shape[0]), lws, fn)
    return (y_prompt, y_sample)
```

```python
import functools
import math

import jax
import jax.numpy as jnp
import numpy as np
from jax import lax
from jax.experimental import pallas as pl
from jax.experimental.pallas import tpu as pltpu

F32 = jnp.float32
BF16 = jnp.bfloat16

D_MODEL = 1024
DEPTH = 2
W_GROUP = 256
H_A, NOPE_A, ROPE_A, V_A = 4, 64, 32, 64
Q_LORA, KV_LORA = 192, 128
H_B, DK_B, DV_B = 4, 32, 64
H_C, BW_C, CONV_W = 4, 64, 4
LRU_C = 8.0
H_D, KV_H_D, HD_D = 4, 2, 64
GRID_W = 64
ROPE_THETA = 500000.0
MLA_ROPE_THETA = 10000.0
AXIAL_THETA = 10000.0
EPS = 1e-6
LOG2E = 1.4426950408889634

IN_SIZES = (Q_LORA, KV_LORA, ROPE_A, W_GROUP,
            H_B * 2 * DK_B, H_B * 2 * DK_B, H_B * DV_B, W_GROUP,
            W_GROUP, W_GROUP,
            H_D * HD_D, KV_H_D * HD_D, KV_H_D * HD_D, W_GROUP)
_OFF = np.concatenate([[0], np.cumsum(IN_SIZES)]).tolist()
(O_QA, O_KVA, O_KRA, O_GA, O_QB, O_KB, O_VB, O_GB, O_XC, O_GC, O_QD, O_KD, O_VD, O_GD, _) = _OFF

LANES = 128
SUBLANES = 8

TOK_TILE = 512
Q_TILE = 512
LRU_CHUNK = 512
DENOM_LOG2_RANGE = 80
SUB_KEYS = 256
SCORE_LEAD = 2
VMEM_LIMIT = 56 * 1024 * 1024

ZS_KVL, ZS_KPE, ZS_KPE_SW, ZS_KB, ZS_KD, ZS_XC, ZS_G, ZS_END = (0, 128, 256, 384, 640, 768, 1024, 2048)
ZT_QA, ZT_KVL, ZT_QB, ZT_VB, ZT_QD, ZT_VD, ZT_END = (0, 192, 320, 576, 832, 1088, 1216)


def _partner(width, block, rot, half):
    idx = np.arange(width)
    d = idx % block
    out = idx.copy()
    in_rot = d < rot
    first = in_rot & ((d % (2 * half)) < half)
    second = in_rot & ~((d % (2 * half)) < half)
    out[first] = idx[first] + half
    out[second] = idx[second] - half
    return out


P_MLA = _partner(ROPE_A, ROPE_A, ROPE_A, ROPE_A // 2)


def _indicator(width, seg):
    m = np.zeros((width, LANES), np.float32)
    m[np.arange(width), np.arange(width) // seg] = 1.0
    return m


IND_MLA = _indicator(H_A * LANES, LANES)
IND_DIFF = _indicator(2 * H_B * DK_B, DK_B)
IND_GQA = _indicator(KV_H_D * HD_D, HD_D)


def _cparams(n_axes):
    return pltpu.CompilerParams(dimension_semantics=("arbitrary",) * n_axes,
                                vmem_limit_bytes=VMEM_LIMIT)


def _mod_kernel(c_ref, w_ref, b_ref, out_ref):
    c = c_ref[...]
    sc = (c * jax.nn.sigmoid(c)).astype(BF16)
    out_ref[0] = jnp.dot(sc, w_ref[0], preferred_element_type=F32) + b_ref[0]


def _modulation(c_all, ada_w_bf, ada_b):
    n = c_all.shape[0]
    return pl.pallas_call(
        _mod_kernel,
        out_shape=jax.ShapeDtypeStruct((DEPTH, n, 3 * D_MODEL), F32),
        grid=(DEPTH, 3),
        in_specs=[
            pl.BlockSpec((n, D_MODEL), lambda l, j: (0, 0)),
            pl.BlockSpec((1, D_MODEL, D_MODEL), lambda l, j: (l, 0, j)),
            pl.BlockSpec((1, 1, D_MODEL), lambda l, j: (l, 0, j)),
        ],
        out_specs=pl.BlockSpec((1, n, D_MODEL), lambda l, j: (l, 0, j)),
        compiler_params=_cparams(2),
        name="adaln_mod",
    )(c_all, ada_w_bf, ada_b.reshape(DEPTH, 1, 3 * D_MODEL))


def _pre_kernel(x_ref, shift_ref, scale_ref, ng_ref, wstd_ref, wt_ref,
                gqa_col_ref, wuqT_ref, gkv_row_ref, gkv_col_ref, wukp_ref, wuvT_ref,
                gq_col_ref, gk_row_ref,
                mkc_ref, mks_ref, mqc_ref, mqs_ref,
                dkc_ref, dks_ref, dqc_ref, dqs_ref,
                gkc_ref, gks_ref, gqc_ref, gqs_ref,
                ind_mla_ref, ind_diff_ref, ind_gqa_ref,
                qt_mla_ref, k_mla_ref, vt_mla_ref,
                qt_diff_ref, k_diff_ref, vt_diff_ref,
                qt_gqa_ref, k_gqa_ref, vt_gqa_ref,
                xc_ref, g_ref, kn_mla_ref, kn_diff_ref, kn_gqa_ref):
    ts = x_ref.shape[1]

    def key_norms(kb, ind_ref, out_ref):
        kf = kb.astype(F32)
        r = jnp.dot((kf * kf).astype(BF16), ind_ref[...], preferred_element_type=F32)
        out_ref[0, 0] = jnp.broadcast_to(jnp.max(r, axis=0, keepdims=True), (SUBLANES, LANES))

    x = x_ref[0]
    ms = jnp.mean(x * x, axis=-1, keepdims=True)
    h = (x * lax.rsqrt(ms + EPS)) * ng_ref[...]
    h = h * (1.0 + scale_ref[0]) + shift_ref[0]
    hb = h.astype(BF16)
    z = jnp.dot(hb, wstd_ref[...], preferred_element_type=F32)
    zt = lax.dot_general(wt_ref[...], hb, (((1,), (1,)), ((), ())),
                         preferred_element_type=F32)

    qlt = zt[ZT_QA:ZT_QA + Q_LORA]
    msq = jnp.mean(qlt * qlt, axis=0, keepdims=True)
    qnb = ((qlt * lax.rsqrt(msq + EPS)) * gqa_col_ref[...]).astype(BF16)
    qat = jnp.dot(wuqT_ref[...], qnb, preferred_element_type=F32)
    sc_mla = (NOPE_A + ROPE_A) ** -0.5 * LOG2E
    dq = NOPE_A + ROPE_A
    pad = jnp.zeros((LANES - dq, ts), F32)
    for hh in range(H_A):
        nope = qat[dq * hh:dq * hh + NOPE_A]
        pe = qat[dq * hh + NOPE_A:dq * (hh + 1)]
        sw = jnp.concatenate([pe[ROPE_A // 2:], pe[:ROPE_A // 2]], axis=0)
        pe = pe * mqc_ref[...] + sw * mqs_ref[...]
        blk = jnp.concatenate([nope, pe, pad], axis=0) * sc_mla
        qt_mla_ref[0, LANES * hh:LANES * (hh + 1), :] = blk.astype(BF16)

    kvl = z[:, ZS_KVL:ZS_KVL + KV_LORA]
    msk = jnp.mean(kvl * kvl, axis=-1, keepdims=True)
    kvn = ((kvl * lax.rsqrt(msk + EPS)) * gkv_row_ref[...]).astype(BF16)
    knope = jnp.dot(kvn, wukp_ref[...], preferred_element_type=F32)
    kpe = z[:, ZS_KPE:ZS_KPE + LANES] * mkc_ref[...] + z[:, ZS_KPE_SW:ZS_KPE_SW + LANES] * mks_ref[...]
    kb = jnp.concatenate([(knope[:, LANES * hh:LANES * (hh + 1)] + kpe).astype(BF16) for hh in range(H_A)],
                         axis=1)
    k_mla_ref[0] = kb
    key_norms(kb, ind_mla_ref, kn_mla_ref)
    kvlt = zt[ZT_KVL:ZT_KVL + KV_LORA]
    mst = jnp.mean(kvlt * kvlt, axis=0, keepdims=True)
    kvnt = ((kvlt * lax.rsqrt(mst + EPS)) * gkv_col_ref[...]).astype(BF16)
    vt_mla_ref[0, 0] = jnp.dot(wuvT_ref[...], kvnt, preferred_element_type=F32).astype(BF16)

    kd = z[:, ZS_KB:ZS_KB + 256]
    lane_d = lax.broadcasted_iota(jnp.int32, (ts, 256), 1)
    half_d = DK_B // 8
    kd_sw = jnp.where(lane_d % (2 * half_d) < half_d, pltpu.roll(kd, 256 - half_d, 1), pltpu.roll(kd, half_d, 1))
    kb = (kd * dkc_ref[...] + kd_sw * dks_ref[...]).astype(BF16)
    k_diff_ref[0] = kb
    key_norms(kb, ind_diff_ref, kn_diff_ref)
    sc_diff = DK_B ** -0.5 * LOG2E
    rot = DK_B // 4
    for m in range(2 * H_B):
        q = zt[ZT_QB + DK_B * m:ZT_QB + DK_B * (m + 1)]
        q8 = q[0:rot]
        q8 = q8 * dqc_ref[0:rot, :] + pltpu.roll(q8, rot // 2, 0) * dqs_ref[0:rot, :]
        qq = jnp.concatenate([q8, q[rot:]], axis=0) * sc_diff
        qt_diff_ref[0, DK_B * m:DK_B * (m + 1), :] = qq.astype(BF16)
    vt_diff_ref[0, 0] = zt[ZT_VB:ZT_VB + 256].astype(BF16)

    kg = z[:, ZS_KD:ZS_KD + LANES]
    k2 = kg * kg
    lo = lax.broadcasted_iota(jnp.int32, (ts, LANES), 1) < HD_D
    s_lo = jnp.sum(jnp.where(lo, k2, 0.0), axis=-1, keepdims=True)
    s_hi = jnp.sum(jnp.where(lo, 0.0, k2), axis=-1, keepdims=True)
    rk = lax.rsqrt(jnp.where(lo, s_lo, s_hi) / HD_D + EPS)
    kn = (kg * rk) * gk_row_ref[...]
    half_g = HD_D // 4
    lane_g = lax.broadcasted_iota(jnp.int32, (ts, LANES), 1)
    knsw = jnp.where(lane_g % (2 * half_g) < half_g, pltpu.roll(kn, LANES - half_g, 1), pltpu.roll(kn, half_g, 1))
    kb = (kn * gkc_ref[...] + knsw * gks_ref[...]).astype(BF16)
    k_gqa_ref[0] = kb
    key_norms(kb, ind_gqa_ref, kn_gqa_ref)
    sc_gqa = HD_D ** -0.5 * LOG2E
    for hh in range(H_D):
        q = zt[ZT_QD + HD_D * hh:ZT_QD + HD_D * (hh + 1)]
        r = lax.rsqrt(jnp.mean(q * q, axis=0, keepdims=True) + EPS)
        qn = (q * r) * gq_col_ref[...]
        qsw = jnp.concatenate([qn[half_g:2 * half_g], qn[0:half_g], qn[3 * half_g:], qn[2 * half_g:3 * half_g]],
                              axis=0)
        qq = (qn * gqc_ref[...] + qsw * gqs_ref[...]) * sc_gqa
        qt_gqa_ref[0, HD_D * hh:HD_D * (hh + 1), :] = qq.astype(BF16)
    vt_gqa_ref[0, 0] = zt[ZT_VD:ZT_VD + 128].astype(BF16)

    xc_ref[0] = z[:, ZS_XC:ZS_XC + W_GROUP]
    g = z[:, ZS_G:ZS_END]
    g_ref[0] = (g * jax.nn.sigmoid(g)).astype(BF16)


def _pre_call(x, shift, scale, lw, tb):
    b, s, d = x.shape
    ts = TOK_TILE
    nt = s // ts
    full2 = lambda a: pl.BlockSpec(a.shape, lambda i, bb: (0, 0))
    tok_rows = lambda w: pl.BlockSpec((ts, w), lambda i, bb: (i, 0))
    tok_cols = lambda r: pl.BlockSpec((r, ts), lambda i, bb: (0, i))
    params = [lw["norm_g"], lw["w_std"], lw["w_t"],
              lw["gqa_col"], lw["w_uqT"], lw["gkv_row"], lw["gkv_col"], lw["w_ukp"], lw["w_uvT"],
              lw["gq_col"], lw["gk_row"]]
    in_specs = [pl.BlockSpec((1, ts, d), lambda i, bb: (bb, i, 0)),
                pl.BlockSpec((1, 1, d), lambda i, bb: (bb, 0, 0)),
                pl.BlockSpec((1, 1, d), lambda i, bb: (bb, 0, 0))]
    in_specs += [full2(p) for p in params]
    in_specs += [tok_rows(128), tok_rows(128), tok_cols(32), tok_cols(32),
                 tok_rows(256), tok_rows(256), tok_cols(32), tok_cols(32),
                 tok_rows(128), tok_rows(128), tok_cols(64), tok_cols(64)]
    tables = [tb["mkc"], tb["mks"], tb["mqc"], tb["mqs"],
              tb["dkc"], tb["dks"], tb["dqc"], tb["dqs"],
              tb["gkc"], tb["gks"], tb["gqc"], tb["gqs"]]
    inds = [jnp.asarray(IND_MLA, BF16), jnp.asarray(IND_DIFF, BF16), jnp.asarray(IND_GQA, BF16)]
    in_specs += [full2(a) for a in inds]
    kn_shape = jax.ShapeDtypeStruct((b, nt, SUBLANES, LANES), F32)
    kn_spec = pl.BlockSpec((1, 1, SUBLANES, LANES), lambda i, bb: (bb, i, 0, 0))

    def qt_spec(rows):
        return pl.BlockSpec((1, rows, ts), lambda i, bb: (bb, 0, i))

    def k_spec(w):
        return pl.BlockSpec((1, ts, w), lambda i, bb: (bb, i, 0))

    def vt_spec(rows):
        return pl.BlockSpec((1, 1, rows, ts), lambda i, bb: (bb, i, 0, 0))

    out_shape = [
        jax.ShapeDtypeStruct((b, 512, s), BF16), jax.ShapeDtypeStruct((b, s, 512), BF16),
        jax.ShapeDtypeStruct((b, nt, 256, ts), BF16),
        jax.ShapeDtypeStruct((b, 256, s), BF16), jax.ShapeDtypeStruct((b, s, 256), BF16),
        jax.ShapeDtypeStruct((b, nt, 256, ts), BF16),
        jax.ShapeDtypeStruct((b, 256, s), BF16), jax.ShapeDtypeStruct((b, s, 128), BF16),
        jax.ShapeDtypeStruct((b, nt, 128, ts), BF16),
        jax.ShapeDtypeStruct((b, s, 256), F32), jax.ShapeDtypeStruct((b, s, 1024), BF16),
        kn_shape, kn_shape, kn_shape,
    ]
    out_specs = [qt_spec(512), k_spec(512), vt_spec(256),
                 qt_spec(256), k_spec(256), vt_spec(256),
                 qt_spec(256), k_spec(128), vt_spec(128),
                 k_spec(256), k_spec(1024), kn_spec, kn_spec, kn_spec]
    return pl.pallas_call(
        _pre_kernel, out_shape=out_shape, grid=(nt, b), in_specs=in_specs, out_specs=out_specs,
        compiler_params=_cparams(2), name="pre_proj",
    )(x, shift, scale, *params, *tables, *inds)


def _attn_kernel(*refs, maps, n_chunks, tk, kind, lam_init):
    if kind == "diff":
        qt_ref, k_ref, vt_ref, kn_ref, lam_ref, subln_ref, out_ref, w_scr = refs
    else:
        qt_ref, k_ref, vt_ref, kn_ref, out_ref, w_scr = refs
        lam_ref = subln_ref = None
    tq = qt_ref.shape[2]
    dv = 64
    nm = len(maps)
    finish = functools.partial(_attn_finish, out_ref=out_ref, lam_ref=lam_ref, subln_ref=subln_ref,
                               kind=kind, lam_init=lam_init)

    for i, (kg, q0, qn, woff, v0, _) in enumerate(maps):
        q = qt_ref[0, q0:q0 + qn, :]
        pieces = []
        if woff > 0:
            pieces.append(jnp.zeros((woff, tq), BF16))
        pieces.append(q)
        if LANES - woff - qn > 0:
            pieces.append(jnp.zeros((LANES - woff - qn, tq), BF16))
        w_scr[i] = pieces[0] if len(pieces) == 1 else jnp.concatenate(pieces, axis=0)

    def scores(i, c, r1=0, rows=tk):
        kg = maps[i][0]
        r0 = c * tk + r1 if isinstance(c, int) else pl.multiple_of(c * tk, tk)
        return jnp.dot(k_ref[0, pl.ds(r0, rows), LANES * kg:LANES * (kg + 1)], w_scr[i],
                       preferred_element_type=F32)

    def pv(i, c, p, r1=0, rows=tk):
        v0 = maps[i][4]
        return jnp.dot(vt_ref[0, c, v0:v0 + dv, r1:r1 + rows], p.astype(BF16),
                       preferred_element_type=F32)

    kn2 = jnp.max(kn_ref[0], axis=0)[0:1]
    lane = lax.broadcasted_iota(jnp.int32, (1, LANES), 1)
    bounds = []
    for i in range(nm):
        kmax = jnp.sqrt(jnp.max(jnp.where(lane == maps[i][5], kn2, 0.0), axis=-1, keepdims=True))
        w = w_scr[i].astype(F32)
        bounds.append(jnp.sqrt(jnp.sum(w * w, axis=0, keepdims=True)) * kmax)
    units = [(i, c, r1) for c in range(n_chunks) for r1 in range(0, tk, SUB_KEYS) for i in range(nm)]
    accs, ls = [None] * nm, [None] * nm
    pending = []
    for n in range(len(units) + SCORE_LEAD):
        if n < len(units):
            i, c, r1 = units[n]
            pending.append(scores(i, c, r1, SUB_KEYS))
        if n >= SCORE_LEAD:
            i, c, r1 = units[n - SCORE_LEAD]
            p = jnp.exp2(pending.pop(0) - bounds[i])
            part, psum = pv(i, c, p, r1, SUB_KEYS), jnp.sum(p, axis=0, keepdims=True)
            accs[i] = part if accs[i] is None else accs[i] + part
            ls[i] = psum if ls[i] is None else ls[i] + psum
    finish(accs, ls)

    lo, hi = 2.0 ** -DENOM_LOG2_RANGE, 2.0 ** DENOM_LOG2_RANGE
    bad = None
    for l in ls:
        b = jnp.where((l > lo) & (l < hi), 0.0, 1.0)
        bad = b if bad is None else jnp.maximum(bad, b)

    @pl.when(jnp.max(bad) > 0.0)
    def _():
        def safe_body(c, carry):
            new = []
            for i in range(nm):
                m, l, acc = carry[i]
                s = scores(i, c)
                mn = jnp.maximum(m, jnp.max(s, axis=0, keepdims=True))
                p, alpha = jnp.exp2(s - mn), jnp.exp2(m - mn)
                new.append((mn, alpha * l + jnp.sum(p, axis=0, keepdims=True), alpha * acc + pv(i, c, p)))
            return tuple(new)

        init = tuple((jnp.full((1, tq), -jnp.inf, F32), jnp.zeros((1, tq), F32), jnp.zeros((dv, tq), F32))
                     for _ in range(nm))
        fs = lax.fori_loop(0, n_chunks, safe_body, init)
        finish([f[2] for f in fs], [f[1] for f in fs])


def _attn_finish(accs, ls, out_ref, lam_ref, subln_ref, kind, lam_init):
    os_ = [acc / l for acc, l in zip(accs, ls)]

    if kind == "diff":
        lf = lam_ref[...]
        s1 = jnp.sum(lf[0:1] * lf[1:2], axis=-1, keepdims=True)
        s2 = jnp.sum(lf[2:3] * lf[3:4], axis=-1, keepdims=True)
        lam_full = jnp.exp(s1) - jnp.exp(s2) + lam_init
        heads = []
        for hh in range(H_B):
            o = os_[2 * hh] - lam_full * os_[2 * hh + 1]
            r = lax.rsqrt(jnp.mean(o * o, axis=0, keepdims=True) + EPS)
            heads.append(((o * r) * subln_ref[...]) * (1.0 - lam_init))
        ot = jnp.concatenate(heads, axis=0)
    else:
        ot = jnp.concatenate(os_, axis=0)
    out_ref[0] = ot.T.astype(out_ref.dtype)


def _attn_call(qt, k, vt, kn, maps, kind, extra=(), lam_init=0.0, name="attn"):
    b, fq, s = qt.shape
    fk = k.shape[2]
    _, nt, fv, tk = vt.shape
    tq = Q_TILE
    in_specs = [pl.BlockSpec((1, fq, tq), lambda bb, qi: (bb, 0, qi)),
                pl.BlockSpec((1, s, fk), lambda bb, qi: (bb, 0, 0)),
                pl.BlockSpec((1, nt, fv, tk), lambda bb, qi: (bb, 0, 0, 0)),
                pl.BlockSpec((1, nt, SUBLANES, LANES), lambda bb, qi: (bb, 0, 0, 0))]
    in_specs += [pl.BlockSpec(e.shape, lambda bb, qi: (0, 0)) for e in extra]
    kern = functools.partial(_attn_kernel, maps=tuple(maps), n_chunks=nt, tk=tk, kind=kind,
                             lam_init=lam_init)
    return pl.pallas_call(
        kern, out_shape=jax.ShapeDtypeStruct((b, s, W_GROUP), BF16),
        grid=(b, s // tq), in_specs=in_specs,
        out_specs=pl.BlockSpec((1, tq, W_GROUP), lambda bb, qi: (bb, qi, 0)),
        scratch_shapes=[pltpu.VMEM((len(maps), LANES, tq), BF16)],
        compiler_params=_cparams(2), name=name,
    )(qt, k, vt, kn, *extra)


MAPS_MLA = [(hh, LANES * hh, LANES, 0, V_A * hh, hh) for hh in range(H_A)]
MAPS_DIFF = [(m // 4, DK_B * m, DK_B, DK_B * (m % 4), DV_B * (m // 2), m) for m in range(2 * H_B)]
MAPS_GQA = [(0, HD_D * hh, HD_D, HD_D * (hh // (H_D // KV_H_D)), HD_D * (hh // (H_D // KV_H_D)),
             hh // (H_D // KV_H_D)) for hh in range(H_D)]


def _softplus(x):
    return jnp.maximum(x, 0.0) + jnp.log1p(jnp.exp(-jnp.abs(x)))


def _lru_kernel(xc_ref, cw_ref, cb_ref, wbd_ref, bias_ref, lam_ref, out_ref,
                xpad, af, gf, ar, gr, hf_scr, hr, *, s, tc):
    zero8 = jnp.zeros((SUBLANES, LANES), F32)
    xpad[0:SUBLANES, :] = zero8
    xpad[s + SUBLANES:s + 2 * SUBLANES, :] = zero8
    xpad[SUBLANES:s + SUBLANES, :] = xc_ref[0]
    lam = lam_ref[0]
    sp_f = _softplus(-lam[0:1])
    sp_r = _softplus(-lam[1:2])
    cw = cw_ref[...]
    cb = cb_ref[...]
    wbd = wbd_ref[0]
    bias = bias_ref[0]
    n = tc + 2 * SUBLANES

    def gates(pre_r, pre_i, sp, xconv):
        r = jax.nn.sigmoid(pre_r)
        i = jax.nn.sigmoid(pre_i)
        log_a = (-LRU_C * r) * sp
        a = jnp.exp(log_a)
        gx = jnp.sqrt(-jnp.tanh(log_a) * (a * a + 1.0)) * (i * xconv)
        return a, gx

    def chunk(ci, carry):
        t0 = pl.multiple_of(ci * tc, tc)
        xw = xpad[pl.ds(t0, n), :]
        taps = [pltpu.roll(xw, 2, 0), pltpu.roll(xw, 1, 0), xw, pltpu.roll(xw, n - 1, 0)]
        xconv = cb + taps[0][SUBLANES:SUBLANES + tc] * cw[0:1]
        for j in range(1, CONV_W):
            xconv = xconv + taps[j][SUBLANES:SUBLANES + tc] * cw[j:j + 1]
        pre = jnp.dot(xconv.astype(BF16), wbd, preferred_element_type=F32) + bias
        a, gx = gates(pre[:, 0:128], pre[:, 128:256], sp_f, xconv)
        af[pl.ds(t0, tc), :] = a
        gf[pl.ds(t0, tc), :] = gx
        a, gx = gates(pre[:, 256:384], pre[:, 384:512], sp_r, xconv)
        ar[pl.ds(t0, tc), :] = a
        gr[pl.ds(t0, tc), :] = gx
        return carry

    lax.fori_loop(0, s // tc, chunk, 0)

    row = lax.broadcasted_iota(jnp.int32, (SUBLANES, LANES), 0)
    ng = s // SUBLANES

    def scan8(a, bv, reverse):
        for dd in (1, 2, 4):
            if reverse:
                ok = row < SUBLANES - dd
                sh = SUBLANES - dd
            else:
                ok = row >= dd
                sh = dd
            a_s = jnp.where(ok, pltpu.roll(a, sh, 0), 1.0)
            b_s = jnp.where(ok, pltpu.roll(bv, sh, 0), 0.0)
            bv = a * b_s + bv
            a = a * a_s
        return a, bv

    def scan_body(gi, carry):
        hf_prev, hr_prev = carry
        t0 = pl.multiple_of(gi * SUBLANES, SUBLANES)
        a, bv = scan8(af[pl.ds(t0, SUBLANES), :], gf[pl.ds(t0, SUBLANES), :], False)
        hf = bv + a * hf_prev
        hf_scr[pl.ds(t0, SUBLANES), :] = hf
        t1 = pl.multiple_of((ng - 1 - gi) * SUBLANES, SUBLANES)
        a, bv = scan8(ar[pl.ds(t1, SUBLANES), :], gr[pl.ds(t1, SUBLANES), :], True)
        hrv = bv + a * hr_prev
        hr[pl.ds(t1, SUBLANES), :] = hrv
        return (jnp.broadcast_to(hf[SUBLANES - 1:SUBLANES], (SUBLANES, LANES)),
                jnp.broadcast_to(hrv[0:1], (SUBLANES, LANES)))

    lax.fori_loop(0, ng, scan_body, (zero8, zero8), unroll=4)
    out_ref[0] = (hf_scr[...] + hr[...]).astype(out_ref.dtype)


def _lru_call(xc, lw):
    b, s, _ = xc.shape
    kern = functools.partial(_lru_kernel, s=s, tc=LRU_CHUNK)
    seq = pltpu.VMEM((s, LANES), F32)
    return pl.pallas_call(
        kern, out_shape=jax.ShapeDtypeStruct((b, s, W_GROUP), BF16),
        grid=(b, W_GROUP // LANES),
        in_specs=[pl.BlockSpec((1, s, LANES), lambda bb, hf: (bb, 0, hf)),
                  pl.BlockSpec((CONV_W, LANES), lambda bb, hf: (0, hf)),
                  pl.BlockSpec((1, LANES), lambda bb, hf: (0, hf)),
                  pl.BlockSpec((1, LANES, 4 * LANES), lambda bb, hf: (hf, 0, 0)),
                  pl.BlockSpec((1, 1, 4 * LANES), lambda bb, hf: (hf, 0, 0)),
                  pl.BlockSpec((1, 2, LANES), lambda bb, hf: (hf, 0, 0))],
        out_specs=pl.BlockSpec((1, s, LANES), lambda bb, hf: (bb, 0, hf)),
        scratch_shapes=[pltpu.VMEM((s + 2 * SUBLANES, LANES), F32), seq, seq, seq, seq, seq, seq],
        compiler_params=_cparams(2), name="rglru",
    )(xc, lw["conv_w"], lw["conv_b"], lw["lru_wbd"], lw["lru_bias"], lw["lru_lam"])


def _post_kernel(x_ref, oa_ref, ob_ref, oc_ref, od_ref, g_ref, gate_ref, wout_ref, fn_ref, out_ref, *, final):
    y = None
    for i, o_ref in enumerate((oa_ref, ob_ref, oc_ref, od_ref)):
        og = (o_ref[0].astype(F32) * g_ref[0, :, W_GROUP * i:W_GROUP * (i + 1)].astype(F32)).astype(BF16)
        part = jnp.dot(og, wout_ref[W_GROUP * i:W_GROUP * (i + 1), :], preferred_element_type=F32)
        y = part if y is None else y + part
    xn = x_ref[0] + gate_ref[0] * y
    if final:
        ms = jnp.mean(xn * xn, axis=-1, keepdims=True)
        xn = (xn * lax.rsqrt(ms + EPS)) * fn_ref[...]
    out_ref[0] = xn


def _post_call(x, oa, ob, oc, od, g, gate, w_out_bf, final_norm, final):
    b, s, d = x.shape
    ts = TOK_TILE
    tok = lambda w: pl.BlockSpec((1, ts, w), lambda bb, i: (bb, i, 0))
    return pl.pallas_call(
        functools.partial(_post_kernel, final=final),
        out_shape=jax.ShapeDtypeStruct((b, s, d), F32),
        grid=(b, s // ts),
        in_specs=[tok(d), tok(W_GROUP), tok(W_GROUP), tok(W_GROUP), tok(W_GROUP), tok(4 * W_GROUP),
                  pl.BlockSpec((1, 1, d), lambda bb, i: (bb, 0, 0)),
                  pl.BlockSpec((d, d), lambda bb, i: (0, 0)),
                  pl.BlockSpec((1, d), lambda bb, i: (0, 0))],
        out_specs=tok(d),
        compiler_params=_cparams(2), name="post_proj",
    )(x, oa, ob, oc, od, g, gate, w_out_bf, final_norm)


def _angles(pos, half, theta):
    inv = jnp.power(jnp.float32(theta), -jnp.arange(half, dtype=F32) / half)
    return pos[:, None] * inv[None, :]


def _tables(s):
    pos = jnp.arange(s, dtype=F32)
    rows = s // GRID_W
    row = jnp.repeat(jnp.arange(rows, dtype=F32), GRID_W)
    col = jnp.tile(jnp.arange(GRID_W, dtype=F32), rows)
    a = _angles(pos, ROPE_A // 2, MLA_ROPE_THETA)
    c, sn = jnp.cos(a), jnp.sin(a)
    z64, z32 = jnp.zeros((s, 64), F32), jnp.zeros((s, 32), F32)
    tb = {"mkc": jnp.concatenate([z64, c, c, z32], axis=1),
          "mks": jnp.concatenate([z64, -sn, sn, z32], axis=1),
          "mqc": jnp.concatenate([c, c], axis=1).T,
          "mqs": jnp.concatenate([-sn, sn], axis=1).T}
    a = _angles(pos, DK_B // 8, ROPE_THETA)
    c, sn = jnp.cos(a), jnp.sin(a)
    c32 = jnp.concatenate([c, c, jnp.ones((s, DK_B - DK_B // 4), F32)], axis=1)
    s32 = jnp.concatenate([-sn, sn, jnp.zeros((s, DK_B - DK_B // 4), F32)], axis=1)
    tb.update({"dkc": jnp.tile(c32, (1, 2 * H_B)), "dks": jnp.tile(s32, (1, 2 * H_B)),
               "dqc": c32.T, "dqs": s32.T})
    ar_ = _angles(row, HD_D // 4, AXIAL_THETA)
    ac_ = _angles(col, HD_D // 4, AXIAL_THETA)
    cr, sr, cc, sc = jnp.cos(ar_), jnp.sin(ar_), jnp.cos(ac_), jnp.sin(ac_)
    c64 = jnp.concatenate([cr, cr, cc, cc], axis=1)
    s64 = jnp.concatenate([-sr, sr, -sc, sc], axis=1)
    tb.update({"gkc": jnp.tile(c64, (1, KV_H_D)), "gks": jnp.tile(s64, (1, KV_H_D)),
               "gqc": c64.T, "gqs": s64.T})
    return tb


def _blockdiag_pair(w, half):
    z = jnp.zeros((BW_C, BW_C), F32)
    top = jnp.concatenate([w[2 * half], z], axis=1)
    bot = jnp.concatenate([z, w[2 * half + 1]], axis=1)
    return jnp.concatenate([top, bot], axis=0)


def _layer_weights(l, p):
    w = p["w_in"][l]
    col = lambda o, n: w[:, o:o + n]
    z64 = jnp.zeros((D_MODEL, 64), F32)
    z32 = jnp.zeros((D_MODEL, 32), F32)
    kra = col(O_KRA, ROPE_A)
    w_std = jnp.concatenate([
        col(O_KVA, KV_LORA),
        z64, kra, z32,
        z64, kra[:, P_MLA], z32,
        col(O_KB, 256), col(O_KD, 128),
        col(O_XC, W_GROUP),
        col(O_GA, W_GROUP), col(O_GB, W_GROUP), col(O_GC, W_GROUP), col(O_GD, W_GROUP)], axis=1)
    w_t = jnp.concatenate([
        col(O_QA, Q_LORA), col(O_KVA, KV_LORA),
        col(O_QB, 256), col(O_VB, 256),
        col(O_QD, 256), col(O_VD, 128)], axis=1).T
    wuq = p["mla_w_uq"][l]
    wukv = p["mla_w_ukv"][l]
    dkv = NOPE_A + V_A
    z_k = jnp.zeros((KV_LORA, LANES - NOPE_A), F32)
    w_ukp = jnp.concatenate(
        [jnp.concatenate([wukv[:, dkv * h:dkv * h + NOPE_A], z_k], axis=1) for h in range(H_A)], axis=1)
    w_uv = jnp.concatenate([wukv[:, dkv * h + NOPE_A:dkv * (h + 1)] for h in range(H_A)], axis=1)
    gq, gk = p["gqa_q_norm"][l], p["gqa_k_norm"][l]
    wa, wx = p["lru_wa"][l], p["lru_wx"][l]
    ba, bx = p["lru_ba"][l], p["lru_bx"][l]
    wbd = jnp.stack([jnp.concatenate([_blockdiag_pair(wa[0], hf), _blockdiag_pair(wx[0], hf),
                                      _blockdiag_pair(wa[1], hf), _blockdiag_pair(wx[1], hf)], axis=1)
                     for hf in range(2)])
    sl = lambda v, hf: v[LANES * hf:LANES * (hf + 1)]
    bias = jnp.stack([jnp.concatenate([sl(ba[0], hf), sl(bx[0], hf), sl(ba[1], hf), sl(bx[1], hf)])[None]
                      for hf in range(2)])
    lam = p["lru_lambda"][l]
    lam_h = jnp.stack([jnp.stack([sl(lam[0], hf), sl(lam[1], hf)]) for hf in range(2)])
    return {
        "norm_g": p["norm_g"][l][None], "w_std": w_std.astype(BF16), "w_t": w_t.astype(BF16),
        "gqa_col": p["mla_q_norm"][l][:, None], "w_uqT": wuq.T.astype(BF16),
        "gkv_row": p["mla_kv_norm"][l][None], "gkv_col": p["mla_kv_norm"][l][:, None],
        "w_ukp": w_ukp.astype(BF16), "w_uvT": w_uv.T.astype(BF16),
        "gq_col": gq[:, None], "gk_row": jnp.tile(gk, KV_H_D)[None],
        "diff_lambda": p["diff_lambda"][l], "subln_col": p["diff_subln"][l][:, None],
        "conv_w": p["lru_conv_w"][l], "conv_b": p["lru_conv_b"][l][None],
        "lru_wbd": wbd.astype(BF16), "lru_bias": bias, "lru_lam": lam_h,
        "w_out": p["w_out"][l].astype(BF16),
    }


def _trunk(x, mods, lws, final_norm):
    b, s, d = x.shape
    tb = _tables(s)
    for l in range(DEPTH):
        lw = lws[l]
        shift, scale, gate = mods[l]
        (qt_mla, k_mla, vt_mla, qt_diff, k_diff, vt_diff, qt_gqa, k_gqa, vt_gqa, xc, g,
         kn_mla, kn_diff, kn_gqa) = _pre_call(x, shift, scale, lw, tb)
        lam_init = 0.8 - 0.6 * math.exp(-0.3 * l)
        oa = _attn_call(qt_mla, k_mla, vt_mla, kn_mla, MAPS_MLA, "mla", name="attn_mla")
        ob = _attn_call(qt_diff, k_diff, vt_diff, kn_diff, MAPS_DIFF, "diff",
                        extra=(lw["diff_lambda"], lw["subln_col"]), lam_init=lam_init, name="attn_diff")
        od = _attn_call(qt_gqa, k_gqa, vt_gqa, kn_gqa, MAPS_GQA, "gqa", name="attn_gqa")
        oc = _lru_call(xc, lw)
        x = _post_call(x, oa, ob, oc, od, g, gate, lw["w_out"], final_norm, final=(l == DEPTH - 1))
    return x


def kernel(x_prompt, x_sample, c_prompt, c_sample, ada_w, ada_b, norm_g, w_in, mla_q_norm, mla_w_uq, mla_kv_norm, mla_w_ukv, diff_lambda, diff_subln, lru_conv_w, lru_conv_b, lru_wa, lru_ba, lru_wx, lru_bx, lru_lambda, gqa_q_norm, gqa_k_norm, w_out, final_norm):
    p = dict(norm_g=norm_g, w_in=w_in, mla_q_norm=mla_q_norm, mla_w_uq=mla_w_uq, mla_kv_norm=mla_kv_norm,
             mla_w_ukv=mla_w_ukv, diff_lambda=diff_lambda, diff_subln=diff_subln,
             lru_conv_w=lru_conv_w, lru_conv_b=lru_conv_b, lru_wa=lru_wa, lru_ba=lru_ba,
             lru_wx=lru_wx, lru_bx=lru_bx, lru_lambda=lru_lambda,
             gqa_q_norm=gqa_q_norm, gqa_k_norm=gqa_k_norm, w_out=w_out)
    lws = [_layer_weights(l, p) for l in range(DEPTH)]
    nb = x_prompt.shape[0]
    mod = _modulation(jnp.concatenate([c_prompt, c_sample], axis=0), ada_w.astype(BF16), ada_b)

    def mods_for(lo, hi):
        out = []
        for l in range(DEPTH):
            m = mod[l, lo:hi]
            out.append(tuple(m[:, D_MODEL * j:D_MODEL * (j + 1)][:, None, :] for j in range(3)))
        return out

    fn = final_norm[None]
    y_prompt = _trunk(x_prompt, mods_for(0, nb), lws, fn)
    y_sample = _trunk(x_sample, mods_for(nb, nb + x_sample.shape[0]), lws, fn)
    return (y_prompt, y_sample)
```

```python
import functools
import math

import jax
import jax.numpy as jnp
import numpy as np
from jax import lax
from jax.experimental import pallas as pl
from jax.experimental.pallas import tpu as pltpu

F32 = jnp.float32
BF16 = jnp.bfloat16

D_MODEL = 1024
DEPTH = 2
W_GROUP = 256
H_A, NOPE_A, ROPE_A, V_A = 4, 64, 32, 64
Q_LORA, KV_LORA = 192, 128
H_B, DK_B, DV_B = 4, 32, 64
H_C, BW_C, CONV_W = 4, 64, 4
LRU_C = 8.0
H_D, KV_H_D, HD_D = 4, 2, 64
GRID_W = 64
ROPE_THETA = 500000.0
MLA_ROPE_THETA = 10000.0
AXIAL_THETA = 10000.0
EPS = 1e-6
LOG2E = 1.4426950408889634

IN_SIZES = (Q_LORA, KV_LORA, ROPE_A, W_GROUP,
            H_B * 2 * DK_B, H_B * 2 * DK_B, H_B * DV_B, W_GROUP,
            W_GROUP, W_GROUP,
            H_D * HD_D, KV_H_D * HD_D, KV_H_D * HD_D, W_GROUP)
_OFF = np.concatenate([[0], np.cumsum(IN_SIZES)]).tolist()
(O_QA, O_KVA, O_KRA, O_GA, O_QB, O_KB, O_VB, O_GB, O_XC, O_GC, O_QD, O_KD, O_VD, O_GD, _) = _OFF

LANES = 128
SUBLANES = 8

TOK_TILE = 512
Q_TILE = 512
Q_TILES_PER_STEP = 2
LRU_CHUNK = 512
DENOM_LOG2_RANGE = 80
SUB_KEYS = 256
SCORE_LEAD = 2
VMEM_LIMIT = 56 * 1024 * 1024

ZS_KVL, ZS_KPE, ZS_KPE_SW, ZS_KB, ZS_KD, ZS_XC, ZS_G, ZS_END = (0, 128, 256, 384, 640, 768, 1024, 2048)
ZT_QA, ZT_KVL, ZT_QB, ZT_VB, ZT_QD, ZT_VD, ZT_END = (0, 192, 320, 576, 832, 1088, 1216)


def _partner(width, block, rot, half):
    idx = np.arange(width)
    d = idx % block
    out = idx.copy()
    in_rot = d < rot
    first = in_rot & ((d % (2 * half)) < half)
    second = in_rot & ~((d % (2 * half)) < half)
    out[first] = idx[first] + half
    out[second] = idx[second] - half
    return out


P_MLA = _partner(ROPE_A, ROPE_A, ROPE_A, ROPE_A // 2)


def _indicator(width, seg):
    m = np.zeros((width, LANES), np.float32)
    m[np.arange(width), np.arange(width) // seg] = 1.0
    return m


IND_MLA = _indicator(H_A * LANES, LANES)
IND_DIFF = _indicator(2 * H_B * DK_B, DK_B)
IND_GQA = _indicator(KV_H_D * HD_D, HD_D)


def _cparams(n_axes):
    return pltpu.CompilerParams(dimension_semantics=("arbitrary",) * n_axes,
                                vmem_limit_bytes=VMEM_LIMIT)


def _mod_kernel(c_ref, w_ref, b_ref, out_ref):
    c = c_ref[...]
    sc = (c * jax.nn.sigmoid(c)).astype(BF16)
    out_ref[0] = jnp.dot(sc, w_ref[0], preferred_element_type=F32) + b_ref[0]


def _modulation(c_all, ada_w_bf, ada_b):
    n = c_all.shape[0]
    return pl.pallas_call(
        _mod_kernel,
        out_shape=jax.ShapeDtypeStruct((DEPTH, n, 3 * D_MODEL), F32),
        grid=(DEPTH, 3),
        in_specs=[
            pl.BlockSpec((n, D_MODEL), lambda l, j: (0, 0)),
            pl.BlockSpec((1, D_MODEL, D_MODEL), lambda l, j: (l, 0, j)),
            pl.BlockSpec((1, 1, D_MODEL), lambda l, j: (l, 0, j)),
        ],
        out_specs=pl.BlockSpec((1, n, D_MODEL), lambda l, j: (l, 0, j)),
        compiler_params=_cparams(2),
        name="adaln_mod",
    )(c_all, ada_w_bf, ada_b.reshape(DEPTH, 1, 3 * D_MODEL))


def _pre_kernel(x_ref, *rest):
    _pre_body(x_ref[0], *rest)


def _post_pre_kernel(x_ref, oa_ref, ob_ref, oc_ref, od_ref, gin_ref, gate_ref, wout_ref, *rest):
    xn = _post_update(x_ref, (oa_ref, ob_ref, oc_ref, od_ref), gin_ref, gate_ref, wout_ref)
    rest[-1][0] = xn
    _pre_body(xn, *rest[:-1])


def _pre_body(x, shift_ref, scale_ref, ng_ref, wstd_ref, wt_ref,
              gqa_col_ref, wuqT_ref, gkv_row_ref, gkv_col_ref, wukp_ref, wuvT_ref,
              gq_col_ref, gk_row_ref,
              mkc_ref, mks_ref, mqc_ref, mqs_ref,
              dkc_ref, dks_ref, dqc_ref, dqs_ref,
              gkc_ref, gks_ref, gqc_ref, gqs_ref,
              ind_mla_ref, ind_diff_ref, ind_gqa_ref,
              qt_mla_ref, k_mla_ref, vt_mla_ref,
              qt_diff_ref, k_diff_ref, vt_diff_ref,
              qt_gqa_ref, k_gqa_ref, vt_gqa_ref,
              xc_ref, g_ref, kn_mla_ref, kn_diff_ref, kn_gqa_ref):
    ts = x.shape[0]

    def key_norms(kb, ind_ref, out_ref):
        kf = kb.astype(F32)
        r = jnp.dot((kf * kf).astype(BF16), ind_ref[...], preferred_element_type=F32)
        out_ref[0, 0] = jnp.broadcast_to(jnp.max(r, axis=0, keepdims=True), (SUBLANES, LANES))

    ms = jnp.mean(x * x, axis=-1, keepdims=True)
    h = (x * lax.rsqrt(ms + EPS)) * ng_ref[...]
    h = h * (1.0 + scale_ref[0]) + shift_ref[0]
    hb = h.astype(BF16)
    z = jnp.dot(hb, wstd_ref[...], preferred_element_type=F32)
    zt = lax.dot_general(wt_ref[...], hb, (((1,), (1,)), ((), ())),
                         preferred_element_type=F32)

    qlt = zt[ZT_QA:ZT_QA + Q_LORA]
    msq = jnp.mean(qlt * qlt, axis=0, keepdims=True)
    qnb = ((qlt * lax.rsqrt(msq + EPS)) * gqa_col_ref[...]).astype(BF16)
    qat = jnp.dot(wuqT_ref[...], qnb, preferred_element_type=F32)
    sc_mla = (NOPE_A + ROPE_A) ** -0.5 * LOG2E
    dq = NOPE_A + ROPE_A
    pad = jnp.zeros((LANES - dq, ts), F32)
    for hh in range(H_A):
        nope = qat[dq * hh:dq * hh + NOPE_A]
        pe = qat[dq * hh + NOPE_A:dq * (hh + 1)]
        sw = jnp.concatenate([pe[ROPE_A // 2:], pe[:ROPE_A // 2]], axis=0)
        pe = pe * mqc_ref[...] + sw * mqs_ref[...]
        blk = jnp.concatenate([nope, pe, pad], axis=0) * sc_mla
        qt_mla_ref[0, LANES * hh:LANES * (hh + 1), :] = blk.astype(BF16)

    kvl = z[:, ZS_KVL:ZS_KVL + KV_LORA]
    msk = jnp.mean(kvl * kvl, axis=-1, keepdims=True)
    kvn = ((kvl * lax.rsqrt(msk + EPS)) * gkv_row_ref[...]).astype(BF16)
    knope = jnp.dot(kvn, wukp_ref[...], preferred_element_type=F32)
    kpe = z[:, ZS_KPE:ZS_KPE + LANES] * mkc_ref[...] + z[:, ZS_KPE_SW:ZS_KPE_SW + LANES] * mks_ref[...]
    kb = jnp.concatenate([(knope[:, LANES * hh:LANES * (hh + 1)] + kpe).astype(BF16) for hh in range(H_A)],
                         axis=1)
    k_mla_ref[0] = kb
    key_norms(kb, ind_mla_ref, kn_mla_ref)
    kvlt = zt[ZT_KVL:ZT_KVL + KV_LORA]
    mst = jnp.mean(kvlt * kvlt, axis=0, keepdims=True)
    kvnt = ((kvlt * lax.rsqrt(mst + EPS)) * gkv_col_ref[...]).astype(BF16)
    vt_mla_ref[0, 0] = jnp.dot(wuvT_ref[...], kvnt, preferred_element_type=F32).astype(BF16)

    kd = z[:, ZS_KB:ZS_KB + 256]
    lane_d = lax.broadcasted_iota(jnp.int32, (ts, 256), 1)
    half_d = DK_B // 8
    kd_sw = jnp.where(lane_d % (2 * half_d) < half_d, pltpu.roll(kd, 256 - half_d, 1), pltpu.roll(kd, half_d, 1))
    kb = (kd * dkc_ref[...] + kd_sw * dks_ref[...]).astype(BF16)
    k_diff_ref[0] = kb
    key_norms(kb, ind_diff_ref, kn_diff_ref)
    sc_diff = DK_B ** -0.5 * LOG2E
    rot = DK_B // 4
    for m in range(2 * H_B):
        q = zt[ZT_QB + DK_B * m:ZT_QB + DK_B * (m + 1)]
        q8 = q[0:rot]
        q8 = q8 * dqc_ref[0:rot, :] + pltpu.roll(q8, rot // 2, 0) * dqs_ref[0:rot, :]
        qq = jnp.concatenate([q8, q[rot:]], axis=0) * sc_diff
        qt_diff_ref[0, DK_B * m:DK_B * (m + 1), :] = qq.astype(BF16)
    vt_diff_ref[0, 0] = zt[ZT_VB:ZT_VB + 256].astype(BF16)

    kg = z[:, ZS_KD:ZS_KD + LANES]
    k2 = kg * kg
    lo = lax.broadcasted_iota(jnp.int32, (ts, LANES), 1) < HD_D
    s_lo = jnp.sum(jnp.where(lo, k2, 0.0), axis=-1, keepdims=True)
    s_hi = jnp.sum(jnp.where(lo, 0.0, k2), axis=-1, keepdims=True)
    rk = lax.rsqrt(jnp.where(lo, s_lo, s_hi) / HD_D + EPS)
    kn = (kg * rk) * gk_row_ref[...]
    half_g = HD_D // 4
    lane_g = lax.broadcasted_iota(jnp.int32, (ts, LANES), 1)
    knsw = jnp.where(lane_g % (2 * half_g) < half_g, pltpu.roll(kn, LANES - half_g, 1), pltpu.roll(kn, half_g, 1))
    kb = (kn * gkc_ref[...] + knsw * gks_ref[...]).astype(BF16)
    k_gqa_ref[0] = kb
    key_norms(kb, ind_gqa_ref, kn_gqa_ref)
    sc_gqa = HD_D ** -0.5 * LOG2E
    for hh in range(H_D):
        q = zt[ZT_QD + HD_D * hh:ZT_QD + HD_D * (hh + 1)]
        r = lax.rsqrt(jnp.mean(q * q, axis=0, keepdims=True) + EPS)
        qn = (q * r) * gq_col_ref[...]
        qsw = jnp.concatenate([qn[half_g:2 * half_g], qn[0:half_g], qn[3 * half_g:], qn[2 * half_g:3 * half_g]],
                              axis=0)
        qq = (qn * gqc_ref[...] + qsw * gqs_ref[...]) * sc_gqa
        qt_gqa_ref[0, HD_D * hh:HD_D * (hh + 1), :] = qq.astype(BF16)
    vt_gqa_ref[0, 0] = zt[ZT_VD:ZT_VD + 128].astype(BF16)

    xc_ref[0] = z[:, ZS_XC:ZS_XC + W_GROUP]
    g = z[:, ZS_G:ZS_END]
    g_ref[0] = (g * jax.nn.sigmoid(g)).astype(BF16)


def _pre_call(x, shift, scale, lw, tb, prev=None):
    b, s, d = x.shape
    ts = TOK_TILE
    nt = s // ts
    full2 = lambda a: pl.BlockSpec(a.shape, lambda i, bb: (0, 0))
    tok_rows = lambda w: pl.BlockSpec((ts, w), lambda i, bb: (i, 0))
    tok_cols = lambda r: pl.BlockSpec((r, ts), lambda i, bb: (0, i))
    tok3 = lambda w: pl.BlockSpec((1, ts, w), lambda i, bb: (bb, i, 0))
    vec3 = pl.BlockSpec((1, 1, d), lambda i, bb: (bb, 0, 0))
    params = [lw["norm_g"], lw["w_std"], lw["w_t"],
              lw["gqa_col"], lw["w_uqT"], lw["gkv_row"], lw["gkv_col"], lw["w_ukp"], lw["w_uvT"],
              lw["gq_col"], lw["gk_row"]]
    in_specs = [tok3(d)]
    head = [x]
    if prev is not None:
        in_specs += [tok3(W_GROUP)] * 4 + [tok3(4 * W_GROUP), vec3, full2(prev[6])]
        head += list(prev)
    in_specs += [vec3, vec3]
    in_specs += [full2(p) for p in params]
    in_specs += [tok_rows(128), tok_rows(128), tok_cols(32), tok_cols(32),
                 tok_rows(256), tok_rows(256), tok_cols(32), tok_cols(32),
                 tok_rows(128), tok_rows(128), tok_cols(64), tok_cols(64)]
    tables = [tb["mkc"], tb["mks"], tb["mqc"], tb["mqs"],
              tb["dkc"], tb["dks"], tb["dqc"], tb["dqs"],
              tb["gkc"], tb["gks"], tb["gqc"], tb["gqs"]]
    inds = [jnp.asarray(IND_MLA, BF16), jnp.asarray(IND_DIFF, BF16), jnp.asarray(IND_GQA, BF16)]
    in_specs += [full2(a) for a in inds]
    kn_shape = jax.ShapeDtypeStruct((b, nt, SUBLANES, LANES), F32)
    kn_spec = pl.BlockSpec((1, 1, SUBLANES, LANES), lambda i, bb: (bb, i, 0, 0))

    def qt_spec(rows):
        return pl.BlockSpec((1, rows, ts), lambda i, bb: (bb, 0, i))

    def k_spec(w):
        return pl.BlockSpec((1, ts, w), lambda i, bb: (bb, i, 0))

    def vt_spec(rows):
        return pl.BlockSpec((1, 1, rows, ts), lambda i, bb: (bb, i, 0, 0))

    out_shape = [
        jax.ShapeDtypeStruct((b, 512, s), BF16), jax.ShapeDtypeStruct((b, s, 512), BF16),
        jax.ShapeDtypeStruct((b, nt, 256, ts), BF16),
        jax.ShapeDtypeStruct((b, 256, s), BF16), jax.ShapeDtypeStruct((b, s, 256), BF16),
        jax.ShapeDtypeStruct((b, nt, 256, ts), BF16),
        jax.ShapeDtypeStruct((b, 256, s), BF16), jax.ShapeDtypeStruct((b, s, 128), BF16),
        jax.ShapeDtypeStruct((b, nt, 128, ts), BF16),
        jax.ShapeDtypeStruct((b, s, 256), F32), jax.ShapeDtypeStruct((b, s, 1024), BF16),
        kn_shape, kn_shape, kn_shape,
    ]
    out_specs = [qt_spec(512), k_spec(512), vt_spec(256),
                 qt_spec(256), k_spec(256), vt_spec(256),
                 qt_spec(256), k_spec(128), vt_spec(128),
                 k_spec(256), k_spec(1024), kn_spec, kn_spec, kn_spec]
    if prev is not None:
        out_shape.append(jax.ShapeDtypeStruct((b, s, d), F32))
        out_specs.append(tok3(d))
    return pl.pallas_call(
        _pre_kernel if prev is None else _post_pre_kernel,
        out_shape=out_shape, grid=(nt, b), in_specs=in_specs, out_specs=out_specs,
        compiler_params=_cparams(2), name="pre_proj" if prev is None else "post_pre_proj",
    )(*head, shift, scale, *params, *tables, *inds)


def _attn_kernel(*refs, maps, n_chunks, tk, tq, kind, lam_init):
    if kind == "diff":
        qt_ref, k_ref, vt_ref, kn_ref, lam_ref, subln_ref, out_ref, w_scr = refs
    else:
        qt_ref, k_ref, vt_ref, kn_ref, out_ref, w_scr = refs
        lam_ref = subln_ref = None
    n_tiles = qt_ref.shape[2] // tq
    dv = 64
    nm = len(maps)
    finish = functools.partial(_attn_finish, out_ref=out_ref, lam_ref=lam_ref, subln_ref=subln_ref,
                               kind=kind, lam_init=lam_init, tq=tq)

    for t in range(n_tiles):
        for i, (kg, q0, qn, woff, v0, _) in enumerate(maps):
            q = qt_ref[0, q0:q0 + qn, tq * t:tq * (t + 1)]
            pieces = []
            if woff > 0:
                pieces.append(jnp.zeros((woff, tq), BF16))
            pieces.append(q)
            if LANES - woff - qn > 0:
                pieces.append(jnp.zeros((LANES - woff - qn, tq), BF16))
            w_scr[t, i] = pieces[0] if len(pieces) == 1 else jnp.concatenate(pieces, axis=0)

    def scores(t, i, c, r1=0, rows=tk):
        kg = maps[i][0]
        r0 = c * tk + r1 if isinstance(c, int) else pl.multiple_of(c * tk, tk)
        return jnp.dot(k_ref[0, pl.ds(r0, rows), LANES * kg:LANES * (kg + 1)], w_scr[t, i],
                       preferred_element_type=F32)

    def pv(i, c, p, r1=0, rows=tk):
        v0 = maps[i][4]
        return jnp.dot(vt_ref[0, c, v0:v0 + dv, r1:r1 + rows], p.astype(BF16),
                       preferred_element_type=F32)

    kn2 = jnp.max(kn_ref[0], axis=0)[0:1]
    lane = lax.broadcasted_iota(jnp.int32, (1, LANES), 1)
    kmax = [jnp.sqrt(jnp.max(jnp.where(lane == maps[i][5], kn2, 0.0), axis=-1, keepdims=True))
            for i in range(nm)]
    bounds = []
    for t in range(n_tiles):
        for i in range(nm):
            w = w_scr[t, i].astype(F32)
            bounds.append(jnp.sqrt(jnp.sum(w * w, axis=0, keepdims=True)) * kmax[i])
    units = [(t, i, c, r1) for t in range(n_tiles) for c in range(n_chunks)
             for r1 in range(0, tk, SUB_KEYS) for i in range(nm)]
    per_tile = len(units) // n_tiles
    accs, ls = [None] * (n_tiles * nm), [None] * (n_tiles * nm)
    pending = []
    lo, hi = 2.0 ** -DENOM_LOG2_RANGE, 2.0 ** DENOM_LOG2_RANGE
    for n in range(len(units) + SCORE_LEAD):
        if n < len(units):
            t, i, c, r1 = units[n]
            pending.append(scores(t, i, c, r1, SUB_KEYS))
        if n >= SCORE_LEAD:
            t, i, c, r1 = units[n - SCORE_LEAD]
            j = t * nm + i
            p = jnp.exp2(pending.pop(0) - bounds[j])
            part, psum = pv(i, c, p, r1, SUB_KEYS), jnp.sum(p, axis=0, keepdims=True)
            accs[j] = part if accs[j] is None else accs[j] + part
            ls[j] = psum if ls[j] is None else ls[j] + psum
            if (n - SCORE_LEAD + 1) % per_tile == 0:
                finish(t, accs[t * nm:(t + 1) * nm], ls[t * nm:(t + 1) * nm])

    for t in range(n_tiles):
        bad = None
        for l in ls[t * nm:(t + 1) * nm]:
            b = jnp.where((l > lo) & (l < hi), 0.0, 1.0)
            bad = b if bad is None else jnp.maximum(bad, b)

        @pl.when(jnp.max(bad) > 0.0)
        def _(t=t):
            def safe_body(c, carry):
                new = []
                for i in range(nm):
                    m, l, acc = carry[i]
                    s = scores(t, i, c)
                    mn = jnp.maximum(m, jnp.max(s, axis=0, keepdims=True))
                    p, alpha = jnp.exp2(s - mn), jnp.exp2(m - mn)
                    new.append((mn, alpha * l + jnp.sum(p, axis=0, keepdims=True), alpha * acc + pv(i, c, p)))
                return tuple(new)

            init = tuple((jnp.full((1, tq), -jnp.inf, F32), jnp.zeros((1, tq), F32), jnp.zeros((dv, tq), F32))
                         for _ in range(nm))
            fs = lax.fori_loop(0, n_chunks, safe_body, init)
            finish(t, [f[2] for f in fs], [f[1] for f in fs])


def _attn_finish(t, accs, ls, out_ref, lam_ref, subln_ref, kind, lam_init, tq):
    os_ = [acc / l for acc, l in zip(accs, ls)]

    if kind == "diff":
        lf = lam_ref[...]
        s1 = jnp.sum(lf[0:1] * lf[1:2], axis=-1, keepdims=True)
        s2 = jnp.sum(lf[2:3] * lf[3:4], axis=-1, keepdims=True)
        lam_full = jnp.exp(s1) - jnp.exp(s2) + lam_init
        heads = []
        for hh in range(H_B):
            o = os_[2 * hh] - lam_full * os_[2 * hh + 1]
            r = lax.rsqrt(jnp.mean(o * o, axis=0, keepdims=True) + EPS)
            heads.append(((o * r) * subln_ref[...]) * (1.0 - lam_init))
        ot = jnp.concatenate(heads, axis=0)
    else:
        ot = jnp.concatenate(os_, axis=0)
    out_ref[0, tq * t:tq * (t + 1), :] = ot.T.astype(out_ref.dtype)


def _attn_call(qt, k, vt, kn, maps, kind, extra=(), lam_init=0.0, name="attn"):
    b, fq, s = qt.shape
    fk = k.shape[2]
    _, nt, fv, tk = vt.shape
    tq = Q_TILE
    tqs = tq * Q_TILES_PER_STEP
    in_specs = [pl.BlockSpec((1, fq, tqs), lambda bb, qi: (bb, 0, qi)),
                pl.BlockSpec((1, s, fk), lambda bb, qi: (bb, 0, 0)),
                pl.BlockSpec((1, nt, fv, tk), lambda bb, qi: (bb, 0, 0, 0)),
                pl.BlockSpec((1, nt, SUBLANES, LANES), lambda bb, qi: (bb, 0, 0, 0))]
    in_specs += [pl.BlockSpec(e.shape, lambda bb, qi: (0, 0)) for e in extra]
    kern = functools.partial(_attn_kernel, maps=tuple(maps), n_chunks=nt, tk=tk, tq=tq, kind=kind,
                             lam_init=lam_init)
    return pl.pallas_call(
        kern, out_shape=jax.ShapeDtypeStruct((b, s, W_GROUP), BF16),
        grid=(b, s // tqs), in_specs=in_specs,
        out_specs=pl.BlockSpec((1, tqs, W_GROUP), lambda bb, qi: (bb, qi, 0)),
        scratch_shapes=[pltpu.VMEM((Q_TILES_PER_STEP, len(maps), LANES, tq), BF16)],
        compiler_params=_cparams(2), name=name,
    )(qt, k, vt, kn, *extra)


MAPS_MLA = [(hh, LANES * hh, LANES, 0, V_A * hh, hh) for hh in range(H_A)]
MAPS_DIFF = [(m // 4, DK_B * m, DK_B, DK_B * (m % 4), DV_B * (m // 2), m) for m in range(2 * H_B)]
MAPS_GQA = [(0, HD_D * hh, HD_D, HD_D * (hh // (H_D // KV_H_D)), HD_D * (hh // (H_D // KV_H_D)),
             hh // (H_D // KV_H_D)) for hh in range(H_D)]


def _softplus(x):
    return jnp.maximum(x, 0.0) + jnp.log1p(jnp.exp(-jnp.abs(x)))


def _lru_kernel(xc_ref, cw_ref, cb_ref, wbd_ref, bias_ref, lam_ref, out_ref,
                xpad, af, gf, ar, gr, hf_scr, hr, *, s, tc):
    zero8 = jnp.zeros((SUBLANES, LANES), F32)
    xpad[0:SUBLANES, :] = zero8
    xpad[s + SUBLANES:s + 2 * SUBLANES, :] = zero8
    xpad[SUBLANES:s + SUBLANES, :] = xc_ref[0]
    lam = lam_ref[0]
    sp_f = _softplus(-lam[0:1])
    sp_r = _softplus(-lam[1:2])
    cw = cw_ref[...]
    cb = cb_ref[...]
    wbd = wbd_ref[0]
    bias = bias_ref[0]
    n = tc + 2 * SUBLANES

    def gates(pre_r, pre_i, sp, xconv):
        r = jax.nn.sigmoid(pre_r)
        i = jax.nn.sigmoid(pre_i)
        log_a = (-LRU_C * r) * sp
        a = jnp.exp(log_a)
        v = -jnp.tanh(log_a) * (a * a + 1.0)
        gx = jnp.where(v > 0.0, v * lax.rsqrt(v), 0.0) * (i * xconv)
        return a, gx

    def chunk(ci, carry):
        t0 = pl.multiple_of(ci * tc, tc)
        xw = xpad[pl.ds(t0, n), :]
        taps = [pltpu.roll(xw, 2, 0), pltpu.roll(xw, 1, 0), xw, pltpu.roll(xw, n - 1, 0)]
        xconv = cb + taps[0][SUBLANES:SUBLANES + tc] * cw[0:1]
        for j in range(1, CONV_W):
            xconv = xconv + taps[j][SUBLANES:SUBLANES + tc] * cw[j:j + 1]
        pre = jnp.dot(xconv.astype(BF16), wbd, preferred_element_type=F32) + bias
        a, gx = gates(pre[:, 0:128], pre[:, 128:256], sp_f, xconv)
        af[pl.ds(t0, tc), :] = a
        gf[pl.ds(t0, tc), :] = gx
        a, gx = gates(pre[:, 256:384], pre[:, 384:512], sp_r, xconv)
        ar[pl.ds(t0, tc), :] = a
        gr[pl.ds(t0, tc), :] = gx
        return carry

    lax.fori_loop(0, s // tc, chunk, 0)

    row = lax.broadcasted_iota(jnp.int32, (SUBLANES, LANES), 0)
    ng = s // SUBLANES

    def scan8(a, bv, reverse):
        for dd in (1, 2, 4):
            if reverse:
                ok = row < SUBLANES - dd
                sh = SUBLANES - dd
            else:
                ok = row >= dd
                sh = dd
            a_s = jnp.where(ok, pltpu.roll(a, sh, 0), 1.0)
            b_s = jnp.where(ok, pltpu.roll(bv, sh, 0), 0.0)
            bv = a * b_s + bv
            a = a * a_s
        return a, bv

    def scan_body(gi, carry):
        hf_prev, hr_prev = carry
        t0 = pl.multiple_of(gi * SUBLANES, SUBLANES)
        a, bv = scan8(af[pl.ds(t0, SUBLANES), :], gf[pl.ds(t0, SUBLANES), :], False)
        hf = bv + a * hf_prev
        hf_scr[pl.ds(t0, SUBLANES), :] = hf
        t1 = pl.multiple_of((ng - 1 - gi) * SUBLANES, SUBLANES)
        a, bv = scan8(ar[pl.ds(t1, SUBLANES), :], gr[pl.ds(t1, SUBLANES), :], True)
        hrv = bv + a * hr_prev
        hr[pl.ds(t1, SUBLANES), :] = hrv
        return (jnp.broadcast_to(hf[SUBLANES - 1:SUBLANES], (SUBLANES, LANES)),
                jnp.broadcast_to(hrv[0:1], (SUBLANES, LANES)))

    lax.fori_loop(0, ng, scan_body, (zero8, zero8), unroll=4)
    out_ref[0] = (hf_scr[...] + hr[...]).astype(out_ref.dtype)


def _lru_call(xc, lw):
    b, s, _ = xc.shape
    kern = functools.partial(_lru_kernel, s=s, tc=LRU_CHUNK)
    seq = pltpu.VMEM((s, LANES), F32)
    return pl.pallas_call(
        kern, out_shape=jax.ShapeDtypeStruct((b, s, W_GROUP), BF16),
        grid=(b, W_GROUP // LANES),
        in_specs=[pl.BlockSpec((1, s, LANES), lambda bb, hf: (bb, 0, hf)),
                  pl.BlockSpec((CONV_W, LANES), lambda bb, hf: (0, hf)),
                  pl.BlockSpec((1, LANES), lambda bb, hf: (0, hf)),
                  pl.BlockSpec((1, LANES, 4 * LANES), lambda bb, hf: (hf, 0, 0)),
                  pl.BlockSpec((1, 1, 4 * LANES), lambda bb, hf: (hf, 0, 0)),
                  pl.BlockSpec((1, 2, LANES), lambda bb, hf: (hf, 0, 0))],
        out_specs=pl.BlockSpec((1, s, LANES), lambda bb, hf: (bb, 0, hf)),
        scratch_shapes=[pltpu.VMEM((s + 2 * SUBLANES, LANES), F32), seq, seq, seq, seq, seq, seq],
        compiler_params=_cparams(2), name="rglru",
    )(xc, lw["conv_w"], lw["conv_b"], lw["lru_wbd"], lw["lru_bias"], lw["lru_lam"])


def _post_update(x_ref, o_refs, g_ref, gate_ref, wout_ref):
    y = None
    for i, o_ref in enumerate(o_refs):
        og = (o_ref[0].astype(F32) * g_ref[0, :, W_GROUP * i:W_GROUP * (i + 1)].astype(F32)).astype(BF16)
        part = jnp.dot(og, wout_ref[W_GROUP * i:W_GROUP * (i + 1), :], preferred_element_type=F32)
        y = part if y is None else y + part
    return x_ref[0] + gate_ref[0] * y


def _post_kernel(x_ref, oa_ref, ob_ref, oc_ref, od_ref, g_ref, gate_ref, wout_ref, fn_ref, out_ref):
    xn = _post_update(x_ref, (oa_ref, ob_ref, oc_ref, od_ref), g_ref, gate_ref, wout_ref)
    ms = jnp.mean(xn * xn, axis=-1, keepdims=True)
    out_ref[0] = (xn * lax.rsqrt(ms + EPS)) * fn_ref[...]


def _post_call(x, oa, ob, oc, od, g, gate, w_out_bf, final_norm):
    b, s, d = x.shape
    ts = TOK_TILE
    tok = lambda w: pl.BlockSpec((1, ts, w), lambda bb, i: (bb, i, 0))
    return pl.pallas_call(
        _post_kernel,
        out_shape=jax.ShapeDtypeStruct((b, s, d), F32),
        grid=(b, s // ts),
        in_specs=[tok(d), tok(W_GROUP), tok(W_GROUP), tok(W_GROUP), tok(W_GROUP), tok(4 * W_GROUP),
                  pl.BlockSpec((1, 1, d), lambda bb, i: (bb, 0, 0)),
                  pl.BlockSpec((d, d), lambda bb, i: (0, 0)),
                  pl.BlockSpec((1, d), lambda bb, i: (0, 0))],
        out_specs=tok(d),
        compiler_params=_cparams(2), name="post_proj",
    )(x, oa, ob, oc, od, g, gate, w_out_bf, final_norm)


def _angles(pos, half, theta):
    inv = jnp.power(jnp.float32(theta), -jnp.arange(half, dtype=F32) / half)
    return pos[:, None] * inv[None, :]


def _tables(s):
    pos = jnp.arange(s, dtype=F32)
    rows = s // GRID_W
    row = jnp.repeat(jnp.arange(rows, dtype=F32), GRID_W)
    col = jnp.tile(jnp.arange(GRID_W, dtype=F32), rows)
    a = _angles(pos, ROPE_A // 2, MLA_ROPE_THETA)
    c, sn = jnp.cos(a), jnp.sin(a)
    z64, z32 = jnp.zeros((s, 64), F32), jnp.zeros((s, 32), F32)
    tb = {"mkc": jnp.concatenate([z64, c, c, z32], axis=1),
          "mks": jnp.concatenate([z64, -sn, sn, z32], axis=1),
          "mqc": jnp.concatenate([c, c], axis=1).T,
          "mqs": jnp.concatenate([-sn, sn], axis=1).T}
    a = _angles(pos, DK_B // 8, ROPE_THETA)
    c, sn = jnp.cos(a), jnp.sin(a)
    c32 = jnp.concatenate([c, c, jnp.ones((s, DK_B - DK_B // 4), F32)], axis=1)
    s32 = jnp.concatenate([-sn, sn, jnp.zeros((s, DK_B - DK_B // 4), F32)], axis=1)
    tb.update({"dkc": jnp.tile(c32, (1, 2 * H_B)), "dks": jnp.tile(s32, (1, 2 * H_B)),
               "dqc": c32.T, "dqs": s32.T})
    ar_ = _angles(row, HD_D // 4, AXIAL_THETA)
    ac_ = _angles(col, HD_D // 4, AXIAL_THETA)
    cr, sr, cc, sc = jnp.cos(ar_), jnp.sin(ar_), jnp.cos(ac_), jnp.sin(ac_)
    c64 = jnp.concatenate([cr, cr, cc, cc], axis=1)
    s64 = jnp.concatenate([-sr, sr, -sc, sc], axis=1)
    tb.update({"gkc": jnp.tile(c64, (1, KV_H_D)), "gks": jnp.tile(s64, (1, KV_H_D)),
               "gqc": c64.T, "gqs": s64.T})
    return tb


def _blockdiag_pair(w, half):
    z = jnp.zeros((BW_C, BW_C), F32)
    top = jnp.concatenate([w[2 * half], z], axis=1)
    bot = jnp.concatenate([z, w[2 * half + 1]], axis=1)
    return jnp.concatenate([top, bot], axis=0)


def _layer_weights(l, p):
    w = p["w_in"][l]
    col = lambda o, n: w[:, o:o + n]
    z64 = jnp.zeros((D_MODEL, 64), F32)
    z32 = jnp.zeros((D_MODEL, 32), F32)
    kra = col(O_KRA, ROPE_A)
    w_std = jnp.concatenate([
        col(O_KVA, KV_LORA),
        z64, kra, z32,
        z64, kra[:, P_MLA], z32,
        col(O_KB, 256), col(O_KD, 128),
        col(O_XC, W_GROUP),
        col(O_GA, W_GROUP), col(O_GB, W_GROUP), col(O_GC, W_GROUP), col(O_GD, W_GROUP)], axis=1)
    w_t = jnp.concatenate([
        col(O_QA, Q_LORA), col(O_KVA, KV_LORA),
        col(O_QB, 256), col(O_VB, 256),
        col(O_QD, 256), col(O_VD, 128)], axis=1).T
    wuq = p["mla_w_uq"][l]
    wukv = p["mla_w_ukv"][l]
    dkv = NOPE_A + V_A
    z_k = jnp.zeros((KV_LORA, LANES - NOPE_A), F32)
    w_ukp = jnp.concatenate(
        [jnp.concatenate([wukv[:, dkv * h:dkv * h + NOPE_A], z_k], axis=1) for h in range(H_A)], axis=1)
    w_uv = jnp.concatenate([wukv[:, dkv * h + NOPE_A:dkv * (h + 1)] for h in range(H_A)], axis=1)
    gq, gk = p["gqa_q_norm"][l], p["gqa_k_norm"][l]
    wa, wx = p["lru_wa"][l], p["lru_wx"][l]
    ba, bx = p["lru_ba"][l], p["lru_bx"][l]
    wbd = jnp.stack([jnp.concatenate([_blockdiag_pair(wa[0], hf), _blockdiag_pair(wx[0], hf),
                                      _blockdiag_pair(wa[1], hf), _blockdiag_pair(wx[1], hf)], axis=1)
                     for hf in range(2)])
    sl = lambda v, hf: v[LANES * hf:LANES * (hf + 1)]
    bias = jnp.stack([jnp.concatenate([sl(ba[0], hf), sl(bx[0], hf), sl(ba[1], hf), sl(bx[1], hf)])[None]
                      for hf in range(2)])
    lam = p["lru_lambda"][l]
    lam_h = jnp.stack([jnp.stack([sl(lam[0], hf), sl(lam[1], hf)]) for hf in range(2)])
    return {
        "norm_g": p["norm_g"][l][None], "w_std": w_std.astype(BF16), "w_t": w_t.astype(BF16),
        "gqa_col": p["mla_q_norm"][l][:, None], "w_uqT": wuq.T.astype(BF16),
        "gkv_row": p["mla_kv_norm"][l][None], "gkv_col": p["mla_kv_norm"][l][:, None],
        "w_ukp": w_ukp.astype(BF16), "w_uvT": w_uv.T.astype(BF16),
        "gq_col": gq[:, None], "gk_row": jnp.tile(gk, KV_H_D)[None],
        "diff_lambda": p["diff_lambda"][l], "subln_col": p["diff_subln"][l][:, None],
        "conv_w": p["lru_conv_w"][l], "conv_b": p["lru_conv_b"][l][None],
        "lru_wbd": wbd.astype(BF16), "lru_bias": bias, "lru_lam": lam_h,
        "w_out": p["w_out"][l].astype(BF16),
    }


def _trunk(x, mods, lws, final_norm):
    b, s, d = x.shape
    tb = _tables(s)
    prev = None
    for l in range(DEPTH):
        lw = lws[l]
        shift, scale, gate = mods[l]
        outs = _pre_call(x, shift, scale, lw, tb, prev)
        if prev is not None:
            x = outs[-1]
        (qt_mla, k_mla, vt_mla, qt_diff, k_diff, vt_diff, qt_gqa, k_gqa, vt_gqa, xc, g,
         kn_mla, kn_diff, kn_gqa) = outs[:14]
        lam_init = 0.8 - 0.6 * math.exp(-0.3 * l)
        oa = _attn_call(qt_mla, k_mla, vt_mla, kn_mla, MAPS_MLA, "mla", name="attn_mla")
        ob = _attn_call(qt_diff, k_diff, vt_diff, kn_diff, MAPS_DIFF, "diff",
                        extra=(lw["diff_lambda"], lw["subln_col"]), lam_init=lam_init, name="attn_diff")
        od = _attn_call(qt_gqa, k_gqa, vt_gqa, kn_gqa, MAPS_GQA, "gqa", name="attn_gqa")
        oc = _lru_call(xc, lw)
        prev = (oa, ob, oc, od, g, gate, lw["w_out"])
    return _post_call(x, *prev, final_norm)


def kernel(x_prompt, x_sample, c_prompt, c_sample, ada_w, ada_b, norm_g, w_in, mla_q_norm, mla_w_uq, mla_kv_norm, mla_w_ukv, diff_lambda, diff_subln, lru_conv_w, lru_conv_b, lru_wa, lru_ba, lru_wx, lru_bx, lru_lambda, gqa_q_norm, gqa_k_norm, w_out, final_norm):
    p = dict(norm_g=norm_g, w_in=w_in, mla_q_norm=mla_q_norm, mla_w_uq=mla_w_uq, mla_kv_norm=mla_kv_norm,
             mla_w_ukv=mla_w_ukv, diff_lambda=diff_lambda, diff_subln=diff_subln,
             lru_conv_w=lru_conv_w, lru_conv_b=lru_conv_b, lru_wa=lru_wa, lru_ba=lru_ba,
             lru_wx=lru_wx, lru_bx=lru_bx, lru_lambda=lru_lambda,
             gqa_q_norm=gqa_q_norm, gqa_k_norm=gqa_k_norm, w_out=w_out)
    lws = [_layer_weights(l, p) for l in range(DEPTH)]
    nb = x_prompt.shape[0]
    mod = _modulation(jnp.concatenate([c_prompt, c_sample], axis=0), ada_w.astype(BF16), ada_b)

    def mods_for(lo, hi):
        out = []
        for l in range(DEPTH):
            m = mod[l, lo:hi]
            out.append(tuple(m[:, D_MODEL * j:D_MODEL * (j + 1)][:, None, :] for j in range(3)))
        return out

    fn = final_norm[None]
    y_prompt = _trunk(x_prompt, mods_for(0, nb), lws, fn)
    y_sample = _trunk(x_sample, mods_for(nb, nb + x_sample.shape[0]), lws, fn)
    return (y_prompt, y_sample)
```

```python
import functools
import math

import jax
import jax.numpy as jnp
import numpy as np
from jax import lax
from jax.experimental import pallas as pl
from jax.experimental.pallas import tpu as pltpu

F32 = jnp.float32
BF16 = jnp.bfloat16

D_MODEL = 1024
DEPTH = 2
W_GROUP = 256
H_A, NOPE_A, ROPE_A, V_A = 4, 64, 32, 64
Q_LORA, KV_LORA = 192, 128
H_B, DK_B, DV_B = 4, 32, 64
H_C, BW_C, CONV_W = 4, 64, 4
LRU_C = 8.0
H_D, KV_H_D, HD_D = 4, 2, 64
GRID_W = 64
ROPE_THETA = 500000.0
MLA_ROPE_THETA = 10000.0
AXIAL_THETA = 10000.0
EPS = 1e-6
LOG2E = 1.4426950408889634

IN_SIZES = (Q_LORA, KV_LORA, ROPE_A, W_GROUP,
            H_B * 2 * DK_B, H_B * 2 * DK_B, H_B * DV_B, W_GROUP,
            W_GROUP, W_GROUP,
            H_D * HD_D, KV_H_D * HD_D, KV_H_D * HD_D, W_GROUP)
_OFF = np.concatenate([[0], np.cumsum(IN_SIZES)]).tolist()
(O_QA, O_KVA, O_KRA, O_GA, O_QB, O_KB, O_VB, O_GB, O_XC, O_GC, O_QD, O_KD, O_VD, O_GD, _) = _OFF

LANES = 128
SUBLANES = 8

TOK_TILE = 512
Q_TILE = 512
UNITS_PER_STEP = 64
LRU_CHUNK = 512
DENOM_LOG2_RANGE = 80
SUB_KEYS = 256
SCORE_LEAD = 2
VMEM_LIMIT = 56 * 1024 * 1024

ZS_KVL, ZS_KPE, ZS_KPE_SW, ZS_KB, ZS_KD, ZS_XC, ZS_G, ZS_END = (0, 128, 256, 384, 640, 768, 1024, 1280)
ZT_QA, ZT_KVL, ZT_QB, ZT_VB, ZT_QD, ZT_VD, ZT_G, ZT_END = (0, 192, 320, 576, 832, 1088, 1216, 1984)


def _partner(width, block, rot, half):
    idx = np.arange(width)
    d = idx % block
    out = idx.copy()
    in_rot = d < rot
    first = in_rot & ((d % (2 * half)) < half)
    second = in_rot & ~((d % (2 * half)) < half)
    out[first] = idx[first] + half
    out[second] = idx[second] - half
    return out


P_MLA = _partner(ROPE_A, ROPE_A, ROPE_A, ROPE_A // 2)


def _indicator(width, seg):
    m = np.zeros((width, LANES), np.float32)
    m[np.arange(width), np.arange(width) // seg] = 1.0
    return m


IND_MLA = _indicator(H_A * LANES, LANES)
IND_DIFF = _indicator(2 * H_B * DK_B, DK_B)
IND_GQA = _indicator(KV_H_D * HD_D, HD_D)


def _cparams(n_axes):
    return pltpu.CompilerParams(dimension_semantics=("arbitrary",) * n_axes,
                                vmem_limit_bytes=VMEM_LIMIT)


def _mod_kernel(c_ref, w_ref, b_ref, out_ref):
    c = c_ref[...]
    sc = (c * jax.nn.sigmoid(c)).astype(BF16)
    out_ref[0] = jnp.dot(sc, w_ref[0], preferred_element_type=F32) + b_ref[0]


def _modulation(c_all, ada_w_bf, ada_b):
    n = c_all.shape[0]
    return pl.pallas_call(
        _mod_kernel,
        out_shape=jax.ShapeDtypeStruct((DEPTH, n, 3 * D_MODEL), F32),
        grid=(DEPTH, 3),
        in_specs=[
            pl.BlockSpec((n, D_MODEL), lambda l, j: (0, 0)),
            pl.BlockSpec((1, D_MODEL, D_MODEL), lambda l, j: (l, 0, j)),
            pl.BlockSpec((1, 1, D_MODEL), lambda l, j: (l, 0, j)),
        ],
        out_specs=pl.BlockSpec((1, n, D_MODEL), lambda l, j: (l, 0, j)),
        compiler_params=_cparams(2),
        name="adaln_mod",
    )(c_all, ada_w_bf, ada_b.reshape(DEPTH, 1, 3 * D_MODEL))


def _pre_kernel(x_ref, *rest):
    _pre_body(x_ref[0], *rest)


def _post_pre_kernel(x_ref, oa_ref, ob_ref, oc_ref, od_ref, gt_in_ref, gc_in_ref, gate_ref, wout_ref, *rest):
    xn = _post_update(x_ref, (oa_ref, ob_ref, oc_ref, od_ref), gt_in_ref, gc_in_ref, gate_ref, wout_ref)
    rest[-1][0] = xn
    _pre_body(xn, *rest[:-1])


def _pre_body(x, shift_ref, scale_ref, ng_ref, wstd_ref, wt_ref,
              gqa_col_ref, wuqT_ref, gkv_row_ref, gkv_col_ref, wukp_ref, wuvT_ref,
              gq_col_ref, gk_row_ref,
              mkc_ref, mks_ref, mqc_ref, mqs_ref,
              dkc_ref, dks_ref, dqc_ref, dqs_ref,
              gkc_ref, gks_ref, gqc_ref, gqs_ref,
              ind_mla_ref, ind_diff_ref, ind_gqa_ref,
              qt_mla_ref, k_mla_ref, vt_mla_ref,
              qt_diff_ref, k_diff_ref, vt_diff_ref,
              qt_gqa_ref, k_gqa_ref, vt_gqa_ref,
              xc_ref, gc_ref, kn_mla_ref, kn_diff_ref, kn_gqa_ref, gt_ref):
    ts = x.shape[0]

    def key_norms(kb, ind_ref, out_ref):
        kf = kb.astype(F32)
        r = jnp.dot((kf * kf).astype(BF16), ind_ref[...], preferred_element_type=F32)
        out_ref[0, 0] = jnp.broadcast_to(jnp.max(r, axis=0, keepdims=True), (SUBLANES, LANES))

    ms = jnp.mean(x * x, axis=-1, keepdims=True)
    h = (x * lax.rsqrt(ms + EPS)) * ng_ref[...]
    h = h * (1.0 + scale_ref[0]) + shift_ref[0]
    hb = h.astype(BF16)
    z = jnp.dot(hb, wstd_ref[...], preferred_element_type=F32)
    zt = lax.dot_general(wt_ref[...], hb, (((1,), (1,)), ((), ())),
                         preferred_element_type=F32)

    qlt = zt[ZT_QA:ZT_QA + Q_LORA]
    msq = jnp.mean(qlt * qlt, axis=0, keepdims=True)
    qnb = ((qlt * lax.rsqrt(msq + EPS)) * gqa_col_ref[...]).astype(BF16)
    qat = jnp.dot(wuqT_ref[...], qnb, preferred_element_type=F32)
    sc_mla = (NOPE_A + ROPE_A) ** -0.5 * LOG2E
    dq = NOPE_A + ROPE_A
    pad = jnp.zeros((LANES - dq, ts), F32)
    for hh in range(H_A):
        nope = qat[dq * hh:dq * hh + NOPE_A]
        pe = qat[dq * hh + NOPE_A:dq * (hh + 1)]
        sw = jnp.concatenate([pe[ROPE_A // 2:], pe[:ROPE_A // 2]], axis=0)
        pe = pe * mqc_ref[...] + sw * mqs_ref[...]
        blk = jnp.concatenate([nope, pe, pad], axis=0) * sc_mla
        qt_mla_ref[0, LANES * hh:LANES * (hh + 1), :] = blk.astype(BF16)

    kvl = z[:, ZS_KVL:ZS_KVL + KV_LORA]
    msk = jnp.mean(kvl * kvl, axis=-1, keepdims=True)
    kvn = ((kvl * lax.rsqrt(msk + EPS)) * gkv_row_ref[...]).astype(BF16)
    knope = jnp.dot(kvn, wukp_ref[...], preferred_element_type=F32)
    kpe = z[:, ZS_KPE:ZS_KPE + LANES] * mkc_ref[...] + z[:, ZS_KPE_SW:ZS_KPE_SW + LANES] * mks_ref[...]
    kb = jnp.concatenate([(knope[:, LANES * hh:LANES * (hh + 1)] + kpe).astype(BF16) for hh in range(H_A)],
                         axis=1)
    k_mla_ref[0] = kb
    key_norms(kb, ind_mla_ref, kn_mla_ref)
    kvlt = zt[ZT_KVL:ZT_KVL + KV_LORA]
    mst = jnp.mean(kvlt * kvlt, axis=0, keepdims=True)
    kvnt = ((kvlt * lax.rsqrt(mst + EPS)) * gkv_col_ref[...]).astype(BF16)
    vt_mla_ref[0, 0] = jnp.dot(wuvT_ref[...], kvnt, preferred_element_type=F32).astype(BF16)

    kd = z[:, ZS_KB:ZS_KB + 256]
    lane_d = lax.broadcasted_iota(jnp.int32, (ts, 256), 1)
    half_d = DK_B // 8
    kd_sw = jnp.where(lane_d % (2 * half_d) < half_d, pltpu.roll(kd, 256 - half_d, 1), pltpu.roll(kd, half_d, 1))
    kb = (kd * dkc_ref[...] + kd_sw * dks_ref[...]).astype(BF16)
    k_diff_ref[0] = kb
    key_norms(kb, ind_diff_ref, kn_diff_ref)
    sc_diff = DK_B ** -0.5 * LOG2E
    rot = DK_B // 4
    for m in range(2 * H_B):
        q = zt[ZT_QB + DK_B * m:ZT_QB + DK_B * (m + 1)]
        q8 = q[0:rot]
        q8 = q8 * dqc_ref[0:rot, :] + pltpu.roll(q8, rot // 2, 0) * dqs_ref[0:rot, :]
        qq = jnp.concatenate([q8, q[rot:]], axis=0) * sc_diff
        qt_diff_ref[0, DK_B * m:DK_B * (m + 1), :] = qq.astype(BF16)
    vt_diff_ref[0, 0] = zt[ZT_VB:ZT_VB + 256].astype(BF16)

    kg = z[:, ZS_KD:ZS_KD + LANES]
    k2 = kg * kg
    lo = lax.broadcasted_iota(jnp.int32, (ts, LANES), 1) < HD_D
    s_lo = jnp.sum(jnp.where(lo, k2, 0.0), axis=-1, keepdims=True)
    s_hi = jnp.sum(jnp.where(lo, 0.0, k2), axis=-1, keepdims=True)
    rk = lax.rsqrt(jnp.where(lo, s_lo, s_hi) / HD_D + EPS)
    kn = (kg * rk) * gk_row_ref[...]
    half_g = HD_D // 4
    lane_g = lax.broadcasted_iota(jnp.int32, (ts, LANES), 1)
    knsw = jnp.where(lane_g % (2 * half_g) < half_g, pltpu.roll(kn, LANES - half_g, 1), pltpu.roll(kn, half_g, 1))
    kb = (kn * gkc_ref[...] + knsw * gks_ref[...]).astype(BF16)
    k_gqa_ref[0] = kb
    key_norms(kb, ind_gqa_ref, kn_gqa_ref)
    sc_gqa = HD_D ** -0.5 * LOG2E
    for hh in range(H_D):
        q = zt[ZT_QD + HD_D * hh:ZT_QD + HD_D * (hh + 1)]
        r = lax.rsqrt(jnp.mean(q * q, axis=0, keepdims=True) + EPS)
        qn = (q * r) * gq_col_ref[...]
        qsw = jnp.concatenate([qn[half_g:2 * half_g], qn[0:half_g], qn[3 * half_g:], qn[2 * half_g:3 * half_g]],
                              axis=0)
        qq = (qn * gqc_ref[...] + qsw * gqs_ref[...]) * sc_gqa
        qt_gqa_ref[0, HD_D * hh:HD_D * (hh + 1), :] = qq.astype(BF16)
    vt_gqa_ref[0, 0] = zt[ZT_VD:ZT_VD + 128].astype(BF16)

    xc_ref[0] = z[:, ZS_XC:ZS_XC + W_GROUP]
    g = z[:, ZS_G:ZS_END]
    gc_ref[0] = (g * jax.nn.sigmoid(g)).astype(BF16)
    g = zt[ZT_G:ZT_END]
    gt_ref[0] = (g * jax.nn.sigmoid(g)).astype(BF16)


def _pre_call(x, shift, scale, lw, tb, prev=None):
    b, s, d = x.shape
    ts = TOK_TILE
    nt = s // ts
    full2 = lambda a: pl.BlockSpec(a.shape, lambda i, bb: (0, 0))
    tok_rows = lambda w: pl.BlockSpec((ts, w), lambda i, bb: (i, 0))
    tok_cols = lambda r: pl.BlockSpec((r, ts), lambda i, bb: (0, i))
    tok3 = lambda w: pl.BlockSpec((1, ts, w), lambda i, bb: (bb, i, 0))
    vec3 = pl.BlockSpec((1, 1, d), lambda i, bb: (bb, 0, 0))
    params = [lw["norm_g"], lw["w_std"], lw["w_t"],
              lw["gqa_col"], lw["w_uqT"], lw["gkv_row"], lw["gkv_col"], lw["w_ukp"], lw["w_uvT"],
              lw["gq_col"], lw["gk_row"]]
    in_specs = [tok3(d)]
    head = [x]
    if prev is not None:
        feat3 = lambda r: pl.BlockSpec((1, r, ts), lambda i, bb: (bb, 0, i))
        in_specs += [feat3(W_GROUP), feat3(W_GROUP), tok3(W_GROUP), feat3(W_GROUP),
                     feat3(3 * W_GROUP), tok3(W_GROUP), vec3, full2(prev[7])]
        head += list(prev)
    in_specs += [vec3, vec3]
    in_specs += [full2(p) for p in params]
    in_specs += [tok_rows(128), tok_rows(128), tok_cols(32), tok_cols(32),
                 tok_rows(256), tok_rows(256), tok_cols(32), tok_cols(32),
                 tok_rows(128), tok_rows(128), tok_cols(64), tok_cols(64)]
    tables = [tb["mkc"], tb["mks"], tb["mqc"], tb["mqs"],
              tb["dkc"], tb["dks"], tb["dqc"], tb["dqs"],
              tb["gkc"], tb["gks"], tb["gqc"], tb["gqs"]]
    inds = [jnp.asarray(IND_MLA, BF16), jnp.asarray(IND_DIFF, BF16), jnp.asarray(IND_GQA, BF16)]
    in_specs += [full2(a) for a in inds]
    kn_shape = jax.ShapeDtypeStruct((b, nt, SUBLANES, LANES), F32)
    kn_spec = pl.BlockSpec((1, 1, SUBLANES, LANES), lambda i, bb: (bb, i, 0, 0))

    def qt_spec(rows):
        return pl.BlockSpec((1, rows, ts), lambda i, bb: (bb, 0, i))

    def k_spec(w):
        return pl.BlockSpec((1, ts, w), lambda i, bb: (bb, i, 0))

    def vt_spec(rows):
        return pl.BlockSpec((1, 1, rows, ts), lambda i, bb: (bb, i, 0, 0))

    out_shape = [
        jax.ShapeDtypeStruct((b, 512, s), BF16), jax.ShapeDtypeStruct((b, s, 512), BF16),
        jax.ShapeDtypeStruct((b, nt, 256, ts), BF16),
        jax.ShapeDtypeStruct((b, 256, s), BF16), jax.ShapeDtypeStruct((b, s, 256), BF16),
        jax.ShapeDtypeStruct((b, nt, 256, ts), BF16),
        jax.ShapeDtypeStruct((b, 256, s), BF16), jax.ShapeDtypeStruct((b, s, 128), BF16),
        jax.ShapeDtypeStruct((b, nt, 128, ts), BF16),
        jax.ShapeDtypeStruct((b, s, 256), F32), jax.ShapeDtypeStruct((b, s, 256), BF16),
        kn_shape, kn_shape, kn_shape, jax.ShapeDtypeStruct((b, 3 * W_GROUP, s), BF16),
    ]
    out_specs = [qt_spec(512), k_spec(512), vt_spec(256),
                 qt_spec(256), k_spec(256), vt_spec(256),
                 qt_spec(256), k_spec(128), vt_spec(128),
                 k_spec(256), k_spec(256), kn_spec, kn_spec, kn_spec, qt_spec(3 * W_GROUP)]
    if prev is not None:
        out_shape.append(jax.ShapeDtypeStruct((b, s, d), F32))
        out_specs.append(tok3(d))
    return pl.pallas_call(
        _pre_kernel if prev is None else _post_pre_kernel,
        out_shape=out_shape, grid=(nt, b), in_specs=in_specs, out_specs=out_specs,
        compiler_params=_cparams(2), name="pre_proj" if prev is None else "post_pre_proj",
    )(*head, shift, scale, *params, *tables, *inds)


def _attn_kernel(*refs, maps, n_chunks, tk, tq, kind, lam_init):
    if kind == "diff":
        qt_ref, k_ref, vt_ref, kn_ref, lam_ref, subln_ref, out_ref, w_scr = refs
    else:
        qt_ref, k_ref, vt_ref, kn_ref, out_ref, w_scr = refs
        lam_ref = subln_ref = None
    n_tiles = qt_ref.shape[2] // tq
    dv = 64
    nm = len(maps)
    finish = functools.partial(_attn_finish, out_ref=out_ref, lam_ref=lam_ref, subln_ref=subln_ref,
                               kind=kind, lam_init=lam_init, tq=tq)

    for t in range(n_tiles):
        for i, (kg, q0, qn, woff, v0, _) in enumerate(maps):
            q = qt_ref[0, q0:q0 + qn, tq * t:tq * (t + 1)]
            pieces = []
            if woff > 0:
                pieces.append(jnp.zeros((woff, tq), BF16))
            pieces.append(q)
            if LANES - woff - qn > 0:
                pieces.append(jnp.zeros((LANES - woff - qn, tq), BF16))
            w_scr[t, i] = pieces[0] if len(pieces) == 1 else jnp.concatenate(pieces, axis=0)

    def scores(t, i, c, r1=0, rows=tk):
        kg = maps[i][0]
        r0 = c * tk + r1 if isinstance(c, int) else pl.multiple_of(c * tk, tk)
        return jnp.dot(k_ref[0, pl.ds(r0, rows), LANES * kg:LANES * (kg + 1)], w_scr[t, i],
                       preferred_element_type=F32)

    def pv(i, c, p, r1=0, rows=tk):
        v0 = maps[i][4]
        return jnp.dot(vt_ref[0, c, v0:v0 + dv, r1:r1 + rows], p.astype(BF16),
                       preferred_element_type=F32)

    kn2 = jnp.max(kn_ref[0], axis=0)[0:1]
    lane = lax.broadcasted_iota(jnp.int32, (1, LANES), 1)
    kmax = [jnp.sqrt(jnp.max(jnp.where(lane == maps[i][5], kn2, 0.0), axis=-1, keepdims=True))
            for i in range(nm)]
    bounds = []
    for t in range(n_tiles):
        for i in range(nm):
            w = w_scr[t, i].astype(F32)
            bounds.append(jnp.sqrt(jnp.sum(w * w, axis=0, keepdims=True)) * kmax[i])
    units = [(t, i, c, r1) for t in range(n_tiles) for c in range(n_chunks)
             for r1 in range(0, tk, SUB_KEYS) for i in range(nm)]
    per_tile = len(units) // n_tiles
    accs, ls = [None] * (n_tiles * nm), [None] * (n_tiles * nm)
    pending = []
    lo, hi = 2.0 ** -DENOM_LOG2_RANGE, 2.0 ** DENOM_LOG2_RANGE
    for n in range(len(units) + SCORE_LEAD):
        if n < len(units):
            t, i, c, r1 = units[n]
            pending.append(scores(t, i, c, r1, SUB_KEYS))
        if n >= SCORE_LEAD:
            t, i, c, r1 = units[n - SCORE_LEAD]
            j = t * nm + i
            p = jnp.exp2(pending.pop(0) - bounds[j])
            part, psum = pv(i, c, p, r1, SUB_KEYS), jnp.sum(p, axis=0, keepdims=True)
            accs[j] = part if accs[j] is None else accs[j] + part
            ls[j] = psum if ls[j] is None else ls[j] + psum
            if (n - SCORE_LEAD + 1) % per_tile == 0:
                finish(t, accs[t * nm:(t + 1) * nm], ls[t * nm:(t + 1) * nm])

    for t in range(n_tiles):
        bad = None
        for l in ls[t * nm:(t + 1) * nm]:
            b = jnp.where((l > lo) & (l < hi), 0.0, 1.0)
            bad = b if bad is None else jnp.maximum(bad, b)

        @pl.when(jnp.max(bad) > 0.0)
        def _(t=t):
            def safe_body(c, carry):
                new = []
                for i in range(nm):
                    m, l, acc = carry[i]
                    s = scores(t, i, c)
                    mn = jnp.maximum(m, jnp.max(s, axis=0, keepdims=True))
                    p, alpha = jnp.exp2(s - mn), jnp.exp2(m - mn)
                    new.append((mn, alpha * l + jnp.sum(p, axis=0, keepdims=True), alpha * acc + pv(i, c, p)))
                return tuple(new)

            init = tuple((jnp.full((1, tq), -jnp.inf, F32), jnp.zeros((1, tq), F32), jnp.zeros((dv, tq), F32))
                         for _ in range(nm))
            fs = lax.fori_loop(0, n_chunks, safe_body, init)
            finish(t, [f[2] for f in fs], [f[1] for f in fs])


def _attn_finish(t, accs, ls, out_ref, lam_ref, subln_ref, kind, lam_init, tq):
    os_ = [acc / l for acc, l in zip(accs, ls)]

    if kind == "diff":
        lf = lam_ref[...]
        s1 = jnp.sum(lf[0:1] * lf[1:2], axis=-1, keepdims=True)
        s2 = jnp.sum(lf[2:3] * lf[3:4], axis=-1, keepdims=True)
        lam_full = jnp.exp(s1) - jnp.exp(s2) + lam_init
        heads = []
        for hh in range(H_B):
            o = os_[2 * hh] - lam_full * os_[2 * hh + 1]
            r = lax.rsqrt(jnp.mean(o * o, axis=0, keepdims=True) + EPS)
            heads.append(((o * r) * subln_ref[...]) * (1.0 - lam_init))
        ot = jnp.concatenate(heads, axis=0)
    else:
        ot = jnp.concatenate(os_, axis=0)
    out_ref[0, :, tq * t:tq * (t + 1)] = ot.astype(out_ref.dtype)


def _attn_call(qt, k, vt, kn, maps, kind, extra=(), lam_init=0.0, name="attn"):
    b, fq, s = qt.shape
    fk = k.shape[2]
    _, nt, fv, tk = vt.shape
    tq = Q_TILE
    n_tiles = max(1, min(s // tq, (UNITS_PER_STEP * tk) // (len(maps) * s)))
    tqs = tq * n_tiles
    in_specs = [pl.BlockSpec((1, fq, tqs), lambda bb, qi: (bb, 0, qi)),
                pl.BlockSpec((1, s, fk), lambda bb, qi: (bb, 0, 0)),
                pl.BlockSpec((1, nt, fv, tk), lambda bb, qi: (bb, 0, 0, 0)),
                pl.BlockSpec((1, nt, SUBLANES, LANES), lambda bb, qi: (bb, 0, 0, 0))]
    in_specs += [pl.BlockSpec(e.shape, lambda bb, qi: (0, 0)) for e in extra]
    kern = functools.partial(_attn_kernel, maps=tuple(maps), n_chunks=nt, tk=tk, tq=tq, kind=kind,
                             lam_init=lam_init)
    return pl.pallas_call(
        kern, out_shape=jax.ShapeDtypeStruct((b, W_GROUP, s), BF16),
        grid=(b, s // tqs), in_specs=in_specs,
        out_specs=pl.BlockSpec((1, W_GROUP, tqs), lambda bb, qi: (bb, 0, qi)),
        scratch_shapes=[pltpu.VMEM((n_tiles, len(maps), LANES, tq), BF16)],
        compiler_params=_cparams(2), name=name,
    )(qt, k, vt, kn, *extra)


MAPS_MLA = [(hh, LANES * hh, LANES, 0, V_A * hh, hh) for hh in range(H_A)]
MAPS_DIFF = [(m // 4, DK_B * m, DK_B, DK_B * (m % 4), DV_B * (m // 2), m) for m in range(2 * H_B)]
MAPS_GQA = [(0, HD_D * hh, HD_D, HD_D * (hh // (H_D // KV_H_D)), HD_D * (hh // (H_D // KV_H_D)),
             hh // (H_D // KV_H_D)) for hh in range(H_D)]


def _softplus(x):
    return jnp.maximum(x, 0.0) + jnp.log1p(jnp.exp(-jnp.abs(x)))


def _lru_kernel(xc_ref, cw_ref, cb_ref, wbd_ref, bias_ref, lam_ref, out_ref,
                xpad, af, gf, ar, gr, hf_scr, hr, *, s, tc):
    zero8 = jnp.zeros((SUBLANES, LANES), F32)
    xpad[0:SUBLANES, :] = zero8
    xpad[s + SUBLANES:s + 2 * SUBLANES, :] = zero8
    xpad[SUBLANES:s + SUBLANES, :] = xc_ref[0]
    lam = lam_ref[0]
    sp_f = _softplus(-lam[0:1])
    sp_r = _softplus(-lam[1:2])
    cw = cw_ref[...]
    cb = cb_ref[...]
    wbd = wbd_ref[0]
    bias = bias_ref[0]
    n = tc + 2 * SUBLANES

    def gates(pre_r, pre_i, sp, xconv):
        r = 0.5 * jnp.tanh(0.5 * pre_r) + 0.5
        i = 0.5 * jnp.tanh(0.5 * pre_i) + 0.5
        log_a = (-LRU_C * r) * sp
        a = jnp.exp(log_a)
        v = -jnp.tanh(log_a) * (a * a + 1.0)
        gx = jnp.where(v > 0.0, v * lax.rsqrt(v), 0.0) * (i * xconv)
        return a, gx

    def chunk(ci, carry):
        t0 = pl.multiple_of(ci * tc, tc)
        xw = xpad[pl.ds(t0, n), :]
        taps = [pltpu.roll(xw, 2, 0), pltpu.roll(xw, 1, 0), xw, pltpu.roll(xw, n - 1, 0)]
        xconv = cb + taps[0][SUBLANES:SUBLANES + tc] * cw[0:1]
        for j in range(1, CONV_W):
            xconv = xconv + taps[j][SUBLANES:SUBLANES + tc] * cw[j:j + 1]
        pre = jnp.dot(xconv.astype(BF16), wbd, preferred_element_type=F32) + bias
        a, gx = gates(pre[:, 0:128], pre[:, 128:256], sp_f, xconv)
        af[pl.ds(t0, tc), :] = a
        gf[pl.ds(t0, tc), :] = gx
        a, gx = gates(pre[:, 256:384], pre[:, 384:512], sp_r, xconv)
        ar[pl.ds(t0, tc), :] = a
        gr[pl.ds(t0, tc), :] = gx
        return carry

    lax.fori_loop(0, s // tc, chunk, 0)

    row = lax.broadcasted_iota(jnp.int32, (SUBLANES, LANES), 0)
    ng = s // SUBLANES

    def scan8(a, bv, reverse):
        for dd in (1, 2, 4):
            if reverse:
                ok = row < SUBLANES - dd
                sh = SUBLANES - dd
            else:
                ok = row >= dd
                sh = dd
            a_s = jnp.where(ok, pltpu.roll(a, sh, 0), 1.0)
            b_s = jnp.where(ok, pltpu.roll(bv, sh, 0), 0.0)
            bv = a * b_s + bv
            a = a * a_s
        return a, bv

    def scan_body(gi, carry):
        hf_prev, hr_prev = carry
        t0 = pl.multiple_of(gi * SUBLANES, SUBLANES)
        a, bv = scan8(af[pl.ds(t0, SUBLANES), :], gf[pl.ds(t0, SUBLANES), :], False)
        hf = bv + a * hf_prev
        hf_scr[pl.ds(t0, SUBLANES), :] = hf
        t1 = pl.multiple_of((ng - 1 - gi) * SUBLANES, SUBLANES)
        a, bv = scan8(ar[pl.ds(t1, SUBLANES), :], gr[pl.ds(t1, SUBLANES), :], True)
        hrv = bv + a * hr_prev
        hr[pl.ds(t1, SUBLANES), :] = hrv
        return (jnp.broadcast_to(hf[SUBLANES - 1:SUBLANES], (SUBLANES, LANES)),
                jnp.broadcast_to(hrv[0:1], (SUBLANES, LANES)))

    lax.fori_loop(0, ng, scan_body, (zero8, zero8), unroll=4)
    out_ref[0] = (hf_scr[...] + hr[...]).astype(out_ref.dtype)


def _lru_call(xc, lw):
    b, s, _ = xc.shape
    kern = functools.partial(_lru_kernel, s=s, tc=LRU_CHUNK)
    seq = pltpu.VMEM((s, LANES), F32)
    return pl.pallas_call(
        kern, out_shape=jax.ShapeDtypeStruct((b, s, W_GROUP), BF16),
        grid=(b, W_GROUP // LANES),
        in_specs=[pl.BlockSpec((1, s, LANES), lambda bb, hf: (bb, 0, hf)),
                  pl.BlockSpec((CONV_W, LANES), lambda bb, hf: (0, hf)),
                  pl.BlockSpec((1, LANES), lambda bb, hf: (0, hf)),
                  pl.BlockSpec((1, LANES, 4 * LANES), lambda bb, hf: (hf, 0, 0)),
                  pl.BlockSpec((1, 1, 4 * LANES), lambda bb, hf: (hf, 0, 0)),
                  pl.BlockSpec((1, 2, LANES), lambda bb, hf: (hf, 0, 0))],
        out_specs=pl.BlockSpec((1, s, LANES), lambda bb, hf: (bb, 0, hf)),
        scratch_shapes=[pltpu.VMEM((s + 2 * SUBLANES, LANES), F32), seq, seq, seq, seq, seq, seq],
        compiler_params=_cparams(2), name="rglru",
    )(xc, lw["conv_w"], lw["conv_b"], lw["lru_wbd"], lw["lru_bias"], lw["lru_lam"])


def _post_update(x_ref, o_refs, gt_ref, gc_ref, gate_ref, wout_ref):
    y = None
    row = 0
    for i, o_ref in enumerate(o_refs):
        w_blk = wout_ref[W_GROUP * i:W_GROUP * (i + 1), :]
        if i == 2:
            og = (o_ref[0].astype(F32) * gc_ref[0].astype(F32)).astype(BF16)
            part = jnp.dot(og, w_blk, preferred_element_type=F32)
        else:
            og = (o_ref[0].astype(F32) * gt_ref[0, row:row + W_GROUP, :].astype(F32)).astype(BF16)
            row += W_GROUP
            part = lax.dot_general(og, w_blk, (((0,), (0,)), ((), ())), preferred_element_type=F32)
        y = part if y is None else y + part
    return x_ref[0] + gate_ref[0] * y


def _post_kernel(x_ref, oa_ref, ob_ref, oc_ref, od_ref, gt_ref, gc_ref, gate_ref, wout_ref, fn_ref, out_ref):
    xn = _post_update(x_ref, (oa_ref, ob_ref, oc_ref, od_ref), gt_ref, gc_ref, gate_ref, wout_ref)
    ms = jnp.mean(xn * xn, axis=-1, keepdims=True)
    out_ref[0] = (xn * lax.rsqrt(ms + EPS)) * fn_ref[...]


def _post_call(x, oa, ob, oc, od, gt, gc, gate, w_out_bf, final_norm):
    b, s, d = x.shape
    ts = TOK_TILE
    tok = lambda w: pl.BlockSpec((1, ts, w), lambda bb, i: (bb, i, 0))
    feat = lambda r: pl.BlockSpec((1, r, ts), lambda bb, i: (bb, 0, i))
    return pl.pallas_call(
        _post_kernel,
        out_shape=jax.ShapeDtypeStruct((b, s, d), F32),
        grid=(b, s // ts),
        in_specs=[tok(d), feat(W_GROUP), feat(W_GROUP), tok(W_GROUP), feat(W_GROUP), feat(3 * W_GROUP), tok(W_GROUP),
                  pl.BlockSpec((1, 1, d), lambda bb, i: (bb, 0, 0)),
                  pl.BlockSpec((d, d), lambda bb, i: (0, 0)),
                  pl.BlockSpec((1, d), lambda bb, i: (0, 0))],
        out_specs=tok(d),
        compiler_params=_cparams(2), name="post_proj",
    )(x, oa, ob, oc, od, gt, gc, gate, w_out_bf, final_norm)


def _angles(pos, half, theta):
    inv = jnp.power(jnp.float32(theta), -jnp.arange(half, dtype=F32) / half)
    return pos[:, None] * inv[None, :]


def _tables(s):
    pos = jnp.arange(s, dtype=F32)
    rows = s // GRID_W
    row = jnp.repeat(jnp.arange(rows, dtype=F32), GRID_W)
    col = jnp.tile(jnp.arange(GRID_W, dtype=F32), rows)
    a = _angles(pos, ROPE_A // 2, MLA_ROPE_THETA)
    c, sn = jnp.cos(a), jnp.sin(a)
    z64, z32 = jnp.zeros((s, 64), F32), jnp.zeros((s, 32), F32)
    tb = {"mkc": jnp.concatenate([z64, c, c, z32], axis=1),
          "mks": jnp.concatenate([z64, -sn, sn, z32], axis=1),
          "mqc": jnp.concatenate([c, c], axis=1).T,
          "mqs": jnp.concatenate([-sn, sn], axis=1).T}
    a = _angles(pos, DK_B // 8, ROPE_THETA)
    c, sn = jnp.cos(a), jnp.sin(a)
    c32 = jnp.concatenate([c, c, jnp.ones((s, DK_B - DK_B // 4), F32)], axis=1)
    s32 = jnp.concatenate([-sn, sn, jnp.zeros((s, DK_B - DK_B // 4), F32)], axis=1)
    tb.update({"dkc": jnp.tile(c32, (1, 2 * H_B)), "dks": jnp.tile(s32, (1, 2 * H_B)),
               "dqc": c32.T, "dqs": s32.T})
    ar_ = _angles(row, HD_D // 4, AXIAL_THETA)
    ac_ = _angles(col, HD_D // 4, AXIAL_THETA)
    cr, sr, cc, sc = jnp.cos(ar_), jnp.sin(ar_), jnp.cos(ac_), jnp.sin(ac_)
    c64 = jnp.concatenate([cr, cr, cc, cc], axis=1)
    s64 = jnp.concatenate([-sr, sr, -sc, sc], axis=1)
    tb.update({"gkc": jnp.tile(c64, (1, KV_H_D)), "gks": jnp.tile(s64, (1, KV_H_D)),
               "gqc": c64.T, "gqs": s64.T})
    return tb


def _blockdiag_pair(w, half):
    z = jnp.zeros((BW_C, BW_C), F32)
    top = jnp.concatenate([w[2 * half], z], axis=1)
    bot = jnp.concatenate([z, w[2 * half + 1]], axis=1)
    return jnp.concatenate([top, bot], axis=0)


def _layer_weights(l, p):
    w = p["w_in"][l]
    col = lambda o, n: w[:, o:o + n]
    z64 = jnp.zeros((D_MODEL, 64), F32)
    z32 = jnp.zeros((D_MODEL, 32), F32)
    kra = col(O_KRA, ROPE_A)
    w_std = jnp.concatenate([
        col(O_KVA, KV_LORA),
        z64, kra, z32,
        z64, kra[:, P_MLA], z32,
        col(O_KB, 256), col(O_KD, 128),
        col(O_XC, W_GROUP), col(O_GC, W_GROUP)], axis=1)
    w_t = jnp.concatenate([
        col(O_QA, Q_LORA), col(O_KVA, KV_LORA),
        col(O_QB, 256), col(O_VB, 256),
        col(O_QD, 256), col(O_VD, 128),
        col(O_GA, W_GROUP), col(O_GB, W_GROUP), col(O_GD, W_GROUP)], axis=1).T
    wuq = p["mla_w_uq"][l]
    wukv = p["mla_w_ukv"][l]
    dkv = NOPE_A + V_A
    z_k = jnp.zeros((KV_LORA, LANES - NOPE_A), F32)
    w_ukp = jnp.concatenate(
        [jnp.concatenate([wukv[:, dkv * h:dkv * h + NOPE_A], z_k], axis=1) for h in range(H_A)], axis=1)
    w_uv = jnp.concatenate([wukv[:, dkv * h + NOPE_A:dkv * (h + 1)] for h in range(H_A)], axis=1)
    gq, gk = p["gqa_q_norm"][l], p["gqa_k_norm"][l]
    wa, wx = p["lru_wa"][l], p["lru_wx"][l]
    ba, bx = p["lru_ba"][l], p["lru_bx"][l]
    wbd = jnp.stack([jnp.concatenate([_blockdiag_pair(wa[0], hf), _blockdiag_pair(wx[0], hf),
                                      _blockdiag_pair(wa[1], hf), _blockdiag_pair(wx[1], hf)], axis=1)
                     for hf in range(2)])
    sl = lambda v, hf: v[LANES * hf:LANES * (hf + 1)]
    bias = jnp.stack([jnp.concatenate([sl(ba[0], hf), sl(bx[0], hf), sl(ba[1], hf), sl(bx[1], hf)])[None]
                      for hf in range(2)])
    lam = p["lru_lambda"][l]
    lam_h = jnp.stack([jnp.stack([sl(lam[0], hf), sl(lam[1], hf)]) for hf in range(2)])
    return {
        "norm_g": p["norm_g"][l][None], "w_std": w_std.astype(BF16), "w_t": w_t.astype(BF16),
        "gqa_col": p["mla_q_norm"][l][:, None], "w_uqT": wuq.T.astype(BF16),
        "gkv_row": p["mla_kv_norm"][l][None], "gkv_col": p["mla_kv_norm"][l][:, None],
        "w_ukp": w_ukp.astype(BF16), "w_uvT": w_uv.T.astype(BF16),
        "gq_col": gq[:, None], "gk_row": jnp.tile(gk, KV_H_D)[None],
        "diff_lambda": p["diff_lambda"][l], "subln_col": p["diff_subln"][l][:, None],
        "conv_w": p["lru_conv_w"][l], "conv_b": p["lru_conv_b"][l][None],
        "lru_wbd": wbd.astype(BF16), "lru_bias": bias, "lru_lam": lam_h,
        "w_out": p["w_out"][l].astype(BF16),
    }


def _trunk(x, mods, lws, final_norm):
    b, s, d = x.shape
    tb = _tables(s)
    prev = None
    for l in range(DEPTH):
        lw = lws[l]
        shift, scale, gate = mods[l]
        outs = _pre_call(x, shift, scale, lw, tb, prev)
        if prev is not None:
            x = outs[-1]
        (qt_mla, k_mla, vt_mla, qt_diff, k_diff, vt_diff, qt_gqa, k_gqa, vt_gqa, xc, gc,
         kn_mla, kn_diff, kn_gqa, gt) = outs[:15]
        lam_init = 0.8 - 0.6 * math.exp(-0.3 * l)
        oa = _attn_call(qt_mla, k_mla, vt_mla, kn_mla, MAPS_MLA, "mla", name="attn_mla")
        ob = _attn_call(qt_diff, k_diff, vt_diff, kn_diff, MAPS_DIFF, "diff",
                        extra=(lw["diff_lambda"], lw["subln_col"]), lam_init=lam_init, name="attn_diff")
        od = _attn_call(qt_gqa, k_gqa, vt_gqa, kn_gqa, MAPS_GQA, "gqa", name="attn_gqa")
        oc = _lru_call(xc, lw)
        prev = (oa, ob, oc, od, gt, gc, gate, lw["w_out"])
    return _post_call(x, *prev, final_norm)


def kernel(x_prompt, x_sample, c_prompt, c_sample, ada_w, ada_b, norm_g, w_in, mla_q_norm, mla_w_uq, mla_kv_norm, mla_w_ukv, diff_lambda, diff_subln, lru_conv_w, lru_conv_b, lru_wa, lru_ba, lru_wx, lru_bx, lru_lambda, gqa_q_norm, gqa_k_norm, w_out, final_norm):
    p = dict(norm_g=norm_g, w_in=w_in, mla_q_norm=mla_q_norm, mla_w_uq=mla_w_uq, mla_kv_norm=mla_kv_norm,
             mla_w_ukv=mla_w_ukv, diff_lambda=diff_lambda, diff_subln=diff_subln,
             lru_conv_w=lru_conv_w, lru_conv_b=lru_conv_b, lru_wa=lru_wa, lru_ba=lru_ba,
             lru_wx=lru_wx, lru_bx=lru_bx, lru_lambda=lru_lambda,
             gqa_q_norm=gqa_q_norm, gqa_k_norm=gqa_k_norm, w_out=w_out)
    lws = [_layer_weights(l, p) for l in range(DEPTH)]
    nb = x_prompt.shape[0]
    mod = _modulation(jnp.concatenate([c_prompt, c_sample], axis=0), ada_w.astype(BF16), ada_b)

    def mods_for(lo, hi):
        out = []
        for l in range(DEPTH):
            m = mod[l, lo:hi]
            out.append(tuple(m[:, D_MODEL * j:D_MODEL * (j + 1)][:, None, :] for j in range(3)))
        return out

    fn = final_norm[None]
    y_prompt = _trunk(x_prompt, mods_for(0, nb), lws, fn)
    y_sample = _trunk(x_sample, mods_for(nb, nb + x_sample.shape[0]), lws, fn)
    return (y_prompt, y_sample)
```

```python
import functools
import math

import jax
import jax.numpy as jnp
import numpy as np
from jax import lax
from jax.experimental import pallas as pl
from jax.experimental.pallas import tpu as pltpu

F32 = jnp.float32
BF16 = jnp.bfloat16

D_MODEL = 1024
DEPTH = 2
W_GROUP = 256
H_A, NOPE_A, ROPE_A, V_A = 4, 64, 32, 64
Q_LORA, KV_LORA = 192, 128
H_B, DK_B, DV_B = 4, 32, 64
H_C, BW_C, CONV_W = 4, 64, 4
CONV_PAD_L = CONV_W // 2
LRU_C = 8.0
H_D, KV_H_D, HD_D = 4, 2, 64
GRID_W = 64
ROPE_THETA = 500000.0
MLA_ROPE_THETA = 10000.0
AXIAL_THETA = 10000.0
EPS = 1e-6
LOG2E = 1.4426950408889634

IN_SIZES = (Q_LORA, KV_LORA, ROPE_A, W_GROUP,
            H_B * 2 * DK_B, H_B * 2 * DK_B, H_B * DV_B, W_GROUP,
            W_GROUP, W_GROUP,
            H_D * HD_D, KV_H_D * HD_D, KV_H_D * HD_D, W_GROUP)
_OFF = np.concatenate([[0], np.cumsum(IN_SIZES)]).tolist()
(O_QA, O_KVA, O_KRA, O_GA, O_QB, O_KB, O_VB, O_GB, O_XC, O_GC, O_QD, O_KD, O_VD, O_GD, _) = _OFF

LANES = 128
SUBLANES = 8

TOK_TILE = 512
Q_TILE = 512
UNITS_PER_STEP = 64
LRU_CHUNK = 512
DENOM_LOG2_RANGE = 80
SUB_KEYS = 256
SCORE_LEAD = 2
VMEM_LIMIT = 56 * 1024 * 1024

ZS_KVL, ZS_KPE, ZS_KPE_SW, ZS_KB, ZS_KD, ZS_XC, ZS_G, ZS_END = (0, 128, 256, 384, 640, 768, 1024, 1280)
ZT_QA, ZT_KVL, ZT_QB, ZT_VB, ZT_QD, ZT_VD, ZT_G, ZT_END = (0, 192, 320, 576, 832, 1088, 1216, 1984)


def _partner(width, block, rot, half):
    idx = np.arange(width)
    d = idx % block
    out = idx.copy()
    in_rot = d < rot
    first = in_rot & ((d % (2 * half)) < half)
    second = in_rot & ~((d % (2 * half)) < half)
    out[first] = idx[first] + half
    out[second] = idx[second] - half
    return out


P_MLA = _partner(ROPE_A, ROPE_A, ROPE_A, ROPE_A // 2)


def _indicator(width, seg):
    m = np.zeros((width, LANES), np.float32)
    m[np.arange(width), np.arange(width) // seg] = 1.0
    return m


IND_MLA = _indicator(H_A * LANES, LANES)
IND_DIFF = _indicator(2 * H_B * DK_B, DK_B)
IND_GQA = _indicator(KV_H_D * HD_D, HD_D)


def _cparams(n_axes):
    return pltpu.CompilerParams(dimension_semantics=("arbitrary",) * n_axes,
                                vmem_limit_bytes=VMEM_LIMIT)


def _mod_kernel(c_ref, w_ref, b_ref, out_ref):
    c = c_ref[...]
    sc = (c * jax.nn.sigmoid(c)).astype(BF16)
    out_ref[0] = jnp.dot(sc, w_ref[0], preferred_element_type=F32) + b_ref[0]


def _modulation(c_all, ada_w_bf, ada_b):
    n = c_all.shape[0]
    return pl.pallas_call(
        _mod_kernel,
        out_shape=jax.ShapeDtypeStruct((DEPTH, n, 3 * D_MODEL), F32),
        grid=(DEPTH, 3),
        in_specs=[
            pl.BlockSpec((n, D_MODEL), lambda l, j: (0, 0)),
            pl.BlockSpec((1, D_MODEL, D_MODEL), lambda l, j: (l, 0, j)),
            pl.BlockSpec((1, 1, D_MODEL), lambda l, j: (l, 0, j)),
        ],
        out_specs=pl.BlockSpec((1, n, D_MODEL), lambda l, j: (l, 0, j)),
        compiler_params=_cparams(2),
        name="adaln_mod",
    )(c_all, ada_w_bf, ada_b.reshape(DEPTH, 1, 3 * D_MODEL))


def _pre_kernel(x_ref, *rest):
    _pre_body(x_ref[0], *rest)


def _post_pre_kernel(x_ref, oa_ref, ob_ref, oc_ref, od_ref, gt_in_ref, gc_in_ref, gate_ref, wout_ref, *rest):
    xn = _post_update(x_ref, (oa_ref, ob_ref, oc_ref, od_ref), gt_in_ref, gc_in_ref, gate_ref, wout_ref)
    rest[-1][0] = xn
    _pre_body(xn, *rest[:-1])


def _pre_body(x, shift_ref, scale_ref, ng_ref, wstd_ref, wt_ref,
              gqa_col_ref, wuqT_ref, gkv_row_ref, gkv_col_ref, wukp_ref, wuvT_ref,
              gq_col_ref, gk_row_ref,
              mkc_ref, mks_ref, mqc_ref, mqs_ref,
              dkc_ref, dks_ref, dqc_ref, dqs_ref,
              gkc_ref, gks_ref, gqc_ref, gqs_ref,
              ind_mla_ref, ind_diff_ref, ind_gqa_ref,
              qt_mla_ref, k_mla_ref, vt_mla_ref,
              qt_diff_ref, k_diff_ref, vt_diff_ref,
              qt_gqa_ref, k_gqa_ref, vt_gqa_ref,
              xc_ref, gc_ref, kn_mla_ref, kn_diff_ref, kn_gqa_ref, gt_ref):
    ts = x.shape[0]

    def key_norms(kb, ind_ref, out_ref):
        kf = kb.astype(F32)
        r = jnp.dot((kf * kf).astype(BF16), ind_ref[...], preferred_element_type=F32)
        out_ref[0, 0] = jnp.broadcast_to(jnp.max(r, axis=0, keepdims=True), (SUBLANES, LANES))

    ms = jnp.mean(x * x, axis=-1, keepdims=True)
    h = (x * lax.rsqrt(ms + EPS)) * ng_ref[...]
    h = h * (1.0 + scale_ref[0]) + shift_ref[0]
    hb = h.astype(BF16)
    z = jnp.dot(hb, wstd_ref[...], preferred_element_type=F32)
    zt = lax.dot_general(wt_ref[...], hb, (((1,), (1,)), ((), ())),
                         preferred_element_type=F32)

    qlt = zt[ZT_QA:ZT_QA + Q_LORA]
    msq = jnp.mean(qlt * qlt, axis=0, keepdims=True)
    qnb = ((qlt * lax.rsqrt(msq + EPS)) * gqa_col_ref[...]).astype(BF16)
    qat = jnp.dot(wuqT_ref[...], qnb, preferred_element_type=F32)
    sc_mla = (NOPE_A + ROPE_A) ** -0.5 * LOG2E
    dq = NOPE_A + ROPE_A
    pad = jnp.zeros((LANES - dq, ts), F32)
    for hh in range(H_A):
        nope = qat[dq * hh:dq * hh + NOPE_A]
        pe = qat[dq * hh + NOPE_A:dq * (hh + 1)]
        sw = jnp.concatenate([pe[ROPE_A // 2:], pe[:ROPE_A // 2]], axis=0)
        pe = pe * mqc_ref[...] + sw * mqs_ref[...]
        blk = jnp.concatenate([nope, pe, pad], axis=0) * sc_mla
        qt_mla_ref[0, LANES * hh:LANES * (hh + 1), :] = blk.astype(BF16)

    kvl = z[:, ZS_KVL:ZS_KVL + KV_LORA]
    msk = jnp.mean(kvl * kvl, axis=-1, keepdims=True)
    kvn = ((kvl * lax.rsqrt(msk + EPS)) * gkv_row_ref[...]).astype(BF16)
    knope = jnp.dot(kvn, wukp_ref[...], preferred_element_type=F32)
    kpe = z[:, ZS_KPE:ZS_KPE + LANES] * mkc_ref[...] + z[:, ZS_KPE_SW:ZS_KPE_SW + LANES] * mks_ref[...]
    kb = jnp.concatenate([(knope[:, LANES * hh:LANES * (hh + 1)] + kpe).astype(BF16) for hh in range(H_A)],
                         axis=1)
    k_mla_ref[0] = kb
    key_norms(kb, ind_mla_ref, kn_mla_ref)
    kvlt = zt[ZT_KVL:ZT_KVL + KV_LORA]
    mst = jnp.mean(kvlt * kvlt, axis=0, keepdims=True)
    kvnt = ((kvlt * lax.rsqrt(mst + EPS)) * gkv_col_ref[...]).astype(BF16)
    vt_mla_ref[0, 0] = jnp.dot(wuvT_ref[...], kvnt, preferred_element_type=F32).astype(BF16)

    kd = z[:, ZS_KB:ZS_KB + 256]
    lane_d = lax.broadcasted_iota(jnp.int32, (ts, 256), 1)
    half_d = DK_B // 8
    kd_sw = jnp.where(lane_d % (2 * half_d) < half_d, pltpu.roll(kd, 256 - half_d, 1), pltpu.roll(kd, half_d, 1))
    kb = (kd * dkc_ref[...] + kd_sw * dks_ref[...]).astype(BF16)
    k_diff_ref[0] = kb
    key_norms(kb, ind_diff_ref, kn_diff_ref)
    sc_diff = DK_B ** -0.5 * LOG2E
    rot = DK_B // 4
    for m in range(2 * H_B):
        q = zt[ZT_QB + DK_B * m:ZT_QB + DK_B * (m + 1)]
        q8 = q[0:rot]
        q8 = q8 * dqc_ref[0:rot, :] + pltpu.roll(q8, rot // 2, 0) * dqs_ref[0:rot, :]
        qq = jnp.concatenate([q8, q[rot:]], axis=0) * sc_diff
        qt_diff_ref[0, DK_B * m:DK_B * (m + 1), :] = qq.astype(BF16)
    vt_diff_ref[0, 0] = zt[ZT_VB:ZT_VB + 256].astype(BF16)

    kg = z[:, ZS_KD:ZS_KD + LANES]
    k2 = kg * kg
    lo = lax.broadcasted_iota(jnp.int32, (ts, LANES), 1) < HD_D
    s_lo = jnp.sum(jnp.where(lo, k2, 0.0), axis=-1, keepdims=True)
    s_hi = jnp.sum(jnp.where(lo, 0.0, k2), axis=-1, keepdims=True)
    rk = lax.rsqrt(jnp.where(lo, s_lo, s_hi) / HD_D + EPS)
    kn = (kg * rk) * gk_row_ref[...]
    half_g = HD_D // 4
    lane_g = lax.broadcasted_iota(jnp.int32, (ts, LANES), 1)
    knsw = jnp.where(lane_g % (2 * half_g) < half_g, pltpu.roll(kn, LANES - half_g, 1), pltpu.roll(kn, half_g, 1))
    kb = (kn * gkc_ref[...] + knsw * gks_ref[...]).astype(BF16)
    k_gqa_ref[0] = kb
    key_norms(kb, ind_gqa_ref, kn_gqa_ref)
    sc_gqa = HD_D ** -0.5 * LOG2E
    for hh in range(H_D):
        q = zt[ZT_QD + HD_D * hh:ZT_QD + HD_D * (hh + 1)]
        r = lax.rsqrt(jnp.mean(q * q, axis=0, keepdims=True) + EPS)
        qn = (q * r) * gq_col_ref[...]
        qsw = jnp.concatenate([qn[half_g:2 * half_g], qn[0:half_g], qn[3 * half_g:], qn[2 * half_g:3 * half_g]],
                              axis=0)
        qq = (qn * gqc_ref[...] + qsw * gqs_ref[...]) * sc_gqa
        qt_gqa_ref[0, HD_D * hh:HD_D * (hh + 1), :] = qq.astype(BF16)
    vt_gqa_ref[0, 0] = zt[ZT_VD:ZT_VD + 128].astype(BF16)

    xc_ref[0] = z[:, ZS_XC:ZS_XC + W_GROUP]
    g = z[:, ZS_G:ZS_END]
    gc_ref[0] = (g * jax.nn.sigmoid(g)).astype(BF16)
    g = zt[ZT_G:ZT_END]
    gt_ref[0] = (g * jax.nn.sigmoid(g)).astype(BF16)


def _pre_call(x, shift, scale, lw, tb, prev=None):
    b, s, d = x.shape
    ts = TOK_TILE
    nt = s // ts
    full2 = lambda a: pl.BlockSpec(a.shape, lambda i, bb: (0, 0))
    tok_rows = lambda w: pl.BlockSpec((ts, w), lambda i, bb: (i, 0))
    tok_cols = lambda r: pl.BlockSpec((r, ts), lambda i, bb: (0, i))
    tok3 = lambda w: pl.BlockSpec((1, ts, w), lambda i, bb: (bb, i, 0))
    vec3 = pl.BlockSpec((1, 1, d), lambda i, bb: (bb, 0, 0))
    params = [lw["norm_g"], lw["w_std"], lw["w_t"],
              lw["gqa_col"], lw["w_uqT"], lw["gkv_row"], lw["gkv_col"], lw["w_ukp"], lw["w_uvT"],
              lw["gq_col"], lw["gk_row"]]
    in_specs = [tok3(d)]
    head = [x]
    if prev is not None:
        feat3 = lambda r: pl.BlockSpec((1, r, ts), lambda i, bb: (bb, 0, i))
        in_specs += [feat3(W_GROUP), feat3(W_GROUP), tok3(W_GROUP), feat3(W_GROUP),
                     feat3(3 * W_GROUP), tok3(W_GROUP), vec3, full2(prev[7])]
        head += list(prev)
    in_specs += [vec3, vec3]
    in_specs += [full2(p) for p in params]
    in_specs += [tok_rows(128), tok_rows(128), tok_cols(32), tok_cols(32),
                 tok_rows(256), tok_rows(256), tok_cols(32), tok_cols(32),
                 tok_rows(128), tok_rows(128), tok_cols(64), tok_cols(64)]
    tables = [tb["mkc"], tb["mks"], tb["mqc"], tb["mqs"],
              tb["dkc"], tb["dks"], tb["dqc"], tb["dqs"],
              tb["gkc"], tb["gks"], tb["gqc"], tb["gqs"]]
    inds = [jnp.asarray(IND_MLA, BF16), jnp.asarray(IND_DIFF, BF16), jnp.asarray(IND_GQA, BF16)]
    in_specs += [full2(a) for a in inds]
    kn_shape = jax.ShapeDtypeStruct((b, nt, SUBLANES, LANES), F32)
    kn_spec = pl.BlockSpec((1, 1, SUBLANES, LANES), lambda i, bb: (bb, i, 0, 0))

    def qt_spec(rows):
        return pl.BlockSpec((1, rows, ts), lambda i, bb: (bb, 0, i))

    def k_spec(w):
        return pl.BlockSpec((1, ts, w), lambda i, bb: (bb, i, 0))

    def vt_spec(rows):
        return pl.BlockSpec((1, 1, rows, ts), lambda i, bb: (bb, i, 0, 0))

    out_shape = [
        jax.ShapeDtypeStruct((b, 512, s), BF16), jax.ShapeDtypeStruct((b, s, 512), BF16),
        jax.ShapeDtypeStruct((b, nt, 256, ts), BF16),
        jax.ShapeDtypeStruct((b, 256, s), BF16), jax.ShapeDtypeStruct((b, s, 256), BF16),
        jax.ShapeDtypeStruct((b, nt, 256, ts), BF16),
        jax.ShapeDtypeStruct((b, 256, s), BF16), jax.ShapeDtypeStruct((b, s, 128), BF16),
        jax.ShapeDtypeStruct((b, nt, 128, ts), BF16),
        jax.ShapeDtypeStruct((b, s, 256), F32), jax.ShapeDtypeStruct((b, s, 256), BF16),
        kn_shape, kn_shape, kn_shape, jax.ShapeDtypeStruct((b, 3 * W_GROUP, s), BF16),
    ]
    out_specs = [qt_spec(512), k_spec(512), vt_spec(256),
                 qt_spec(256), k_spec(256), vt_spec(256),
                 qt_spec(256), k_spec(128), vt_spec(128),
                 k_spec(256), k_spec(256), kn_spec, kn_spec, kn_spec, qt_spec(3 * W_GROUP)]
    if prev is not None:
        out_shape.append(jax.ShapeDtypeStruct((b, s, d), F32))
        out_specs.append(tok3(d))
    return pl.pallas_call(
        _pre_kernel if prev is None else _post_pre_kernel,
        out_shape=out_shape, grid=(nt, b), in_specs=in_specs, out_specs=out_specs,
        compiler_params=_cparams(2), name="pre_proj" if prev is None else "post_pre_proj",
    )(*head, shift, scale, *params, *tables, *inds)


def _attn_kernel(*refs, maps, n_chunks, tk, tq, kind, lam_init):
    if kind == "diff":
        qt_ref, k_ref, vt_ref, kn_ref, lam_ref, subln_ref, out_ref, w_scr = refs
    else:
        qt_ref, k_ref, vt_ref, kn_ref, out_ref, w_scr = refs
        lam_ref = subln_ref = None
    n_tiles = qt_ref.shape[2] // tq
    dv = 64
    nm = len(maps)
    finish = functools.partial(_attn_finish, out_ref=out_ref, lam_ref=lam_ref, subln_ref=subln_ref,
                               kind=kind, lam_init=lam_init, tq=tq)

    for t in range(n_tiles):
        for i, (kg, q0, qn, woff, v0, _) in enumerate(maps):
            q = qt_ref[0, q0:q0 + qn, tq * t:tq * (t + 1)]
            pieces = []
            if woff > 0:
                pieces.append(jnp.zeros((woff, tq), BF16))
            pieces.append(q)
            if LANES - woff - qn > 0:
                pieces.append(jnp.zeros((LANES - woff - qn, tq), BF16))
            w_scr[t, i] = pieces[0] if len(pieces) == 1 else jnp.concatenate(pieces, axis=0)

    def scores(t, i, c, r1=0, rows=tk):
        kg = maps[i][0]
        r0 = c * tk + r1 if isinstance(c, int) else pl.multiple_of(c * tk, tk)
        return jnp.dot(k_ref[0, pl.ds(r0, rows), LANES * kg:LANES * (kg + 1)], w_scr[t, i],
                       preferred_element_type=F32)

    def pv(i, c, p, r1=0, rows=tk):
        v0 = maps[i][4]
        return jnp.dot(vt_ref[0, c, v0:v0 + dv, r1:r1 + rows], p.astype(BF16),
                       preferred_element_type=F32)

    kn2 = jnp.max(kn_ref[0], axis=0)[0:1]
    lane = lax.broadcasted_iota(jnp.int32, (1, LANES), 1)
    kmax = [jnp.sqrt(jnp.max(jnp.where(lane == maps[i][5], kn2, 0.0), axis=-1, keepdims=True))
            for i in range(nm)]
    bounds = []
    for t in range(n_tiles):
        for i in range(nm):
            w = w_scr[t, i].astype(F32)
            bounds.append(jnp.sqrt(jnp.sum(w * w, axis=0, keepdims=True)) * kmax[i])
    units = [(t, i, c, r1) for t in range(n_tiles) for c in range(n_chunks)
             for r1 in range(0, tk, SUB_KEYS) for i in range(nm)]
    per_tile = len(units) // n_tiles
    accs, ls = [None] * (n_tiles * nm), [None] * (n_tiles * nm)
    pending = []
    lo, hi = 2.0 ** -DENOM_LOG2_RANGE, 2.0 ** DENOM_LOG2_RANGE
    for n in range(len(units) + SCORE_LEAD):
        if n < len(units):
            t, i, c, r1 = units[n]
            pending.append(scores(t, i, c, r1, SUB_KEYS))
        if n >= SCORE_LEAD:
            t, i, c, r1 = units[n - SCORE_LEAD]
            j = t * nm + i
            p = jnp.exp2(pending.pop(0) - bounds[j])
            part, psum = pv(i, c, p, r1, SUB_KEYS), jnp.sum(p, axis=0, keepdims=True)
            accs[j] = part if accs[j] is None else accs[j] + part
            ls[j] = psum if ls[j] is None else ls[j] + psum
            if (n - SCORE_LEAD + 1) % per_tile == 0:
                finish(t, accs[t * nm:(t + 1) * nm], ls[t * nm:(t + 1) * nm])

    for t in range(n_tiles):
        bad = None
        for l in ls[t * nm:(t + 1) * nm]:
            b = jnp.where((l > lo) & (l < hi), 0.0, 1.0)
            bad = b if bad is None else jnp.maximum(bad, b)

        @pl.when(jnp.max(bad) > 0.0)
        def _(t=t):
            def safe_body(c, carry):
                new = []
                for i in range(nm):
                    m, l, acc = carry[i]
                    s = scores(t, i, c)
                    mn = jnp.maximum(m, jnp.max(s, axis=0, keepdims=True))
                    p, alpha = jnp.exp2(s - mn), jnp.exp2(m - mn)
                    new.append((mn, alpha * l + jnp.sum(p, axis=0, keepdims=True), alpha * acc + pv(i, c, p)))
                return tuple(new)

            init = tuple((jnp.full((1, tq), -jnp.inf, F32), jnp.zeros((1, tq), F32), jnp.zeros((dv, tq), F32))
                         for _ in range(nm))
            fs = lax.fori_loop(0, n_chunks, safe_body, init)
            finish(t, [f[2] for f in fs], [f[1] for f in fs])


def _attn_finish(t, accs, ls, out_ref, lam_ref, subln_ref, kind, lam_init, tq):
    os_ = [acc / l for acc, l in zip(accs, ls)]

    if kind == "diff":
        lf = lam_ref[...]
        s1 = jnp.sum(lf[0:1] * lf[1:2], axis=-1, keepdims=True)
        s2 = jnp.sum(lf[2:3] * lf[3:4], axis=-1, keepdims=True)
        lam_full = jnp.exp(s1) - jnp.exp(s2) + lam_init
        heads = []
        for hh in range(H_B):
            o = os_[2 * hh] - lam_full * os_[2 * hh + 1]
            r = lax.rsqrt(jnp.mean(o * o, axis=0, keepdims=True) + EPS)
            heads.append(((o * r) * subln_ref[...]) * (1.0 - lam_init))
        ot = jnp.concatenate(heads, axis=0)
    else:
        ot = jnp.concatenate(os_, axis=0)
    out_ref[0, :, tq * t:tq * (t + 1)] = ot.astype(out_ref.dtype)


def _attn_call(qt, k, vt, kn, maps, kind, extra=(), lam_init=0.0, name="attn"):
    b, fq, s = qt.shape
    fk = k.shape[2]
    _, nt, fv, tk = vt.shape
    tq = Q_TILE
    n_tiles = max(1, min(s // tq, (UNITS_PER_STEP * tk) // (len(maps) * s)))
    tqs = tq * n_tiles
    in_specs = [pl.BlockSpec((1, fq, tqs), lambda bb, qi: (bb, 0, qi)),
                pl.BlockSpec((1, s, fk), lambda bb, qi: (bb, 0, 0)),
                pl.BlockSpec((1, nt, fv, tk), lambda bb, qi: (bb, 0, 0, 0)),
                pl.BlockSpec((1, nt, SUBLANES, LANES), lambda bb, qi: (bb, 0, 0, 0))]
    in_specs += [pl.BlockSpec(e.shape, lambda bb, qi: (0, 0)) for e in extra]
    kern = functools.partial(_attn_kernel, maps=tuple(maps), n_chunks=nt, tk=tk, tq=tq, kind=kind,
                             lam_init=lam_init)
    return pl.pallas_call(
        kern, out_shape=jax.ShapeDtypeStruct((b, W_GROUP, s), BF16),
        grid=(b, s // tqs), in_specs=in_specs,
        out_specs=pl.BlockSpec((1, W_GROUP, tqs), lambda bb, qi: (bb, 0, qi)),
        scratch_shapes=[pltpu.VMEM((n_tiles, len(maps), LANES, tq), BF16)],
        compiler_params=_cparams(2), name=name,
    )(qt, k, vt, kn, *extra)


MAPS_MLA = [(hh, LANES * hh, LANES, 0, V_A * hh, hh) for hh in range(H_A)]
MAPS_DIFF = [(m // 4, DK_B * m, DK_B, DK_B * (m % 4), DV_B * (m // 2), m) for m in range(2 * H_B)]
MAPS_GQA = [(0, HD_D * hh, HD_D, HD_D * (hh // (H_D // KV_H_D)), HD_D * (hh // (H_D // KV_H_D)),
             hh // (H_D // KV_H_D)) for hh in range(H_D)]


def _softplus(x):
    return jnp.maximum(x, 0.0) + jnp.log1p(jnp.exp(-jnp.abs(x)))


def _lru_kernel(xc_ref, cw_ref, cb_ref, wbd_ref, bias_ref, lam_ref, out_ref,
                xpad, af, gf, ar, gr, hfl, pfl, hrl, prl, *, s, tc):
    tseg = s // SUBLANES
    tl_chunk = tc // SUBLANES
    xpad[0:tseg, :] = jnp.zeros((tseg, LANES), F32)
    xpad[tseg + s:2 * tseg + s, :] = jnp.zeros((tseg, LANES), F32)
    xpad[tseg:tseg + s, :] = xc_ref[0]
    lam = lam_ref[0]
    sp_f = _softplus(-lam[0:1])
    sp_r = _softplus(-lam[1:2])
    cw = cw_ref[...]
    cb = cb_ref[...]
    wbd = wbd_ref[0]
    bias = bias_ref[0]

    def gates(half_pre_r, half_pre_i, sp, half_xconv):
        log_a = (jnp.tanh(half_pre_r) + 1.0) * ((-0.5 * LRU_C) * sp)
        a = jnp.exp(log_a)
        v = -jnp.tanh(log_a) * (a * a + 1.0)
        gx = jnp.where(v > 0.0, v * lax.rsqrt(v), 0.0) * ((jnp.tanh(half_pre_i) + 1.0) * half_xconv)
        return a, gx

    def chunk(ci, carry):
        tl0 = ci * tl_chunk
        tiles = [xpad[pl.ds(tseg + tl0 + k, SUBLANES, stride=tseg), :]
                 for k in range(-CONV_PAD_L, tl_chunk + CONV_W - 1 - CONV_PAD_L)]
        xconv = cb
        for j in range(CONV_W):
            xconv = xconv + jnp.concatenate(tiles[j:j + tl_chunk], axis=0) * cw[j:j + 1]
        pre = jnp.dot(xconv.astype(BF16), wbd, preferred_element_type=F32) + bias
        half_xconv = 0.5 * xconv
        r0 = pl.multiple_of(ci * tc, tc)
        a, gx = gates(pre[:, 0:128], pre[:, 128:256], sp_f, half_xconv)
        af[pl.ds(r0, tc), :] = a
        gf[pl.ds(r0, tc), :] = gx
        a, gx = gates(pre[:, 256:384], pre[:, 384:512], sp_r, half_xconv)
        ar[pl.ds(r0, tc), :] = a
        gr[pl.ds(r0, tc), :] = gx
        return carry

    lax.fori_loop(0, s // tc, chunk, 0)

    def scan_body(t, carry):
        hf, pf, hr, pr = carry
        r0 = pl.multiple_of(t * SUBLANES, SUBLANES)
        a = af[pl.ds(r0, SUBLANES), :]
        hf = a * hf + gf[pl.ds(r0, SUBLANES), :]
        pf = a * pf
        hfl[pl.ds(r0, SUBLANES), :] = hf
        pfl[pl.ds(r0, SUBLANES), :] = pf
        r1 = pl.multiple_of((tseg - 1 - t) * SUBLANES, SUBLANES)
        a = ar[pl.ds(r1, SUBLANES), :]
        hr = a * hr + gr[pl.ds(r1, SUBLANES), :]
        pr = a * pr
        hrl[pl.ds(r1, SUBLANES), :] = hr
        prl[pl.ds(r1, SUBLANES), :] = pr
        return hf, pf, hr, pr

    zero8 = jnp.zeros((SUBLANES, LANES), F32)
    one8 = jnp.ones((SUBLANES, LANES), F32)
    h_end, p_end, h_start, p_start = lax.fori_loop(0, tseg, scan_body, (zero8, one8, zero8, one8), unroll=8)

    row = lax.broadcasted_iota(jnp.int32, (SUBLANES, LANES), 0)
    cf, cr = zero8, zero8
    for _ in range(SUBLANES - 1):
        cf = jnp.where(row >= 1, pltpu.roll(h_end + p_end * cf, 1, 0), 0.0)
        cr = jnp.where(row < SUBLANES - 1, pltpu.roll(h_start + p_start * cr, SUBLANES - 1, 0), 0.0)

    ho = af

    def patch(ci, carry):
        r0 = pl.multiple_of(ci * tc, tc)
        cft = jnp.concatenate([cf] * tl_chunk, axis=0)
        crt = jnp.concatenate([cr] * tl_chunk, axis=0)
        ho[pl.ds(r0, tc), :] = (hfl[pl.ds(r0, tc), :] + pfl[pl.ds(r0, tc), :] * cft) + (
            hrl[pl.ds(r0, tc), :] + prl[pl.ds(r0, tc), :] * crt)
        return carry

    lax.fori_loop(0, s // tc, patch, 0)

    def untangle(u, carry):
        for j in range(SUBLANES):
            base = u * (2 * SUBLANES * SUBLANES) + j
            blk = jnp.concatenate([ho[pl.ds(base, SUBLANES, stride=SUBLANES), :],
                                   ho[pl.ds(base + SUBLANES * SUBLANES, SUBLANES, stride=SUBLANES), :]], axis=0)
            out_ref[0, pl.ds(pl.multiple_of(j * tseg + u * 2 * SUBLANES, 2 * SUBLANES), 2 * SUBLANES), :] = (
                blk.astype(out_ref.dtype))
        return carry

    lax.fori_loop(0, tseg // (2 * SUBLANES), untangle, 0)


def _lru_call(xc, lw):
    b, s, _ = xc.shape
    kern = functools.partial(_lru_kernel, s=s, tc=LRU_CHUNK)
    seq = pltpu.VMEM((s, LANES), F32)
    return pl.pallas_call(
        kern, out_shape=jax.ShapeDtypeStruct((b, s, W_GROUP), BF16),
        grid=(b, W_GROUP // LANES),
        in_specs=[pl.BlockSpec((1, s, LANES), lambda bb, hf: (bb, 0, hf)),
                  pl.BlockSpec((CONV_W, LANES), lambda bb, hf: (0, hf)),
                  pl.BlockSpec((1, LANES), lambda bb, hf: (0, hf)),
                  pl.BlockSpec((1, LANES, 4 * LANES), lambda bb, hf: (hf, 0, 0)),
                  pl.BlockSpec((1, 1, 4 * LANES), lambda bb, hf: (hf, 0, 0)),
                  pl.BlockSpec((1, 2, LANES), lambda bb, hf: (hf, 0, 0))],
        out_specs=pl.BlockSpec((1, s, LANES), lambda bb, hf: (bb, 0, hf)),
        scratch_shapes=[pltpu.VMEM((s + 2 * (s // SUBLANES), LANES), F32)] + [seq] * 8,
        compiler_params=_cparams(2), name="rglru",
    )(xc, lw["conv_w"], lw["conv_b"], lw["lru_wbd"], lw["lru_bias"], lw["lru_lam"])


def _post_update(x_ref, o_refs, gt_ref, gc_ref, gate_ref, wout_ref):
    y = None
    row = 0
    for i, o_ref in enumerate(o_refs):
        w_blk = wout_ref[W_GROUP * i:W_GROUP * (i + 1), :]
        if i == 2:
            og = (o_ref[0].astype(F32) * gc_ref[0].astype(F32)).astype(BF16)
            part = jnp.dot(og, w_blk, preferred_element_type=F32)
        else:
            og = (o_ref[0].astype(F32) * gt_ref[0, row:row + W_GROUP, :].astype(F32)).astype(BF16)
            row += W_GROUP
            part = lax.dot_general(og, w_blk, (((0,), (0,)), ((), ())), preferred_element_type=F32)
        y = part if y is None else y + part
    return x_ref[0] + gate_ref[0] * y


def _post_kernel(x_ref, oa_ref, ob_ref, oc_ref, od_ref, gt_ref, gc_ref, gate_ref, wout_ref, fn_ref, out_ref):
    xn = _post_update(x_ref, (oa_ref, ob_ref, oc_ref, od_ref), gt_ref, gc_ref, gate_ref, wout_ref)
    ms = jnp.mean(xn * xn, axis=-1, keepdims=True)
    out_ref[0] = (xn * lax.rsqrt(ms + EPS)) * fn_ref[...]


def _post_call(x, oa, ob, oc, od, gt, gc, gate, w_out_bf, final_norm):
    b, s, d = x.shape
    ts = TOK_TILE
    tok = lambda w: pl.BlockSpec((1, ts, w), lambda bb, i: (bb, i, 0))
    feat = lambda r: pl.BlockSpec((1, r, ts), lambda bb, i: (bb, 0, i))
    return pl.pallas_call(
        _post_kernel,
        out_shape=jax.ShapeDtypeStruct((b, s, d), F32),
        grid=(b, s // ts),
        in_specs=[tok(d), feat(W_GROUP), feat(W_GROUP), tok(W_GROUP), feat(W_GROUP), feat(3 * W_GROUP), tok(W_GROUP),
                  pl.BlockSpec((1, 1, d), lambda bb, i: (bb, 0, 0)),
                  pl.BlockSpec((d, d), lambda bb, i: (0, 0)),
                  pl.BlockSpec((1, d), lambda bb, i: (0, 0))],
        out_specs=tok(d),
        compiler_params=_cparams(2), name="post_proj",
    )(x, oa, ob, oc, od, gt, gc, gate, w_out_bf, final_norm)


def _angles(pos, half, theta):
    inv = jnp.power(jnp.float32(theta), -jnp.arange(half, dtype=F32) / half)
    return pos[:, None] * inv[None, :]


def _tables(s):
    pos = jnp.arange(s, dtype=F32)
    rows = s // GRID_W
    row = jnp.repeat(jnp.arange(rows, dtype=F32), GRID_W)
    col = jnp.tile(jnp.arange(GRID_W, dtype=F32), rows)
    a = _angles(pos, ROPE_A // 2, MLA_ROPE_THETA)
    c, sn = jnp.cos(a), jnp.sin(a)
    z64, z32 = jnp.zeros((s, 64), F32), jnp.zeros((s, 32), F32)
    tb = {"mkc": jnp.concatenate([z64, c, c, z32], axis=1),
          "mks": jnp.concatenate([z64, -sn, sn, z32], axis=1),
          "mqc": jnp.concatenate([c, c], axis=1).T,
          "mqs": jnp.concatenate([-sn, sn], axis=1).T}
    a = _angles(pos, DK_B // 8, ROPE_THETA)
    c, sn = jnp.cos(a), jnp.sin(a)
    c32 = jnp.concatenate([c, c, jnp.ones((s, DK_B - DK_B // 4), F32)], axis=1)
    s32 = jnp.concatenate([-sn, sn, jnp.zeros((s, DK_B - DK_B // 4), F32)], axis=1)
    tb.update({"dkc": jnp.tile(c32, (1, 2 * H_B)), "dks": jnp.tile(s32, (1, 2 * H_B)),
               "dqc": c32.T, "dqs": s32.T})
    ar_ = _angles(row, HD_D // 4, AXIAL_THETA)
    ac_ = _angles(col, HD_D // 4, AXIAL_THETA)
    cr, sr, cc, sc = jnp.cos(ar_), jnp.sin(ar_), jnp.cos(ac_), jnp.sin(ac_)
    c64 = jnp.concatenate([cr, cr, cc, cc], axis=1)
    s64 = jnp.concatenate([-sr, sr, -sc, sc], axis=1)
    tb.update({"gkc": jnp.tile(c64, (1, KV_H_D)), "gks": jnp.tile(s64, (1, KV_H_D)),
               "gqc": c64.T, "gqs": s64.T})
    return tb


def _blockdiag_pair(w, half):
    z = jnp.zeros((BW_C, BW_C), F32)
    top = jnp.concatenate([w[2 * half], z], axis=1)
    bot = jnp.concatenate([z, w[2 * half + 1]], axis=1)
    return jnp.concatenate([top, bot], axis=0)


def _layer_weights(l, p):
    w = p["w_in"][l]
    col = lambda o, n: w[:, o:o + n]
    z64 = jnp.zeros((D_MODEL, 64), F32)
    z32 = jnp.zeros((D_MODEL, 32), F32)
    kra = col(O_KRA, ROPE_A)
    w_std = jnp.concatenate([
        col(O_KVA, KV_LORA),
        z64, kra, z32,
        z64, kra[:, P_MLA], z32,
        col(O_KB, 256), col(O_KD, 128),
        col(O_XC, W_GROUP), col(O_GC, W_GROUP)], axis=1)
    w_t = jnp.concatenate([
        col(O_QA, Q_LORA), col(O_KVA, KV_LORA),
        col(O_QB, 256), col(O_VB, 256),
        col(O_QD, 256), col(O_VD, 128),
        col(O_GA, W_GROUP), col(O_GB, W_GROUP), col(O_GD, W_GROUP)], axis=1).T
    wuq = p["mla_w_uq"][l]
    wukv = p["mla_w_ukv"][l]
    dkv = NOPE_A + V_A
    z_k = jnp.zeros((KV_LORA, LANES - NOPE_A), F32)
    w_ukp = jnp.concatenate(
        [jnp.concatenate([wukv[:, dkv * h:dkv * h + NOPE_A], z_k], axis=1) for h in range(H_A)], axis=1)
    w_uv = jnp.concatenate([wukv[:, dkv * h + NOPE_A:dkv * (h + 1)] for h in range(H_A)], axis=1)
    gq, gk = p["gqa_q_norm"][l], p["gqa_k_norm"][l]
    wa, wx = p["lru_wa"][l], p["lru_wx"][l]
    ba, bx = p["lru_ba"][l], p["lru_bx"][l]
    wbd = jnp.stack([jnp.concatenate([_blockdiag_pair(wa[0], hf), _blockdiag_pair(wx[0], hf),
                                      _blockdiag_pair(wa[1], hf), _blockdiag_pair(wx[1], hf)], axis=1)
                     for hf in range(2)])
    sl = lambda v, hf: v[LANES * hf:LANES * (hf + 1)]
    bias = jnp.stack([jnp.concatenate([sl(ba[0], hf), sl(bx[0], hf), sl(ba[1], hf), sl(bx[1], hf)])[None]
                      for hf in range(2)])
    lam = p["lru_lambda"][l]
    lam_h = jnp.stack([jnp.stack([sl(lam[0], hf), sl(lam[1], hf)]) for hf in range(2)])
    return {
        "norm_g": p["norm_g"][l][None], "w_std": w_std.astype(BF16), "w_t": w_t.astype(BF16),
        "gqa_col": p["mla_q_norm"][l][:, None], "w_uqT": wuq.T.astype(BF16),
        "gkv_row": p["mla_kv_norm"][l][None], "gkv_col": p["mla_kv_norm"][l][:, None],
        "w_ukp": w_ukp.astype(BF16), "w_uvT": w_uv.T.astype(BF16),
        "gq_col": gq[:, None], "gk_row": jnp.tile(gk, KV_H_D)[None],
        "diff_lambda": p["diff_lambda"][l], "subln_col": p["diff_subln"][l][:, None],
        "conv_w": p["lru_conv_w"][l], "conv_b": p["lru_conv_b"][l][None],
        "lru_wbd": (0.5 * wbd).astype(BF16), "lru_bias": 0.5 * bias, "lru_lam": lam_h,
        "w_out": p["w_out"][l].astype(BF16),
    }


def _trunk(x, mods, lws, final_norm):
    b, s, d = x.shape
    tb = _tables(s)
    prev = None
    for l in range(DEPTH):
        lw = lws[l]
        shift, scale, gate = mods[l]
        outs = _pre_call(x, shift, scale, lw, tb, prev)
        if prev is not None:
            x = outs[-1]
        (qt_mla, k_mla, vt_mla, qt_diff, k_diff, vt_diff, qt_gqa, k_gqa, vt_gqa, xc, gc,
         kn_mla, kn_diff, kn_gqa, gt) = outs[:15]
        lam_init = 0.8 - 0.6 * math.exp(-0.3 * l)
        oa = _attn_call(qt_mla, k_mla, vt_mla, kn_mla, MAPS_MLA, "mla", name="attn_mla")
        ob = _attn_call(qt_diff, k_diff, vt_diff, kn_diff, MAPS_DIFF, "diff",
                        extra=(lw["diff_lambda"], lw["subln_col"]), lam_init=lam_init, name="attn_diff")
        od = _attn_call(qt_gqa, k_gqa, vt_gqa, kn_gqa, MAPS_GQA, "gqa", name="attn_gqa")
        oc = _lru_call(xc, lw)
        prev = (oa, ob, oc, od, gt, gc, gate, lw["w_out"])
    return _post_call(x, *prev, final_norm)


def kernel(x_prompt, x_sample, c_prompt, c_sample, ada_w, ada_b, norm_g, w_in, mla_q_norm, mla_w_uq, mla_kv_norm, mla_w_ukv, diff_lambda, diff_subln, lru_conv_w, lru_conv_b, lru_wa, lru_ba, lru_wx, lru_bx, lru_lambda, gqa_q_norm, gqa_k_norm, w_out, final_norm):
    p = dict(norm_g=norm_g, w_in=w_in, mla_q_norm=mla_q_norm, mla_w_uq=mla_w_uq, mla_kv_norm=mla_kv_norm,
             mla_w_ukv=mla_w_ukv, diff_lambda=diff_lambda, diff_subln=diff_subln,
             lru_conv_w=lru_conv_w, lru_conv_b=lru_conv_b, lru_wa=lru_wa, lru_ba=lru_ba,
             lru_wx=lru_wx, lru_bx=lru_bx, lru_lambda=lru_lambda,
             gqa_q_norm=gqa_q_norm, gqa_k_norm=gqa_k_norm, w_out=w_out)
    lws = [_layer_weights(l, p) for l in range(DEPTH)]
    nb = x_prompt.shape[0]
    mod = _modulation(jnp.concatenate([c_prompt, c_sample], axis=0), ada_w.astype(BF16), ada_b)

    def mods_for(lo, hi):
        out = []
        for l in range(DEPTH):
            m = mod[l, lo:hi]
            out.append(tuple(m[:, D_MODEL * j:D_MODEL * (j + 1)][:, None, :] for j in range(3)))
        return out

    fn = final_norm[None]
    y_prompt = _trunk(x_prompt, mods_for(0, nb), lws, fn)
    y_sample = _trunk(x_sample, mods_for(nb, nb + x_sample.shape[0]), lws, fn)
    return (y_prompt, y_sample)
```

```python
import functools
import math

import jax
import jax.numpy as jnp
import numpy as np
from jax import lax
from jax.experimental import pallas as pl
from jax.experimental.pallas import tpu as pltpu

F32 = jnp.float32
BF16 = jnp.bfloat16

D_MODEL = 1024
DEPTH = 2
W_GROUP = 256
H_A, NOPE_A, ROPE_A, V_A = 4, 64, 32, 64
Q_LORA, KV_LORA = 192, 128
H_B, DK_B, DV_B = 4, 32, 64
H_C, BW_C, CONV_W = 4, 64, 4
CONV_PAD_L = CONV_W // 2
LRU_C = 8.0
H_D, KV_H_D, HD_D = 4, 2, 64
GRID_W = 64
ROPE_THETA = 500000.0
MLA_ROPE_THETA = 10000.0
AXIAL_THETA = 10000.0
EPS = 1e-6
LOG2E = 1.4426950408889634

IN_SIZES = (Q_LORA, KV_LORA, ROPE_A, W_GROUP,
            H_B * 2 * DK_B, H_B * 2 * DK_B, H_B * DV_B, W_GROUP,
            W_GROUP, W_GROUP,
            H_D * HD_D, KV_H_D * HD_D, KV_H_D * HD_D, W_GROUP)
_OFF = np.concatenate([[0], np.cumsum(IN_SIZES)]).tolist()
(O_QA, O_KVA, O_KRA, O_GA, O_QB, O_KB, O_VB, O_GB, O_XC, O_GC, O_QD, O_KD, O_VD, O_GD, _) = _OFF

LANES = 128
SUBLANES = 8

TOK_TILE = 512
SUB_TOK = 512
POST_TILE = 1024
Q_TILE = 512
UNITS_PER_STEP = 64
LRU_CHUNK = 512
DENOM_LOG2_RANGE = 80
SUB_KEYS = 256
SCORE_LEAD = 2
VMEM_LIMIT = 56 * 1024 * 1024

ZS_KVL, ZS_KPE, ZS_KPE_SW, ZS_KB, ZS_KD, ZS_XC, ZS_G, ZS_END = (0, 128, 256, 384, 640, 768, 1024, 1280)
ZT_QA, ZT_KVL, ZT_QB, ZT_VB, ZT_QD, ZT_VD, ZT_G, ZT_END = (0, 192, 320, 576, 832, 1088, 1216, 1984)


def _partner(width, block, rot, half):
    idx = np.arange(width)
    d = idx % block
    out = idx.copy()
    in_rot = d < rot
    first = in_rot & ((d % (2 * half)) < half)
    second = in_rot & ~((d % (2 * half)) < half)
    out[first] = idx[first] + half
    out[second] = idx[second] - half
    return out


P_MLA = _partner(ROPE_A, ROPE_A, ROPE_A, ROPE_A // 2)


def _indicator(width, seg):
    m = np.zeros((width, LANES), np.float32)
    m[np.arange(width), np.arange(width) // seg] = 1.0
    return m


IND_MLA = _indicator(H_A * LANES, LANES)
IND_DIFF = _indicator(2 * H_B * DK_B, DK_B)
IND_GQA = _indicator(KV_H_D * HD_D, HD_D)


def _cparams(n_axes):
    return pltpu.CompilerParams(dimension_semantics=("arbitrary",) * n_axes,
                                vmem_limit_bytes=VMEM_LIMIT)


def _mod_kernel(c_ref, w_ref, b_ref, out_ref):
    c = c_ref[...]
    sc = (c * jax.nn.sigmoid(c)).astype(BF16)
    out_ref[0] = jnp.dot(sc, w_ref[0], preferred_element_type=F32) + b_ref[0]


def _modulation(c_all, ada_w_bf, ada_b):
    n = c_all.shape[0]
    return pl.pallas_call(
        _mod_kernel,
        out_shape=jax.ShapeDtypeStruct((DEPTH, n, 3 * D_MODEL), F32),
        grid=(DEPTH, 3),
        in_specs=[
            pl.BlockSpec((n, D_MODEL), lambda l, j: (0, 0)),
            pl.BlockSpec((1, D_MODEL, D_MODEL), lambda l, j: (l, 0, j)),
            pl.BlockSpec((1, 1, D_MODEL), lambda l, j: (l, 0, j)),
        ],
        out_specs=pl.BlockSpec((1, n, D_MODEL), lambda l, j: (l, 0, j)),
        compiler_params=_cparams(2),
        name="adaln_mod",
    )(c_all, ada_w_bf, ada_b.reshape(DEPTH, 1, 3 * D_MODEL))


def _pre_kernel(x_ref, shift_ref, scale_ref, ng_ref, wstd_ref, wt_ref, *rest):
    subs = range(0, x_ref.shape[1], SUB_TOK)
    proj = [_pre_project(x_ref[0, r0:r0 + SUB_TOK, :], shift_ref, scale_ref, ng_ref, wstd_ref, wt_ref)
            for r0 in subs]
    for r0, (z, zt) in zip(subs, proj):
        _pre_finish(z, zt, *rest, r0=r0)


def _post_pre_kernel(x_ref, oa_ref, ob_ref, oc_ref, od_ref, gt_in_ref, gc_in_ref, gate_ref, wout_ref,
                     shift_ref, scale_ref, ng_ref, wstd_ref, wt_ref, *rest):
    subs = range(0, x_ref.shape[1], SUB_TOK)
    proj = []
    for r0 in subs:
        rs = slice(r0, r0 + SUB_TOK)
        xn = _post_update(x_ref, (oa_ref, ob_ref, oc_ref, od_ref), gt_in_ref, gc_in_ref, gate_ref, wout_ref, rs)
        rest[-1][0, rs, :] = xn
        proj.append(_pre_project(xn, shift_ref, scale_ref, ng_ref, wstd_ref, wt_ref))
    for r0, (z, zt) in zip(subs, proj):
        _pre_finish(z, zt, *rest[:-1], r0=r0)


def _pre_project(x, shift_ref, scale_ref, ng_ref, wstd_ref, wt_ref):
    ms = jnp.mean(x * x, axis=-1, keepdims=True)
    h = (x * lax.rsqrt(ms + EPS)) * ng_ref[...]
    h = h * (1.0 + scale_ref[0]) + shift_ref[0]
    hb = h.astype(BF16)
    z = jnp.dot(hb, wstd_ref[...], preferred_element_type=F32)
    zt = lax.dot_general(wt_ref[...], hb, (((1,), (1,)), ((), ())),
                         preferred_element_type=F32)
    return z, zt


def _pre_finish(z, zt,
              gqa_col_ref, wuqT_ref, gkv_row_ref, gkv_col_ref, wukp_ref, wuvT_ref,
              gq_col_ref, gk_row_ref,
              mkc_ref, mks_ref, mqc_ref, mqs_ref,
              dkc_ref, dks_ref, dqc_ref, dqs_ref,
              gkc_ref, gks_ref, gqc_ref, gqs_ref,
              ind_mla_ref, ind_diff_ref, ind_gqa_ref,
              qt_mla_ref, k_mla_ref, vt_mla_ref,
              qt_diff_ref, k_diff_ref, vt_diff_ref,
              qt_gqa_ref, k_gqa_ref, vt_gqa_ref,
              xc_ref, gc_ref, kn_mla_ref, kn_diff_ref, kn_gqa_ref, gt_ref, *, r0):
    ts = z.shape[0]
    rs = slice(r0, r0 + ts)

    def key_norms(kb, ind_ref, out_ref):
        kf = kb.astype(F32)
        r = jnp.dot((kf * kf).astype(BF16), ind_ref[...], preferred_element_type=F32)
        v = jnp.broadcast_to(jnp.max(r, axis=0, keepdims=True), (SUBLANES, LANES))
        out_ref[0, 0] = v if r0 == 0 else jnp.maximum(out_ref[0, 0], v)

    qlt = zt[ZT_QA:ZT_QA + Q_LORA]
    msq = jnp.mean(qlt * qlt, axis=0, keepdims=True)
    qnb = ((qlt * lax.rsqrt(msq + EPS)) * gqa_col_ref[...]).astype(BF16)
    qat = jnp.dot(wuqT_ref[...], qnb, preferred_element_type=F32)
    sc_mla = (NOPE_A + ROPE_A) ** -0.5 * LOG2E
    dq = NOPE_A + ROPE_A
    pad = jnp.zeros((LANES - dq, ts), F32)
    for hh in range(H_A):
        nope = qat[dq * hh:dq * hh + NOPE_A]
        pe = qat[dq * hh + NOPE_A:dq * (hh + 1)]
        sw = jnp.concatenate([pe[ROPE_A // 2:], pe[:ROPE_A // 2]], axis=0)
        pe = pe * mqc_ref[:, rs] + sw * mqs_ref[:, rs]
        blk = jnp.concatenate([nope, pe, pad], axis=0) * sc_mla
        qt_mla_ref[0, LANES * hh:LANES * (hh + 1), rs] = blk.astype(BF16)

    kvl = z[:, ZS_KVL:ZS_KVL + KV_LORA]
    msk = jnp.mean(kvl * kvl, axis=-1, keepdims=True)
    kvn = ((kvl * lax.rsqrt(msk + EPS)) * gkv_row_ref[...]).astype(BF16)
    knope = jnp.dot(kvn, wukp_ref[...], preferred_element_type=F32)
    kpe = z[:, ZS_KPE:ZS_KPE + LANES] * mkc_ref[rs, :] + z[:, ZS_KPE_SW:ZS_KPE_SW + LANES] * mks_ref[rs, :]
    kb = jnp.concatenate([(knope[:, LANES * hh:LANES * (hh + 1)] + kpe).astype(BF16) for hh in range(H_A)],
                         axis=1)
    k_mla_ref[0, rs, :] = kb
    key_norms(kb, ind_mla_ref, kn_mla_ref)
    kvlt = zt[ZT_KVL:ZT_KVL + KV_LORA]
    mst = jnp.mean(kvlt * kvlt, axis=0, keepdims=True)
    kvnt = ((kvlt * lax.rsqrt(mst + EPS)) * gkv_col_ref[...]).astype(BF16)
    vt_mla_ref[0, 0, :, rs] = jnp.dot(wuvT_ref[...], kvnt, preferred_element_type=F32).astype(BF16)

    kd = z[:, ZS_KB:ZS_KB + 256]
    lane_d = lax.broadcasted_iota(jnp.int32, (ts, 256), 1)
    half_d = DK_B // 8
    kd_sw = jnp.where(lane_d % (2 * half_d) < half_d, pltpu.roll(kd, 256 - half_d, 1), pltpu.roll(kd, half_d, 1))
    kb = (kd * dkc_ref[rs, :] + kd_sw * dks_ref[rs, :]).astype(BF16)
    k_diff_ref[0, rs, :] = kb
    key_norms(kb, ind_diff_ref, kn_diff_ref)
    sc_diff = DK_B ** -0.5 * LOG2E
    rot = DK_B // 4
    for m in range(2 * H_B):
        q = zt[ZT_QB + DK_B * m:ZT_QB + DK_B * (m + 1)]
        q8 = q[0:rot]
        q8 = q8 * dqc_ref[0:rot, rs] + pltpu.roll(q8, rot // 2, 0) * dqs_ref[0:rot, rs]
        qq = jnp.concatenate([q8, q[rot:]], axis=0) * sc_diff
        qt_diff_ref[0, DK_B * m:DK_B * (m + 1), rs] = qq.astype(BF16)
    vt_diff_ref[0, 0, :, rs] = zt[ZT_VB:ZT_VB + 256].astype(BF16)

    kg = z[:, ZS_KD:ZS_KD + LANES]
    k2 = kg * kg
    lo = lax.broadcasted_iota(jnp.int32, (ts, LANES), 1) < HD_D
    s_lo = jnp.sum(jnp.where(lo, k2, 0.0), axis=-1, keepdims=True)
    s_hi = jnp.sum(jnp.where(lo, 0.0, k2), axis=-1, keepdims=True)
    rk = lax.rsqrt(jnp.where(lo, s_lo, s_hi) / HD_D + EPS)
    kn = (kg * rk) * gk_row_ref[...]
    half_g = HD_D // 4
    lane_g = lax.broadcasted_iota(jnp.int32, (ts, LANES), 1)
    knsw = jnp.where(lane_g % (2 * half_g) < half_g, pltpu.roll(kn, LANES - half_g, 1), pltpu.roll(kn, half_g, 1))
    kb = (kn * gkc_ref[rs, :] + knsw * gks_ref[rs, :]).astype(BF16)
    k_gqa_ref[0, rs, :] = kb
    key_norms(kb, ind_gqa_ref, kn_gqa_ref)
    sc_gqa = HD_D ** -0.5 * LOG2E
    for hh in range(H_D):
        q = zt[ZT_QD + HD_D * hh:ZT_QD + HD_D * (hh + 1)]
        r = lax.rsqrt(jnp.mean(q * q, axis=0, keepdims=True) + EPS)
        qn = (q * r) * gq_col_ref[...]
        qsw = jnp.concatenate([qn[half_g:2 * half_g], qn[0:half_g], qn[3 * half_g:], qn[2 * half_g:3 * half_g]],
                              axis=0)
        qq = (qn * gqc_ref[:, rs] + qsw * gqs_ref[:, rs]) * sc_gqa
        qt_gqa_ref[0, HD_D * hh:HD_D * (hh + 1), rs] = qq.astype(BF16)
    vt_gqa_ref[0, 0, :, rs] = zt[ZT_VD:ZT_VD + 128].astype(BF16)

    xc_ref[0, rs, :] = z[:, ZS_XC:ZS_XC + W_GROUP]
    g = z[:, ZS_G:ZS_END]
    gc_ref[0, rs, :] = (g * jax.nn.sigmoid(g)).astype(BF16)
    g = zt[ZT_G:ZT_END]
    gt_ref[0, :, rs] = (g * jax.nn.sigmoid(g)).astype(BF16)


def _pre_call(x, shift, scale, lw, tb, prev=None):
    b, s, d = x.shape
    ts = TOK_TILE
    nt = s // ts
    full2 = lambda a: pl.BlockSpec(a.shape, lambda i, bb: (0, 0))
    tok_rows = lambda w: pl.BlockSpec((ts, w), lambda i, bb: (i, 0))
    tok_cols = lambda r: pl.BlockSpec((r, ts), lambda i, bb: (0, i))
    tok3 = lambda w: pl.BlockSpec((1, ts, w), lambda i, bb: (bb, i, 0))
    vec3 = pl.BlockSpec((1, 1, d), lambda i, bb: (bb, 0, 0))
    params = [lw["norm_g"], lw["w_std"], lw["w_t"],
              lw["gqa_col"], lw["w_uqT"], lw["gkv_row"], lw["gkv_col"], lw["w_ukp"], lw["w_uvT"],
              lw["gq_col"], lw["gk_row"]]
    in_specs = [tok3(d)]
    head = [x]
    if prev is not None:
        feat3 = lambda r: pl.BlockSpec((1, r, ts), lambda i, bb: (bb, 0, i))
        in_specs += [feat3(W_GROUP), feat3(W_GROUP), tok3(W_GROUP), feat3(W_GROUP),
                     feat3(3 * W_GROUP), tok3(W_GROUP), vec3, full2(prev[7])]
        head += list(prev)
    in_specs += [vec3, vec3]
    in_specs += [full2(p) for p in params]
    in_specs += [tok_rows(128), tok_rows(128), tok_cols(32), tok_cols(32),
                 tok_rows(256), tok_rows(256), tok_cols(32), tok_cols(32),
                 tok_rows(128), tok_rows(128), tok_cols(64), tok_cols(64)]
    tables = [tb["mkc"], tb["mks"], tb["mqc"], tb["mqs"],
              tb["dkc"], tb["dks"], tb["dqc"], tb["dqs"],
              tb["gkc"], tb["gks"], tb["gqc"], tb["gqs"]]
    inds = [jnp.asarray(IND_MLA, BF16), jnp.asarray(IND_DIFF, BF16), jnp.asarray(IND_GQA, BF16)]
    in_specs += [full2(a) for a in inds]
    kn_shape = jax.ShapeDtypeStruct((b, nt, SUBLANES, LANES), F32)
    kn_spec = pl.BlockSpec((1, 1, SUBLANES, LANES), lambda i, bb: (bb, i, 0, 0))

    def qt_spec(rows):
        return pl.BlockSpec((1, rows, ts), lambda i, bb: (bb, 0, i))

    def k_spec(w):
        return pl.BlockSpec((1, ts, w), lambda i, bb: (bb, i, 0))

    def vt_spec(rows):
        return pl.BlockSpec((1, 1, rows, ts), lambda i, bb: (bb, i, 0, 0))

    out_shape = [
        jax.ShapeDtypeStruct((b, 512, s), BF16), jax.ShapeDtypeStruct((b, s, 512), BF16),
        jax.ShapeDtypeStruct((b, nt, 256, ts), BF16),
        jax.ShapeDtypeStruct((b, 256, s), BF16), jax.ShapeDtypeStruct((b, s, 256), BF16),
        jax.ShapeDtypeStruct((b, nt, 256, ts), BF16),
        jax.ShapeDtypeStruct((b, 256, s), BF16), jax.ShapeDtypeStruct((b, s, 128), BF16),
        jax.ShapeDtypeStruct((b, nt, 128, ts), BF16),
        jax.ShapeDtypeStruct((b, s, 256), F32), jax.ShapeDtypeStruct((b, s, 256), BF16),
        kn_shape, kn_shape, kn_shape, jax.ShapeDtypeStruct((b, 3 * W_GROUP, s), BF16),
    ]
    out_specs = [qt_spec(512), k_spec(512), vt_spec(256),
                 qt_spec(256), k_spec(256), vt_spec(256),
                 qt_spec(256), k_spec(128), vt_spec(128),
                 k_spec(256), k_spec(256), kn_spec, kn_spec, kn_spec, qt_spec(3 * W_GROUP)]
    if prev is not None:
        out_shape.append(jax.ShapeDtypeStruct((b, s, d), F32))
        out_specs.append(tok3(d))
    return pl.pallas_call(
        _pre_kernel if prev is None else _post_pre_kernel,
        out_shape=out_shape, grid=(nt, b), in_specs=in_specs, out_specs=out_specs,
        compiler_params=_cparams(2), name="pre_proj" if prev is None else "post_pre_proj",
    )(*head, shift, scale, *params, *tables, *inds)


def _attn_kernel(*refs, maps, n_chunks, tk, tq, kind, lam_init):
    if kind == "diff":
        qt_ref, k_ref, vt_ref, kn_ref, lam_ref, subln_ref, out_ref, w_scr = refs
    else:
        qt_ref, k_ref, vt_ref, kn_ref, out_ref, w_scr = refs
        lam_ref = subln_ref = None
    n_tiles = qt_ref.shape[2] // tq
    dv = 64
    nm = len(maps)
    finish = functools.partial(_attn_finish, out_ref=out_ref, lam_ref=lam_ref, subln_ref=subln_ref,
                               kind=kind, lam_init=lam_init, tq=tq)

    for t in range(n_tiles):
        for i, (kg, q0, qn, woff, v0, _) in enumerate(maps):
            q = qt_ref[0, q0:q0 + qn, tq * t:tq * (t + 1)]
            pieces = []
            if woff > 0:
                pieces.append(jnp.zeros((woff, tq), BF16))
            pieces.append(q)
            if LANES - woff - qn > 0:
                pieces.append(jnp.zeros((LANES - woff - qn, tq), BF16))
            w_scr[t, i] = pieces[0] if len(pieces) == 1 else jnp.concatenate(pieces, axis=0)

    def scores(t, i, c, r1=0, rows=tk):
        kg = maps[i][0]
        r0 = c * tk + r1 if isinstance(c, int) else pl.multiple_of(c * tk, tk)
        return jnp.dot(k_ref[0, pl.ds(r0, rows), LANES * kg:LANES * (kg + 1)], w_scr[t, i],
                       preferred_element_type=F32)

    def pv(i, c, p, r1=0, rows=tk):
        v0 = maps[i][4]
        return jnp.dot(vt_ref[0, c, v0:v0 + dv, r1:r1 + rows], p.astype(BF16),
                       preferred_element_type=F32)

    kn2 = jnp.max(kn_ref[0], axis=0)[0:1]
    lane = lax.broadcasted_iota(jnp.int32, (1, LANES), 1)
    kmax = [jnp.sqrt(jnp.max(jnp.where(lane == maps[i][5], kn2, 0.0), axis=-1, keepdims=True))
            for i in range(nm)]
    bounds = []
    for t in range(n_tiles):
        for i in range(nm):
            w = w_scr[t, i].astype(F32)
            bounds.append(jnp.sqrt(jnp.sum(w * w, axis=0, keepdims=True)) * kmax[i])
    units = [(t, i, c, r1) for t in range(n_tiles) for c in range(n_chunks)
             for r1 in range(0, tk, SUB_KEYS) for i in range(nm)]
    per_tile = len(units) // n_tiles
    accs, ls = [None] * (n_tiles * nm), [None] * (n_tiles * nm)
    pending = []
    lo, hi = 2.0 ** -DENOM_LOG2_RANGE, 2.0 ** DENOM_LOG2_RANGE
    for n in range(len(units) + SCORE_LEAD):
        if n < len(units):
            t, i, c, r1 = units[n]
            pending.append(scores(t, i, c, r1, SUB_KEYS))
        if n >= SCORE_LEAD:
            t, i, c, r1 = units[n - SCORE_LEAD]
            j = t * nm + i
            p = jnp.exp2(pending.pop(0) - bounds[j])
            part, psum = pv(i, c, p, r1, SUB_KEYS), jnp.sum(p, axis=0, keepdims=True)
            accs[j] = part if accs[j] is None else accs[j] + part
            ls[j] = psum if ls[j] is None else ls[j] + psum
            if (n - SCORE_LEAD + 1) % per_tile == 0:
                finish(t, accs[t * nm:(t + 1) * nm], ls[t * nm:(t + 1) * nm])

    for t in range(n_tiles):
        bad = None
        for l in ls[t * nm:(t + 1) * nm]:
            b = jnp.where((l > lo) & (l < hi), 0.0, 1.0)
            bad = b if bad is None else jnp.maximum(bad, b)

        @pl.when(jnp.max(bad) > 0.0)
        def _(t=t):
            def safe_body(c, carry):
                new = []
                for i in range(nm):
                    m, l, acc = carry[i]
                    s = scores(t, i, c)
                    mn = jnp.maximum(m, jnp.max(s, axis=0, keepdims=True))
                    p, alpha = jnp.exp2(s - mn), jnp.exp2(m - mn)
                    new.append((mn, alpha * l + jnp.sum(p, axis=0, keepdims=True), alpha * acc + pv(i, c, p)))
                return tuple(new)

            init = tuple((jnp.full((1, tq), -jnp.inf, F32), jnp.zeros((1, tq), F32), jnp.zeros((dv, tq), F32))
                         for _ in range(nm))
            fs = lax.fori_loop(0, n_chunks, safe_body, init)
            finish(t, [f[2] for f in fs], [f[1] for f in fs])


def _attn_finish(t, accs, ls, out_ref, lam_ref, subln_ref, kind, lam_init, tq):
    os_ = [acc / l for acc, l in zip(accs, ls)]

    if kind == "diff":
        lf = lam_ref[...]
        s1 = jnp.sum(lf[0:1] * lf[1:2], axis=-1, keepdims=True)
        s2 = jnp.sum(lf[2:3] * lf[3:4], axis=-1, keepdims=True)
        lam_full = jnp.exp(s1) - jnp.exp(s2) + lam_init
        heads = []
        for hh in range(H_B):
            o = os_[2 * hh] - lam_full * os_[2 * hh + 1]
            r = lax.rsqrt(jnp.mean(o * o, axis=0, keepdims=True) + EPS)
            heads.append(((o * r) * subln_ref[...]) * (1.0 - lam_init))
        ot = jnp.concatenate(heads, axis=0)
    else:
        ot = jnp.concatenate(os_, axis=0)
    out_ref[0, :, tq * t:tq * (t + 1)] = ot.astype(out_ref.dtype)


def _attn_call(qt, k, vt, kn, maps, kind, extra=(), lam_init=0.0, name="attn"):
    b, fq, s = qt.shape
    fk = k.shape[2]
    _, nt, fv, tk = vt.shape
    tq = Q_TILE
    n_tiles = max(1, min(s // tq, (UNITS_PER_STEP * tk) // (len(maps) * s)))
    tqs = tq * n_tiles
    in_specs = [pl.BlockSpec((1, fq, tqs), lambda bb, qi: (bb, 0, qi)),
                pl.BlockSpec((1, s, fk), lambda bb, qi: (bb, 0, 0)),
                pl.BlockSpec((1, nt, fv, tk), lambda bb, qi: (bb, 0, 0, 0)),
                pl.BlockSpec((1, nt, SUBLANES, LANES), lambda bb, qi: (bb, 0, 0, 0))]
    in_specs += [pl.BlockSpec(e.shape, lambda bb, qi: (0, 0)) for e in extra]
    kern = functools.partial(_attn_kernel, maps=tuple(maps), n_chunks=nt, tk=tk, tq=tq, kind=kind,
                             lam_init=lam_init)
    return pl.pallas_call(
        kern, out_shape=jax.ShapeDtypeStruct((b, W_GROUP, s), BF16),
        grid=(b, s // tqs), in_specs=in_specs,
        out_specs=pl.BlockSpec((1, W_GROUP, tqs), lambda bb, qi: (bb, 0, qi)),
        scratch_shapes=[pltpu.VMEM((n_tiles, len(maps), LANES, tq), BF16)],
        compiler_params=_cparams(2), name=name,
    )(qt, k, vt, kn, *extra)


MAPS_MLA = [(hh, LANES * hh, LANES, 0, V_A * hh, hh) for hh in range(H_A)]
MAPS_DIFF = [(m // 4, DK_B * m, DK_B, DK_B * (m % 4), DV_B * (m // 2), m) for m in range(2 * H_B)]
MAPS_GQA = [(0, HD_D * hh, HD_D, HD_D * (hh // (H_D // KV_H_D)), HD_D * (hh // (H_D // KV_H_D)),
             hh // (H_D // KV_H_D)) for hh in range(H_D)]


def _softplus(x):
    return jnp.maximum(x, 0.0) + jnp.log1p(jnp.exp(-jnp.abs(x)))


def _lru_kernel(xc_ref, cw_ref, cb_ref, wbd_ref, bias_ref, lam_ref, out_ref,
                xpad, af, gf, ar, gr, hfl, pfl, hrl, prl, *, s, tc):
    tseg = s // SUBLANES
    tl_chunk = tc // SUBLANES
    xpad[0:tseg, :] = jnp.zeros((tseg, LANES), F32)
    xpad[tseg + s:2 * tseg + s, :] = jnp.zeros((tseg, LANES), F32)
    xpad[tseg:tseg + s, :] = xc_ref[0]
    lam = lam_ref[0]
    sp_f = _softplus(-lam[0:1])
    sp_r = _softplus(-lam[1:2])
    cw = cw_ref[...]
    cb = cb_ref[...]
    wbd = wbd_ref[0]
    bias = bias_ref[0]

    def gates(half_pre_r, half_pre_i, sp, half_xconv):
        log_a = (jnp.tanh(half_pre_r) + 1.0) * ((-0.5 * LRU_C) * sp)
        a = jnp.exp(log_a)
        v = -jnp.tanh(log_a) * (a * a + 1.0)
        gx = jnp.where(v > 0.0, v * lax.rsqrt(v), 0.0) * ((jnp.tanh(half_pre_i) + 1.0) * half_xconv)
        return a, gx

    def chunk(ci, carry):
        tl0 = ci * tl_chunk
        tiles = [xpad[pl.ds(tseg + tl0 + k, SUBLANES, stride=tseg), :]
                 for k in range(-CONV_PAD_L, tl_chunk + CONV_W - 1 - CONV_PAD_L)]
        xconv = cb
        for j in range(CONV_W):
            xconv = xconv + jnp.concatenate(tiles[j:j + tl_chunk], axis=0) * cw[j:j + 1]
        pre = jnp.dot(xconv.astype(BF16), wbd, preferred_element_type=F32) + bias
        half_xconv = 0.5 * xconv
        r0 = pl.multiple_of(ci * tc, tc)
        a, gx = gates(pre[:, 0:128], pre[:, 128:256], sp_f, half_xconv)
        af[pl.ds(r0, tc), :] = a
        gf[pl.ds(r0, tc), :] = gx
        a, gx = gates(pre[:, 256:384], pre[:, 384:512], sp_r, half_xconv)
        ar[pl.ds(r0, tc), :] = a
        gr[pl.ds(r0, tc), :] = gx
        return carry

    lax.fori_loop(0, s // tc, chunk, 0)

    def scan_body(t, carry):
        hf, pf, hr, pr = carry
        r0 = pl.multiple_of(t * SUBLANES, SUBLANES)
        a = af[pl.ds(r0, SUBLANES), :]
        hf = a * hf + gf[pl.ds(r0, SUBLANES), :]
        pf = a * pf
        hfl[pl.ds(r0, SUBLANES), :] = hf
        pfl[pl.ds(r0, SUBLANES), :] = pf
        r1 = pl.multiple_of((tseg - 1 - t) * SUBLANES, SUBLANES)
        a = ar[pl.ds(r1, SUBLANES), :]
        hr = a * hr + gr[pl.ds(r1, SUBLANES), :]
        pr = a * pr
        hrl[pl.ds(r1, SUBLANES), :] = hr
        prl[pl.ds(r1, SUBLANES), :] = pr
        return hf, pf, hr, pr

    zero8 = jnp.zeros((SUBLANES, LANES), F32)
    one8 = jnp.ones((SUBLANES, LANES), F32)
    h_end, p_end, h_start, p_start = lax.fori_loop(0, tseg, scan_body, (zero8, one8, zero8, one8), unroll=8)

    row = lax.broadcasted_iota(jnp.int32, (SUBLANES, LANES), 0)
    cf, cr = zero8, zero8
    for _ in range(SUBLANES - 1):
        cf = jnp.where(row >= 1, pltpu.roll(h_end + p_end * cf, 1, 0), 0.0)
        cr = jnp.where(row < SUBLANES - 1, pltpu.roll(h_start + p_start * cr, SUBLANES - 1, 0), 0.0)

    ho = af

    def patch(ci, carry):
        r0 = pl.multiple_of(ci * tc, tc)
        cft = jnp.concatenate([cf] * tl_chunk, axis=0)
        crt = jnp.concatenate([cr] * tl_chunk, axis=0)
        ho[pl.ds(r0, tc), :] = (hfl[pl.ds(r0, tc), :] + pfl[pl.ds(r0, tc), :] * cft) + (
            hrl[pl.ds(r0, tc), :] + prl[pl.ds(r0, tc), :] * crt)
        return carry

    lax.fori_loop(0, s // tc, patch, 0)

    def untangle(u, carry):
        for j in range(SUBLANES):
            base = u * (2 * SUBLANES * SUBLANES) + j
            blk = jnp.concatenate([ho[pl.ds(base, SUBLANES, stride=SUBLANES), :],
                                   ho[pl.ds(base + SUBLANES * SUBLANES, SUBLANES, stride=SUBLANES), :]], axis=0)
            out_ref[0, pl.ds(pl.multiple_of(j * tseg + u * 2 * SUBLANES, 2 * SUBLANES), 2 * SUBLANES), :] = (
                blk.astype(out_ref.dtype))
        return carry

    lax.fori_loop(0, tseg // (2 * SUBLANES), untangle, 0)


def _lru_call(xc, lw):
    b, s, _ = xc.shape
    kern = functools.partial(_lru_kernel, s=s, tc=LRU_CHUNK)
    seq = pltpu.VMEM((s, LANES), F32)
    return pl.pallas_call(
        kern, out_shape=jax.ShapeDtypeStruct((b, s, W_GROUP), BF16),
        grid=(b, W_GROUP // LANES),
        in_specs=[pl.BlockSpec((1, s, LANES), lambda bb, hf: (bb, 0, hf)),
                  pl.BlockSpec((CONV_W, LANES), lambda bb, hf: (0, hf)),
                  pl.BlockSpec((1, LANES), lambda bb, hf: (0, hf)),
                  pl.BlockSpec((1, LANES, 4 * LANES), lambda bb, hf: (hf, 0, 0)),
                  pl.BlockSpec((1, 1, 4 * LANES), lambda bb, hf: (hf, 0, 0)),
                  pl.BlockSpec((1, 2, LANES), lambda bb, hf: (hf, 0, 0))],
        out_specs=pl.BlockSpec((1, s, LANES), lambda bb, hf: (bb, 0, hf)),
        scratch_shapes=[pltpu.VMEM((s + 2 * (s // SUBLANES), LANES), F32)] + [seq] * 8,
        compiler_params=_cparams(2), name="rglru",
    )(xc, lw["conv_w"], lw["conv_b"], lw["lru_wbd"], lw["lru_bias"], lw["lru_lam"])


def _post_update(x_ref, o_refs, gt_ref, gc_ref, gate_ref, wout_ref, rs=slice(None)):
    y = None
    row = 0
    for i, o_ref in enumerate(o_refs):
        w_blk = wout_ref[W_GROUP * i:W_GROUP * (i + 1), :]
        if i == 2:
            og = (o_ref[0, rs, :].astype(F32) * gc_ref[0, rs, :].astype(F32)).astype(BF16)
            part = jnp.dot(og, w_blk, preferred_element_type=F32)
        else:
            og = (o_ref[0, :, rs].astype(F32) * gt_ref[0, row:row + W_GROUP, rs].astype(F32)).astype(BF16)
            row += W_GROUP
            part = lax.dot_general(og, w_blk, (((0,), (0,)), ((), ())), preferred_element_type=F32)
        y = part if y is None else y + part
    return x_ref[0, rs, :] + gate_ref[0] * y


def _post_kernel(x_ref, oa_ref, ob_ref, oc_ref, od_ref, gt_ref, gc_ref, gate_ref, wout_ref, fn_ref, out_ref):
    xn = _post_update(x_ref, (oa_ref, ob_ref, oc_ref, od_ref), gt_ref, gc_ref, gate_ref, wout_ref)
    ms = jnp.mean(xn * xn, axis=-1, keepdims=True)
    out_ref[0] = (xn * lax.rsqrt(ms + EPS)) * fn_ref[...]


def _post_call(x, oa, ob, oc, od, gt, gc, gate, w_out_bf, final_norm):
    b, s, d = x.shape
    ts = POST_TILE
    tok = lambda w: pl.BlockSpec((1, ts, w), lambda bb, i: (bb, i, 0))
    feat = lambda r: pl.BlockSpec((1, r, ts), lambda bb, i: (bb, 0, i))
    return pl.pallas_call(
        _post_kernel,
        out_shape=jax.ShapeDtypeStruct((b, s, d), F32),
        grid=(b, s // ts),
        in_specs=[tok(d), feat(W_GROUP), feat(W_GROUP), tok(W_GROUP), feat(W_GROUP), feat(3 * W_GROUP), tok(W_GROUP),
                  pl.BlockSpec((1, 1, d), lambda bb, i: (bb, 0, 0)),
                  pl.BlockSpec((d, d), lambda bb, i: (0, 0)),
                  pl.BlockSpec((1, d), lambda bb, i: (0, 0))],
        out_specs=tok(d),
        compiler_params=_cparams(2), name="post_proj",
    )(x, oa, ob, oc, od, gt, gc, gate, w_out_bf, final_norm)


def _angles(pos, half, theta):
    inv = jnp.power(jnp.float32(theta), -jnp.arange(half, dtype=F32) / half)
    return pos[:, None] * inv[None, :]


def _tables(s):
    pos = jnp.arange(s, dtype=F32)
    rows = s // GRID_W
    row = jnp.repeat(jnp.arange(rows, dtype=F32), GRID_W)
    col = jnp.tile(jnp.arange(GRID_W, dtype=F32), rows)
    a = _angles(pos, ROPE_A // 2, MLA_ROPE_THETA)
    c, sn = jnp.cos(a), jnp.sin(a)
    z64, z32 = jnp.zeros((s, 64), F32), jnp.zeros((s, 32), F32)
    tb = {"mkc": jnp.concatenate([z64, c, c, z32], axis=1),
          "mks": jnp.concatenate([z64, -sn, sn, z32], axis=1),
          "mqc": jnp.concatenate([c, c], axis=1).T,
          "mqs": jnp.concatenate([-sn, sn], axis=1).T}
    a = _angles(pos, DK_B // 8, ROPE_THETA)
    c, sn = jnp.cos(a), jnp.sin(a)
    c32 = jnp.concatenate([c, c, jnp.ones((s, DK_B - DK_B // 4), F32)], axis=1)
    s32 = jnp.concatenate([-sn, sn, jnp.zeros((s, DK_B - DK_B // 4), F32)], axis=1)
    tb.update({"dkc": jnp.tile(c32, (1, 2 * H_B)), "dks": jnp.tile(s32, (1, 2 * H_B)),
               "dqc": c32.T, "dqs": s32.T})
    ar_ = _angles(row, HD_D // 4, AXIAL_THETA)
    ac_ = _angles(col, HD_D // 4, AXIAL_THETA)
    cr, sr, cc, sc = jnp.cos(ar_), jnp.sin(ar_), jnp.cos(ac_), jnp.sin(ac_)
    c64 = jnp.concatenate([cr, cr, cc, cc], axis=1)
    s64 = jnp.concatenate([-sr, sr, -sc, sc], axis=1)
    tb.update({"gkc": jnp.tile(c64, (1, KV_H_D)), "gks": jnp.tile(s64, (1, KV_H_D)),
               "gqc": c64.T, "gqs": s64.T})
    return tb


def _blockdiag_pair(w, half):
    z = jnp.zeros((BW_C, BW_C), F32)
    top = jnp.concatenate([w[2 * half], z], axis=1)
    bot = jnp.concatenate([z, w[2 * half + 1]], axis=1)
    return jnp.concatenate([top, bot], axis=0)


def _layer_weights(l, p):
    w = p["w_in"][l]
    col = lambda o, n: w[:, o:o + n]
    z64 = jnp.zeros((D_MODEL, 64), F32)
    z32 = jnp.zeros((D_MODEL, 32), F32)
    kra = col(O_KRA, ROPE_A)
    w_std = jnp.concatenate([
        col(O_KVA, KV_LORA),
        z64, kra, z32,
        z64, kra[:, P_MLA], z32,
        col(O_KB, 256), col(O_KD, 128),
        col(O_XC, W_GROUP), col(O_GC, W_GROUP)], axis=1)
    w_t = jnp.concatenate([
        col(O_QA, Q_LORA), col(O_KVA, KV_LORA),
        col(O_QB, 256), col(O_VB, 256),
        col(O_QD, 256), col(O_VD, 128),
        col(O_GA, W_GROUP), col(O_GB, W_GROUP), col(O_GD, W_GROUP)], axis=1).T
    wuq = p["mla_w_uq"][l]
    wukv = p["mla_w_ukv"][l]
    dkv = NOPE_A + V_A
    z_k = jnp.zeros((KV_LORA, LANES - NOPE_A), F32)
    w_ukp = jnp.concatenate(
        [jnp.concatenate([wukv[:, dkv * h:dkv * h + NOPE_A], z_k], axis=1) for h in range(H_A)], axis=1)
    w_uv = jnp.concatenate([wukv[:, dkv * h + NOPE_A:dkv * (h + 1)] for h in range(H_A)], axis=1)
    gq, gk = p["gqa_q_norm"][l], p["gqa_k_norm"][l]
    wa, wx = p["lru_wa"][l], p["lru_wx"][l]
    ba, bx = p["lru_ba"][l], p["lru_bx"][l]
    wbd = jnp.stack([jnp.concatenate([_blockdiag_pair(wa[0], hf), _blockdiag_pair(wx[0], hf),
                                      _blockdiag_pair(wa[1], hf), _blockdiag_pair(wx[1], hf)], axis=1)
                     for hf in range(2)])
    sl = lambda v, hf: v[LANES * hf:LANES * (hf + 1)]
    bias = jnp.stack([jnp.concatenate([sl(ba[0], hf), sl(bx[0], hf), sl(ba[1], hf), sl(bx[1], hf)])[None]
                      for hf in range(2)])
    lam = p["lru_lambda"][l]
    lam_h = jnp.stack([jnp.stack([sl(lam[0], hf), sl(lam[1], hf)]) for hf in range(2)])
    return {
        "norm_g": p["norm_g"][l][None], "w_std": w_std.astype(BF16), "w_t": w_t.astype(BF16),
        "gqa_col": p["mla_q_norm"][l][:, None], "w_uqT": wuq.T.astype(BF16),
        "gkv_row": p["mla_kv_norm"][l][None], "gkv_col": p["mla_kv_norm"][l][:, None],
        "w_ukp": w_ukp.astype(BF16), "w_uvT": w_uv.T.astype(BF16),
        "gq_col": gq[:, None], "gk_row": jnp.tile(gk, KV_H_D)[None],
        "diff_lambda": p["diff_lambda"][l], "subln_col": p["diff_subln"][l][:, None],
        "conv_w": p["lru_conv_w"][l], "conv_b": p["lru_conv_b"][l][None],
        "lru_wbd": (0.5 * wbd).astype(BF16), "lru_bias": 0.5 * bias, "lru_lam": lam_h,
        "w_out": p["w_out"][l].astype(BF16),
    }


def _trunk(x, mods, lws, final_norm):
    b, s, d = x.shape
    tb = _tables(s)
    prev = None
    for l in range(DEPTH):
        lw = lws[l]
        shift, scale, gate = mods[l]
        outs = _pre_call(x, shift, scale, lw, tb, prev)
        if prev is not None:
            x = outs[-1]
        (qt_mla, k_mla, vt_mla, qt_diff, k_diff, vt_diff, qt_gqa, k_gqa, vt_gqa, xc, gc,
         kn_mla, kn_diff, kn_gqa, gt) = outs[:15]
        lam_init = 0.8 - 0.6 * math.exp(-0.3 * l)
        oa = _attn_call(qt_mla, k_mla, vt_mla, kn_mla, MAPS_MLA, "mla", name="attn_mla")
        ob = _attn_call(qt_diff, k_diff, vt_diff, kn_diff, MAPS_DIFF, "diff",
                        extra=(lw["diff_lambda"], lw["subln_col"]), lam_init=lam_init, name="attn_diff")
        od = _attn_call(qt_gqa, k_gqa, vt_gqa, kn_gqa, MAPS_GQA, "gqa", name="attn_gqa")
        oc = _lru_call(xc, lw)
        prev = (oa, ob, oc, od, gt, gc, gate, lw["w_out"])
    return _post_call(x, *prev, final_norm)


def kernel(x_prompt, x_sample, c_prompt, c_sample, ada_w, ada_b, norm_g, w_in, mla_q_norm, mla_w_uq, mla_kv_norm, mla_w_ukv, diff_lambda, diff_subln, lru_conv_w, lru_conv_b, lru_wa, lru_ba, lru_wx, lru_bx, lru_lambda, gqa_q_norm, gqa_k_norm, w_out, final_norm):
    p = dict(norm_g=norm_g, w_in=w_in, mla_q_norm=mla_q_norm, mla_w_uq=mla_w_uq, mla_kv_norm=mla_kv_norm,
             mla_w_ukv=mla_w_ukv, diff_lambda=diff_lambda, diff_subln=diff_subln,
             lru_conv_w=lru_conv_w, lru_conv_b=lru_conv_b, lru_wa=lru_wa, lru_ba=lru_ba,
             lru_wx=lru_wx, lru_bx=lru_bx, lru_lambda=lru_lambda,
             gqa_q_norm=gqa_q_norm, gqa_k_norm=gqa_k_norm, w_out=w_out)
    lws = [_layer_weights(l, p) for l in range(DEPTH)]
    nb = x_prompt.shape[0]
    mod = _modulation(jnp.concatenate([c_prompt, c_sample], axis=0), ada_w.astype(BF16), ada_b)

    def mods_for(lo, hi):
        out = []
        for l in range(DEPTH):
            m = mod[l, lo:hi]
            out.append(tuple(m[:, D_MODEL * j:D_MODEL * (j + 1)][:, None, :] for j in range(3)))
        return out

    fn = final_norm[None]
    y_prompt = _trunk(x_prompt, mods_for(0, nb), lws, fn)
    y_sample = _trunk(x_sample, mods_for(nb, nb + x_sample.shape[0]), lws, fn)
    return (y_prompt, y_sample)
```

```python
import functools
import math

import jax
import jax.numpy as jnp
import numpy as np
from jax import lax
from jax.experimental import pallas as pl
from jax.experimental.pallas import tpu as pltpu

F32 = jnp.float32
BF16 = jnp.bfloat16

D_MODEL = 1024
DEPTH = 2
W_GROUP = 256
H_A, NOPE_A, ROPE_A, V_A = 4, 64, 32, 64
Q_LORA, KV_LORA = 192, 128
H_B, DK_B, DV_B = 4, 32, 64
H_C, BW_C, CONV_W = 4, 64, 4
CONV_PAD_L = CONV_W // 2
LRU_C = 8.0
H_D, KV_H_D, HD_D = 4, 2, 64
DV = 64
assert V_A == DV_B == HD_D == DV
GRID_W = 64
ROPE_THETA = 500000.0
MLA_ROPE_THETA = 10000.0
AXIAL_THETA = 10000.0
EPS = 1e-6
LOG2E = 1.4426950408889634

IN_SIZES = (Q_LORA, KV_LORA, ROPE_A, W_GROUP,
            H_B * 2 * DK_B, H_B * 2 * DK_B, H_B * DV_B, W_GROUP,
            W_GROUP, W_GROUP,
            H_D * HD_D, KV_H_D * HD_D, KV_H_D * HD_D, W_GROUP)
_OFF = np.concatenate([[0], np.cumsum(IN_SIZES)]).tolist()
(O_QA, O_KVA, O_KRA, O_GA, O_QB, O_KB, O_VB, O_GB, O_XC, O_GC, O_QD, O_KD, O_VD, O_GD, _) = _OFF

LANES = 128
SUBLANES = 8

TOK_TILE = 512
POST_TILE = 1024
Q_TILE = 512
UNITS_PER_STEP = 64
LRU_CHUNK = 512
DENOM_LOG2_RANGE = 80
SUB_KEYS = 256
SCORE_LEAD = 2
VMEM_LIMIT = 56 * 1024 * 1024

ZS_KVL, ZS_KPE, ZS_KPE_SW, ZS_KB, ZS_KD, ZS_XC, ZS_G, ZS_END = (0, 128, 256, 384, 640, 768, 1024, 1280)
ZT_QA, ZT_QB, ZT_VB, ZT_QD, ZT_VD, ZT_G, ZT_END = (0, 192, 448, 704, 960, 1088, 1856)


def _partner(width, block, rot, half):
    idx = np.arange(width)
    d = idx % block
    out = idx.copy()
    in_rot = d < rot
    first = in_rot & ((d % (2 * half)) < half)
    second = in_rot & ~((d % (2 * half)) < half)
    out[first] = idx[first] + half
    out[second] = idx[second] - half
    return out


P_MLA = _partner(ROPE_A, ROPE_A, ROPE_A, ROPE_A // 2)


NORM_ROWS = 2 * SUBLANES


def _indicator(width, seg):
    m = np.zeros((NORM_ROWS, width), np.float32)
    m[np.arange(width) // seg, np.arange(width)] = 1.0
    return m


IND_MLA = _indicator(H_A * LANES, LANES)
IND_DIFF = _indicator(2 * H_B * DK_B, DK_B)
IND_GQA = _indicator(KV_H_D * HD_D, HD_D)


def _cparams(n_axes):
    return pltpu.CompilerParams(dimension_semantics=("arbitrary",) * n_axes,
                                vmem_limit_bytes=VMEM_LIMIT)


def _mod_kernel(c_ref, w_ref, b_ref, out_ref):
    c = c_ref[...]
    sc = (c * jax.nn.sigmoid(c)).astype(BF16)
    out_ref[0] = jnp.dot(sc, w_ref[0], preferred_element_type=F32) + b_ref[0]


def _modulation(c_all, ada_w_bf, ada_b):
    n = c_all.shape[0]
    return pl.pallas_call(
        _mod_kernel,
        out_shape=jax.ShapeDtypeStruct((DEPTH, n, 3 * D_MODEL), F32),
        grid=(DEPTH, 3),
        in_specs=[
            pl.BlockSpec((n, D_MODEL), lambda l, j: (0, 0)),
            pl.BlockSpec((1, D_MODEL, D_MODEL), lambda l, j: (l, 0, j)),
            pl.BlockSpec((1, 1, D_MODEL), lambda l, j: (l, 0, j)),
        ],
        out_specs=pl.BlockSpec((1, n, D_MODEL), lambda l, j: (l, 0, j)),
        compiler_params=_cparams(2),
        name="adaln_mod",
    )(c_all, ada_w_bf, ada_b.reshape(DEPTH, 1, 3 * D_MODEL))


def _pre_kernel(x_ref, shift_ref, scale_ref, ng_ref, wstd_ref, wt_ref, *rest):
    z, zt = _pre_project(x_ref[0], shift_ref, scale_ref, ng_ref, wstd_ref, wt_ref)
    _pre_finish(z, zt, *rest)


def _post_pre_kernel(x_ref, oa_ref, ob_ref, oc_ref, od_ref, gt_in_ref, gc_in_ref, gate_ref, wout_ref,
                     shift_ref, scale_ref, ng_ref, wstd_ref, wt_ref, *rest):
    xn = _post_update(x_ref, (oa_ref, ob_ref, oc_ref, od_ref), gt_in_ref, gc_in_ref, gate_ref, wout_ref)
    rest[-1][0] = xn
    z, zt = _pre_project(xn, shift_ref, scale_ref, ng_ref, wstd_ref, wt_ref)
    _pre_finish(z, zt, *rest[:-1])


def _pre_project(x, shift_ref, scale_ref, ng_ref, wstd_ref, wt_ref):
    ms = jnp.mean(x * x, axis=-1, keepdims=True)
    gain = ng_ref[...] * (1.0 + scale_ref[0])
    hb = ((x * lax.rsqrt(ms + EPS)) * gain + shift_ref[0]).astype(BF16)
    z = jnp.dot(hb, wstd_ref[...], preferred_element_type=F32)
    zt = lax.dot_general(wt_ref[...], hb, (((1,), (1,)), ((), ())),
                         preferred_element_type=F32)
    return z, zt


def _pre_finish(z, zt,
              gqa_col_ref, wuqT_ref, gkv_row_ref, wukp_ref, wuvT_ref,
              gq_col_ref, gk_row_ref,
              mkc_ref, mks_ref, mqc_ref, mqs_ref,
              dkc_ref, dks_ref, dqc_ref, dqs_ref,
              gkc_ref, gks_ref, gqc_ref, gqs_ref,
              ind_mla_ref, ind_diff_ref, ind_gqa_ref,
              qt_mla_ref, k_mla_ref, vt_mla_ref,
              qt_diff_ref, k_diff_ref, vt_diff_ref,
              qt_gqa_ref, k_gqa_ref, vt_gqa_ref,
              xc_ref, gc_ref, kn_mla_ref, kn_diff_ref, kn_gqa_ref, gt_ref):
    ts = z.shape[0]

    def key_norms(kb, ind_ref, out_ref):
        kf = kb.astype(F32)
        r = lax.dot_general(ind_ref[...], (kf * kf).astype(BF16), (((1,), (1,)), ((), ())),
                            preferred_element_type=F32)
        out_ref[0, 0] = jnp.broadcast_to(jnp.max(r, axis=1, keepdims=True), (NORM_ROWS, LANES))

    qlt = zt[ZT_QA:ZT_QA + Q_LORA]
    msq = jnp.mean(qlt * qlt, axis=0, keepdims=True)
    qnb = ((qlt * lax.rsqrt(msq + EPS)) * gqa_col_ref[...]).astype(BF16)
    qat = jnp.dot(wuqT_ref[...], qnb, preferred_element_type=F32)
    sc_mla = (NOPE_A + ROPE_A) ** -0.5 * LOG2E
    dq = NOPE_A + ROPE_A
    pad = jnp.zeros((LANES - dq, ts), F32)
    for hh in range(H_A):
        nope = qat[dq * hh:dq * hh + NOPE_A]
        pe = qat[dq * hh + NOPE_A:dq * (hh + 1)]
        sw = jnp.concatenate([pe[ROPE_A // 2:], pe[:ROPE_A // 2]], axis=0)
        pe = pe * mqc_ref[...] + sw * mqs_ref[...]
        blk = jnp.concatenate([nope, pe, pad], axis=0) * sc_mla
        qt_mla_ref[0, LANES * hh:LANES * (hh + 1), :] = blk.astype(BF16)

    kvl = z[:, ZS_KVL:ZS_KVL + KV_LORA]
    msk = jnp.mean(kvl * kvl, axis=-1, keepdims=True)
    kvn_f = (kvl * lax.rsqrt(msk + EPS)) * gkv_row_ref[...]
    kvn = kvn_f.astype(BF16)
    knope = jnp.dot(kvn, wukp_ref[...], preferred_element_type=F32)
    kpe = z[:, ZS_KPE:ZS_KPE + LANES] * mkc_ref[...] + z[:, ZS_KPE_SW:ZS_KPE_SW + LANES] * mks_ref[...]
    kb = jnp.concatenate([(knope[:, LANES * hh:LANES * (hh + 1)] + kpe).astype(BF16) for hh in range(H_A)],
                         axis=1)
    k_mla_ref[0] = kb
    key_norms(kb, ind_mla_ref, kn_mla_ref)
    vt_mla_ref[0, 0] = jnp.dot(wuvT_ref[...], kvn_f.T.astype(BF16),
                               preferred_element_type=F32).astype(BF16)

    kd = z[:, ZS_KB:ZS_KB + 256]
    lane_d = lax.broadcasted_iota(jnp.int32, (ts, 256), 1)
    half_d = DK_B // 8
    kd_sw = jnp.where(lane_d % (2 * half_d) < half_d, pltpu.roll(kd, 256 - half_d, 1), pltpu.roll(kd, half_d, 1))
    kb = (kd * dkc_ref[...] + kd_sw * dks_ref[...]).astype(BF16)
    k_diff_ref[0] = kb
    key_norms(kb, ind_diff_ref, kn_diff_ref)
    sc_diff = DK_B ** -0.5 * LOG2E
    rot = DK_B // 4
    for m in range(2 * H_B):
        q = zt[ZT_QB + DK_B * m:ZT_QB + DK_B * (m + 1)]
        q8 = q[0:rot]
        q8 = q8 * dqc_ref[0:rot, :] + pltpu.roll(q8, rot // 2, 0) * dqs_ref[0:rot, :]
        qq = jnp.concatenate([q8, q[rot:]], axis=0) * sc_diff
        qt_diff_ref[0, DK_B * m:DK_B * (m + 1), :] = qq.astype(BF16)
    vt_diff_ref[0, 0] = zt[ZT_VB:ZT_VB + 256].astype(BF16)

    kg = z[:, ZS_KD:ZS_KD + LANES]
    k2 = kg * kg
    lo = lax.broadcasted_iota(jnp.int32, (ts, LANES), 1) < HD_D
    s_lo = jnp.sum(jnp.where(lo, k2, 0.0), axis=-1, keepdims=True)
    s_hi = jnp.sum(jnp.where(lo, 0.0, k2), axis=-1, keepdims=True)
    rk = lax.rsqrt(jnp.where(lo, s_lo, s_hi) / HD_D + EPS)
    kn = (kg * rk) * gk_row_ref[...]
    half_g = HD_D // 4
    lane_g = lax.broadcasted_iota(jnp.int32, (ts, LANES), 1)
    knsw = jnp.where(lane_g % (2 * half_g) < half_g, pltpu.roll(kn, LANES - half_g, 1), pltpu.roll(kn, half_g, 1))
    kb = (kn * gkc_ref[...] + knsw * gks_ref[...]).astype(BF16)
    k_gqa_ref[0] = kb
    key_norms(kb, ind_gqa_ref, kn_gqa_ref)
    sc_gqa = HD_D ** -0.5 * LOG2E
    for hh in range(H_D):
        q = zt[ZT_QD + HD_D * hh:ZT_QD + HD_D * (hh + 1)]
        r = lax.rsqrt(jnp.mean(q * q, axis=0, keepdims=True) + EPS)
        qn = (q * r) * gq_col_ref[...]
        qsw = jnp.concatenate([qn[half_g:2 * half_g], qn[0:half_g], qn[3 * half_g:], qn[2 * half_g:3 * half_g]],
                              axis=0)
        qq = (qn * gqc_ref[...] + qsw * gqs_ref[...]) * sc_gqa
        qt_gqa_ref[0, HD_D * hh:HD_D * (hh + 1), :] = qq.astype(BF16)
    vt_gqa_ref[0, 0] = zt[ZT_VD:ZT_VD + 128].astype(BF16)

    xc_ref[0] = z[:, ZS_XC:ZS_XC + W_GROUP]
    g = z[:, ZS_G:ZS_END]
    gc_ref[0] = (g * jax.nn.sigmoid(g)).astype(BF16)
    g = zt[ZT_G:ZT_END]
    gt_ref[0] = (g * jax.nn.sigmoid(g)).astype(BF16)


def _pre_call(x, shift, scale, lw, tb, prev=None):
    b, s, d = x.shape
    ts = TOK_TILE
    nt = s // ts
    full2 = lambda a: pl.BlockSpec(a.shape, lambda i, bb: (0, 0))
    tok_rows = lambda w: pl.BlockSpec((ts, w), lambda i, bb: (i, 0))
    tok_cols = lambda r: pl.BlockSpec((r, ts), lambda i, bb: (0, i))
    tok3 = lambda w: pl.BlockSpec((1, ts, w), lambda i, bb: (bb, i, 0))
    vec3 = pl.BlockSpec((1, 1, d), lambda i, bb: (bb, 0, 0))
    params = [lw["norm_g"], lw["w_std"], lw["w_t"],
              lw["gqa_col"], lw["w_uqT"], lw["gkv_row"], lw["w_ukp"], lw["w_uvT"],
              lw["gq_col"], lw["gk_row"]]
    in_specs = [tok3(d)]
    head = [x]
    if prev is not None:
        feat3 = lambda r: pl.BlockSpec((1, r, ts), lambda i, bb: (bb, 0, i))
        in_specs += [feat3(W_GROUP), feat3(W_GROUP), tok3(W_GROUP), feat3(W_GROUP),
                     feat3(3 * W_GROUP), tok3(W_GROUP), vec3, full2(prev[7])]
        head += list(prev)
    in_specs += [vec3, vec3]
    in_specs += [full2(p) for p in params]
    in_specs += [tok_rows(128), tok_rows(128), tok_cols(32), tok_cols(32),
                 tok_rows(256), tok_rows(256), tok_cols(32), tok_cols(32),
                 tok_rows(128), tok_rows(128), tok_cols(64), tok_cols(64)]
    tables = [tb["mkc"], tb["mks"], tb["mqc"], tb["mqs"],
              tb["dkc"], tb["dks"], tb["dqc"], tb["dqs"],
              tb["gkc"], tb["gks"], tb["gqc"], tb["gqs"]]
    inds = [jnp.asarray(IND_MLA, BF16), jnp.asarray(IND_DIFF, BF16), jnp.asarray(IND_GQA, BF16)]
    in_specs += [full2(a) for a in inds]
    kn_shape = jax.ShapeDtypeStruct((b, nt, NORM_ROWS, LANES), F32)
    kn_spec = pl.BlockSpec((1, 1, NORM_ROWS, LANES), lambda i, bb: (bb, i, 0, 0))

    def qt_spec(rows):
        return pl.BlockSpec((1, rows, ts), lambda i, bb: (bb, 0, i))

    def k_spec(w):
        return pl.BlockSpec((1, ts, w), lambda i, bb: (bb, i, 0))

    def vt_spec(rows):
        return pl.BlockSpec((1, 1, rows, ts), lambda i, bb: (bb, i, 0, 0))

    out_shape = [
        jax.ShapeDtypeStruct((b, 512, s), BF16), jax.ShapeDtypeStruct((b, s, 512), BF16),
        jax.ShapeDtypeStruct((b, nt, 256, ts), BF16),
        jax.ShapeDtypeStruct((b, 256, s), BF16), jax.ShapeDtypeStruct((b, s, 256), BF16),
        jax.ShapeDtypeStruct((b, nt, 256, ts), BF16),
        jax.ShapeDtypeStruct((b, 256, s), BF16), jax.ShapeDtypeStruct((b, s, 128), BF16),
        jax.ShapeDtypeStruct((b, nt, 128, ts), BF16),
        jax.ShapeDtypeStruct((b, s, 256), F32), jax.ShapeDtypeStruct((b, s, 256), BF16),
        kn_shape, kn_shape, kn_shape, jax.ShapeDtypeStruct((b, 3 * W_GROUP, s), BF16),
    ]
    out_specs = [qt_spec(512), k_spec(512), vt_spec(256),
                 qt_spec(256), k_spec(256), vt_spec(256),
                 qt_spec(256), k_spec(128), vt_spec(128),
                 k_spec(256), k_spec(256), kn_spec, kn_spec, kn_spec, qt_spec(3 * W_GROUP)]
    if prev is not None:
        out_shape.append(jax.ShapeDtypeStruct((b, s, d), F32))
        out_specs.append(tok3(d))
    return pl.pallas_call(
        _pre_kernel if prev is None else _post_pre_kernel,
        out_shape=out_shape, grid=(nt, b), in_specs=in_specs, out_specs=out_specs,
        compiler_params=_cparams(2), name="pre_proj" if prev is None else "post_pre_proj",
    )(*head, shift, scale, *params, *tables, *inds)


def _attn_kernel(*refs, maps, n_chunks, tk, tq, kind, lam_init):
    if kind == "diff":
        qt_ref, k_ref, vt_ref, kn_ref, lam_ref, subln_ref, out_ref, w_scr = refs
    else:
        qt_ref, k_ref, vt_ref, kn_ref, out_ref, w_scr = refs
        lam_ref = subln_ref = None
    n_tiles = qt_ref.shape[2] // tq
    dv = DV
    nm = len(maps)
    finish = functools.partial(_attn_finish, out_ref=out_ref, lam_ref=lam_ref, subln_ref=subln_ref,
                               kind=kind, lam_init=lam_init, tq=tq)

    for t in range(n_tiles):
        for i, (kg, q0, qn, woff, v0, _) in enumerate(maps):
            q = qt_ref[0, q0:q0 + qn, tq * t:tq * (t + 1)]
            pieces = []
            if woff > 0:
                pieces.append(jnp.zeros((woff, tq), BF16))
            pieces.append(q)
            if LANES - woff - qn > 0:
                pieces.append(jnp.zeros((LANES - woff - qn, tq), BF16))
            w_scr[t, i] = pieces[0] if len(pieces) == 1 else jnp.concatenate(pieces, axis=0)

    def scores(t, i, c, r1=0, rows=tk):
        kg = maps[i][0]
        r0 = c * tk + r1 if isinstance(c, int) else pl.multiple_of(c * tk, tk)
        return jnp.dot(k_ref[0, pl.ds(r0, rows), LANES * kg:LANES * (kg + 1)], w_scr[t, i],
                       preferred_element_type=F32)

    def pv(i, c, p, r1=0, rows=tk):
        v0 = maps[i][4]
        return jnp.dot(vt_ref[0, c, v0:v0 + dv, r1:r1 + rows], p.astype(BF16),
                       preferred_element_type=F32)

    kn2 = jnp.max(kn_ref[0], axis=0)
    kmax = [jnp.sqrt(kn2[maps[i][5]:maps[i][5] + 1, 0:1]) for i in range(nm)]
    bounds = []
    for t in range(n_tiles):
        for i in range(nm):
            w = w_scr[t, i].astype(F32)
            bounds.append(jnp.sqrt(jnp.sum(w * w, axis=0, keepdims=True)) * kmax[i])
    units = [(t, i, c, r1) for t in range(n_tiles) for c in range(n_chunks)
             for r1 in range(0, tk, SUB_KEYS) for i in range(nm)]
    per_tile = len(units) // n_tiles
    accs, ls = [None] * (n_tiles * nm), [None] * (n_tiles * nm)
    pending = []
    lo, hi = 2.0 ** -DENOM_LOG2_RANGE, 2.0 ** DENOM_LOG2_RANGE
    for n in range(len(units) + SCORE_LEAD):
        if n < len(units):
            t, i, c, r1 = units[n]
            pending.append(scores(t, i, c, r1, SUB_KEYS))
        if n >= SCORE_LEAD:
            t, i, c, r1 = units[n - SCORE_LEAD]
            j = t * nm + i
            p = jnp.exp2(pending.pop(0) - bounds[j])
            part, psum = pv(i, c, p, r1, SUB_KEYS), jnp.sum(p, axis=0, keepdims=True)
            accs[j] = part if accs[j] is None else accs[j] + part
            ls[j] = psum if ls[j] is None else ls[j] + psum
            if (n - SCORE_LEAD + 1) % per_tile == 0:
                finish(t, accs[t * nm:(t + 1) * nm], ls[t * nm:(t + 1) * nm])

    for t in range(n_tiles):
        bad = None
        for l in ls[t * nm:(t + 1) * nm]:
            b = jnp.where((l > lo) & (l < hi), 0.0, 1.0)
            bad = b if bad is None else jnp.maximum(bad, b)

        @pl.when(jnp.max(bad) > 0.0)
        def _(t=t):
            def safe_body(c, carry):
                new = []
                for i in range(nm):
                    m, l, acc = carry[i]
                    s = scores(t, i, c)
                    mn = jnp.maximum(m, jnp.max(s, axis=0, keepdims=True))
                    p, alpha = jnp.exp2(s - mn), jnp.exp2(m - mn)
                    new.append((mn, alpha * l + jnp.sum(p, axis=0, keepdims=True), alpha * acc + pv(i, c, p)))
                return tuple(new)

            init = tuple((jnp.full((1, tq), -jnp.inf, F32), jnp.zeros((1, tq), F32), jnp.zeros((dv, tq), F32))
                         for _ in range(nm))
            fs = lax.fori_loop(0, n_chunks, safe_body, init)
            finish(t, [f[2] for f in fs], [f[1] for f in fs])


def _attn_finish(t, accs, ls, out_ref, lam_ref, subln_ref, kind, lam_init, tq):
    os_ = [acc / l for acc, l in zip(accs, ls)]

    if kind == "diff":
        lf = lam_ref[...]
        s1 = jnp.sum(lf[0:1] * lf[1:2], axis=-1, keepdims=True)
        s2 = jnp.sum(lf[2:3] * lf[3:4], axis=-1, keepdims=True)
        lam_full = jnp.exp(s1) - jnp.exp(s2) + lam_init
        heads = []
        for hh in range(H_B):
            o = os_[2 * hh] - lam_full * os_[2 * hh + 1]
            r = lax.rsqrt(jnp.mean(o * o, axis=0, keepdims=True) + EPS)
            heads.append(((o * r) * subln_ref[...]) * (1.0 - lam_init))
        ot = jnp.concatenate(heads, axis=0)
    else:
        ot = jnp.concatenate(os_, axis=0)
    out_ref[0, :, tq * t:tq * (t + 1)] = ot.astype(out_ref.dtype)


def _attn_call(qt, k, vt, kn, maps, kind, extra=(), lam_init=0.0, name="attn"):
    b, fq, s = qt.shape
    fk = k.shape[2]
    _, nt, fv, tk = vt.shape
    tq = Q_TILE
    n_tiles = 1 if kind == "diff" else max(1, min(s // tq, (UNITS_PER_STEP * tk) // (len(maps) * s)))
    tqs = tq * n_tiles
    in_specs = [pl.BlockSpec((1, fq, tqs), lambda bb, qi: (bb, 0, qi)),
                pl.BlockSpec((1, s, fk), lambda bb, qi: (bb, 0, 0)),
                pl.BlockSpec((1, nt, fv, tk), lambda bb, qi: (bb, 0, 0, 0)),
                pl.BlockSpec((1, nt, NORM_ROWS, LANES), lambda bb, qi: (bb, 0, 0, 0))]
    in_specs += [pl.BlockSpec(e.shape, lambda bb, qi: (0, 0)) for e in extra]
    kern = functools.partial(_attn_kernel, maps=tuple(maps), n_chunks=nt, tk=tk, tq=tq, kind=kind,
                             lam_init=lam_init)
    return pl.pallas_call(
        kern, out_shape=jax.ShapeDtypeStruct((b, W_GROUP, s), BF16),
        grid=(b, s // tqs), in_specs=in_specs,
        out_specs=pl.BlockSpec((1, W_GROUP, tqs), lambda bb, qi: (bb, 0, qi)),
        scratch_shapes=[pltpu.VMEM((n_tiles, len(maps), LANES, tq), BF16)],
        compiler_params=_cparams(2), name=name,
    )(qt, k, vt, kn, *extra)


MAPS_MLA = [(hh, LANES * hh, LANES, 0, V_A * hh, hh) for hh in range(H_A)]
MAPS_DIFF = [(m // 4, DK_B * m, DK_B, DK_B * (m % 4), DV_B * (m // 2), m) for m in range(2 * H_B)]
MAPS_GQA = [(0, HD_D * hh, HD_D, HD_D * (hh // (H_D // KV_H_D)), HD_D * (hh // (H_D // KV_H_D)),
             hh // (H_D // KV_H_D)) for hh in range(H_D)]


def _softplus(x):
    return jnp.maximum(x, 0.0) + jnp.log1p(jnp.exp(-jnp.abs(x)))


def _lru_kernel(xc_ref, cw_ref, cb_ref, wbd_ref, bias_ref, lam_ref, out_ref,
                xpad, af, gf, ar, gr, hfl, pfl, hrl, prl, *, s, tc):
    tseg = s // SUBLANES
    tl_chunk = tc // SUBLANES
    xpad[0:tseg, :] = jnp.zeros((tseg, LANES), F32)
    xpad[tseg + s:2 * tseg + s, :] = jnp.zeros((tseg, LANES), F32)
    xpad[tseg:tseg + s, :] = xc_ref[0]
    lam = lam_ref[0]
    sp_f = _softplus(-lam[0:1])
    sp_r = _softplus(-lam[1:2])
    cw = cw_ref[...]
    cb = cb_ref[...]
    wbd = wbd_ref[0]
    bias = bias_ref[0]

    def gates(half_pre_r, half_pre_i, sp, half_xconv):
        log_a = (jnp.tanh(half_pre_r) + 1.0) * ((-0.5 * LRU_C) * sp)
        a = jnp.exp(log_a)
        v = -jnp.tanh(log_a) * (a * a + 1.0)
        gx = jnp.where(v > 0.0, v * lax.rsqrt(v), 0.0) * ((jnp.tanh(half_pre_i) + 1.0) * half_xconv)
        return a, gx

    def chunk(ci, carry):
        tl0 = ci * tl_chunk
        tiles = [xpad[pl.ds(tseg + tl0 + k, SUBLANES, stride=tseg), :]
                 for k in range(-CONV_PAD_L, tl_chunk + CONV_W - 1 - CONV_PAD_L)]
        xconv = cb
        for j in range(CONV_W):
            xconv = xconv + jnp.concatenate(tiles[j:j + tl_chunk], axis=0) * cw[j:j + 1]
        pre = jnp.dot(xconv.astype(BF16), wbd, preferred_element_type=F32) + bias
        half_xconv = 0.5 * xconv
        r0 = pl.multiple_of(ci * tc, tc)
        a, gx = gates(pre[:, 0:128], pre[:, 128:256], sp_f, half_xconv)
        af[pl.ds(r0, tc), :] = a
        gf[pl.ds(r0, tc), :] = gx
        a, gx = gates(pre[:, 256:384], pre[:, 384:512], sp_r, half_xconv)
        ar[pl.ds(r0, tc), :] = a
        gr[pl.ds(r0, tc), :] = gx
        return carry

    lax.fori_loop(0, s // tc, chunk, 0)

    def scan_body(t, carry):
        hf, pf, hr, pr = carry
        r0 = pl.multiple_of(t * SUBLANES, SUBLANES)
        a = af[pl.ds(r0, SUBLANES), :]
        hf = a * hf + gf[pl.ds(r0, SUBLANES), :]
        pf = a * pf
        hfl[pl.ds(r0, SUBLANES), :] = hf
        pfl[pl.ds(r0, SUBLANES), :] = pf
        r1 = pl.multiple_of((tseg - 1 - t) * SUBLANES, SUBLANES)
        a = ar[pl.ds(r1, SUBLANES), :]
        hr = a * hr + gr[pl.ds(r1, SUBLANES), :]
        pr = a * pr
        hrl[pl.ds(r1, SUBLANES), :] = hr
        prl[pl.ds(r1, SUBLANES), :] = pr
        return hf, pf, hr, pr

    zero8 = jnp.zeros((SUBLANES, LANES), F32)
    one8 = jnp.ones((SUBLANES, LANES), F32)
    h_end, p_end, h_start, p_start = lax.fori_loop(0, tseg, scan_body, (zero8, one8, zero8, one8), unroll=8)

    row = lax.broadcasted_iota(jnp.int32, (SUBLANES, LANES), 0)
    cf, cr = zero8, zero8
    for _ in range(SUBLANES - 1):
        cf = jnp.where(row >= 1, pltpu.roll(h_end + p_end * cf, 1, 0), 0.0)
        cr = jnp.where(row < SUBLANES - 1, pltpu.roll(h_start + p_start * cr, SUBLANES - 1, 0), 0.0)

    ho = af

    def patch(ci, carry):
        r0 = pl.multiple_of(ci * tc, tc)
        cft = jnp.concatenate([cf] * tl_chunk, axis=0)
        crt = jnp.concatenate([cr] * tl_chunk, axis=0)
        ho[pl.ds(r0, tc), :] = (hfl[pl.ds(r0, tc), :] + pfl[pl.ds(r0, tc), :] * cft) + (
            hrl[pl.ds(r0, tc), :] + prl[pl.ds(r0, tc), :] * crt)
        return carry

    lax.fori_loop(0, s // tc, patch, 0)

    def untangle(u, carry):
        for j in range(SUBLANES):
            base = u * (2 * SUBLANES * SUBLANES) + j
            blk = jnp.concatenate([ho[pl.ds(base, SUBLANES, stride=SUBLANES), :],
                                   ho[pl.ds(base + SUBLANES * SUBLANES, SUBLANES, stride=SUBLANES), :]], axis=0)
            out_ref[0, pl.ds(pl.multiple_of(j * tseg + u * 2 * SUBLANES, 2 * SUBLANES), 2 * SUBLANES), :] = (
                blk.astype(out_ref.dtype))
        return carry

    lax.fori_loop(0, tseg // (2 * SUBLANES), untangle, 0)


def _lru_call(xc, lw):
    b, s, _ = xc.shape
    kern = functools.partial(_lru_kernel, s=s, tc=LRU_CHUNK)
    seq = pltpu.VMEM((s, LANES), F32)
    return pl.pallas_call(
        kern, out_shape=jax.ShapeDtypeStruct((b, s, W_GROUP), BF16),
        grid=(b, W_GROUP // LANES),
        in_specs=[pl.BlockSpec((1, s, LANES), lambda bb, hf: (bb, 0, hf)),
                  pl.BlockSpec((CONV_W, LANES), lambda bb, hf: (0, hf)),
                  pl.BlockSpec((1, LANES), lambda bb, hf: (0, hf)),
                  pl.BlockSpec((1, LANES, 4 * LANES), lambda bb, hf: (hf, 0, 0)),
                  pl.BlockSpec((1, 1, 4 * LANES), lambda bb, hf: (hf, 0, 0)),
                  pl.BlockSpec((1, 2, LANES), lambda bb, hf: (hf, 0, 0))],
        out_specs=pl.BlockSpec((1, s, LANES), lambda bb, hf: (bb, 0, hf)),
        scratch_shapes=[pltpu.VMEM((s + 2 * (s // SUBLANES), LANES), F32)] + [seq] * 8,
        compiler_params=_cparams(2), name="rglru",
    )(xc, lw["conv_w"], lw["conv_b"], lw["lru_wbd"], lw["lru_bias"], lw["lru_lam"])


def _post_update(x_ref, o_refs, gt_ref, gc_ref, gate_ref, wout_ref):
    y = None
    row = 0
    for i, o_ref in enumerate(o_refs):
        w_blk = wout_ref[W_GROUP * i:W_GROUP * (i + 1), :]
        if i == 2:
            og = (o_ref[0].astype(F32) * gc_ref[0].astype(F32)).astype(BF16)
            part = jnp.dot(og, w_blk, preferred_element_type=F32)
        else:
            og = (o_ref[0].astype(F32) * gt_ref[0, row:row + W_GROUP, :].astype(F32)).astype(BF16)
            row += W_GROUP
            part = lax.dot_general(og, w_blk, (((0,), (0,)), ((), ())), preferred_element_type=F32)
        y = part if y is None else y + part
    return x_ref[0] + gate_ref[0] * y


def _post_kernel(x_ref, oa_ref, ob_ref, oc_ref, od_ref, gt_ref, gc_ref, gate_ref, wout_ref, fn_ref, out_ref):
    xn = _post_update(x_ref, (oa_ref, ob_ref, oc_ref, od_ref), gt_ref, gc_ref, gate_ref, wout_ref)
    ms = jnp.mean(xn * xn, axis=-1, keepdims=True)
    out_ref[0] = (xn * lax.rsqrt(ms + EPS)) * fn_ref[...]


def _post_call(x, oa, ob, oc, od, gt, gc, gate, w_out_bf, final_norm):
    b, s, d = x.shape
    ts = POST_TILE
    tok = lambda w: pl.BlockSpec((1, ts, w), lambda bb, i: (bb, i, 0))
    feat = lambda r: pl.BlockSpec((1, r, ts), lambda bb, i: (bb, 0, i))
    return pl.pallas_call(
        _post_kernel,
        out_shape=jax.ShapeDtypeStruct((b, s, d), F32),
        grid=(b, s // ts),
        in_specs=[tok(d), feat(W_GROUP), feat(W_GROUP), tok(W_GROUP), feat(W_GROUP), feat(3 * W_GROUP), tok(W_GROUP),
                  pl.BlockSpec((1, 1, d), lambda bb, i: (bb, 0, 0)),
                  pl.BlockSpec((d, d), lambda bb, i: (0, 0)),
                  pl.BlockSpec((1, d), lambda bb, i: (0, 0))],
        out_specs=tok(d),
        compiler_params=_cparams(2), name="post_proj",
    )(x, oa, ob, oc, od, gt, gc, gate, w_out_bf, final_norm)


def _angles(pos, half, theta):
    inv = jnp.power(jnp.float32(theta), -jnp.arange(half, dtype=F32) / half)
    return pos[:, None] * inv[None, :]


def _tables(s):
    pos = jnp.arange(s, dtype=F32)
    rows = s // GRID_W
    row = jnp.repeat(jnp.arange(rows, dtype=F32), GRID_W)
    col = jnp.tile(jnp.arange(GRID_W, dtype=F32), rows)
    a = _angles(pos, ROPE_A // 2, MLA_ROPE_THETA)
    c, sn = jnp.cos(a), jnp.sin(a)
    z64, z32 = jnp.zeros((s, 64), F32), jnp.zeros((s, 32), F32)
    tb = {"mkc": jnp.concatenate([z64, c, c, z32], axis=1),
          "mks": jnp.concatenate([z64, -sn, sn, z32], axis=1),
          "mqc": jnp.concatenate([c, c], axis=1).T,
          "mqs": jnp.concatenate([-sn, sn], axis=1).T}
    a = _angles(pos, DK_B // 8, ROPE_THETA)
    c, sn = jnp.cos(a), jnp.sin(a)
    c32 = jnp.concatenate([c, c, jnp.ones((s, DK_B - DK_B // 4), F32)], axis=1)
    s32 = jnp.concatenate([-sn, sn, jnp.zeros((s, DK_B - DK_B // 4), F32)], axis=1)
    tb.update({"dkc": jnp.tile(c32, (1, 2 * H_B)), "dks": jnp.tile(s32, (1, 2 * H_B)),
               "dqc": c32.T, "dqs": s32.T})
    ar_ = _angles(row, HD_D // 4, AXIAL_THETA)
    ac_ = _angles(col, HD_D // 4, AXIAL_THETA)
    cr, sr, cc, sc = jnp.cos(ar_), jnp.sin(ar_), jnp.cos(ac_), jnp.sin(ac_)
    c64 = jnp.concatenate([cr, cr, cc, cc], axis=1)
    s64 = jnp.concatenate([-sr, sr, -sc, sc], axis=1)
    tb.update({"gkc": jnp.tile(c64, (1, KV_H_D)), "gks": jnp.tile(s64, (1, KV_H_D)),
               "gqc": c64.T, "gqs": s64.T})
    return tb


def _blockdiag_pair(w, half):
    z = jnp.zeros((BW_C, BW_C), F32)
    top = jnp.concatenate([w[2 * half], z], axis=1)
    bot = jnp.concatenate([z, w[2 * half + 1]], axis=1)
    return jnp.concatenate([top, bot], axis=0)


def _layer_weights(l, p):
    w = p["w_in"][l]
    col = lambda o, n: w[:, o:o + n]
    z64 = jnp.zeros((D_MODEL, 64), F32)
    z32 = jnp.zeros((D_MODEL, 32), F32)
    kra = col(O_KRA, ROPE_A)
    w_std = jnp.concatenate([
        col(O_KVA, KV_LORA),
        z64, kra, z32,
        z64, kra[:, P_MLA], z32,
        col(O_KB, 256), col(O_KD, 128),
        col(O_XC, W_GROUP), col(O_GC, W_GROUP)], axis=1)
    w_t = jnp.concatenate([
        col(O_QA, Q_LORA),
        col(O_QB, 256), col(O_VB, 256),
        col(O_QD, 256), col(O_VD, 128),
        col(O_GA, W_GROUP), col(O_GB, W_GROUP), col(O_GD, W_GROUP)], axis=1).T
    wuq = p["mla_w_uq"][l]
    wukv = p["mla_w_ukv"][l]
    dkv = NOPE_A + V_A
    z_k = jnp.zeros((KV_LORA, LANES - NOPE_A), F32)
    w_ukp = jnp.concatenate(
        [jnp.concatenate([wukv[:, dkv * h:dkv * h + NOPE_A], z_k], axis=1) for h in range(H_A)], axis=1)
    w_uv = jnp.concatenate([wukv[:, dkv * h + NOPE_A:dkv * (h + 1)] for h in range(H_A)], axis=1)
    gq, gk = p["gqa_q_norm"][l], p["gqa_k_norm"][l]
    wa, wx = p["lru_wa"][l], p["lru_wx"][l]
    ba, bx = p["lru_ba"][l], p["lru_bx"][l]
    wbd = jnp.stack([jnp.concatenate([_blockdiag_pair(wa[0], hf), _blockdiag_pair(wx[0], hf),
                                      _blockdiag_pair(wa[1], hf), _blockdiag_pair(wx[1], hf)], axis=1)
                     for hf in range(2)])
    sl = lambda v, hf: v[LANES * hf:LANES * (hf + 1)]
    bias = jnp.stack([jnp.concatenate([sl(ba[0], hf), sl(bx[0], hf), sl(ba[1], hf), sl(bx[1], hf)])[None]
                      for hf in range(2)])
    lam = p["lru_lambda"][l]
    lam_h = jnp.stack([jnp.stack([sl(lam[0], hf), sl(lam[1], hf)]) for hf in range(2)])
    return {
        "norm_g": p["norm_g"][l][None], "w_std": w_std.astype(BF16), "w_t": w_t.astype(BF16),
        "gqa_col": p["mla_q_norm"][l][:, None], "w_uqT": wuq.T.astype(BF16),
        "gkv_row": p["mla_kv_norm"][l][None],
        "w_ukp": w_ukp.astype(BF16), "w_uvT": w_uv.T.astype(BF16),
        "gq_col": gq[:, None], "gk_row": jnp.tile(gk, KV_H_D)[None],
        "diff_lambda": p["diff_lambda"][l], "subln_col": p["diff_subln"][l][:, None],
        "conv_w": p["lru_conv_w"][l], "conv_b": p["lru_conv_b"][l][None],
        "lru_wbd": (0.5 * wbd).astype(BF16), "lru_bias": 0.5 * bias, "lru_lam": lam_h,
        "w_out": p["w_out"][l].astype(BF16),
    }


def _trunk(x, mods, lws, final_norm):
    b, s, d = x.shape
    tb = _tables(s)
    prev = None
    for l in range(DEPTH):
        lw = lws[l]
        shift, scale, gate = mods[l]
        outs = _pre_call(x, shift, scale, lw, tb, prev)
        if prev is not None:
            x = outs[-1]
        (qt_mla, k_mla, vt_mla, qt_diff, k_diff, vt_diff, qt_gqa, k_gqa, vt_gqa, xc, gc,
         kn_mla, kn_diff, kn_gqa, gt) = outs[:15]
        lam_init = 0.8 - 0.6 * math.exp(-0.3 * l)
        oa = _attn_call(qt_mla, k_mla, vt_mla, kn_mla, MAPS_MLA, "mla", name="attn_mla")
        ob = _attn_call(qt_diff, k_diff, vt_diff, kn_diff, MAPS_DIFF, "diff",
                        extra=(lw["diff_lambda"], lw["subln_col"]), lam_init=lam_init, name="attn_diff")
        od = _attn_call(qt_gqa, k_gqa, vt_gqa, kn_gqa, MAPS_GQA, "gqa", name="attn_gqa")
        oc = _lru_call(xc, lw)
        prev = (oa, ob, oc, od, gt, gc, gate, lw["w_out"])
    return _post_call(x, *prev, final_norm)


def kernel(x_prompt, x_sample, c_prompt, c_sample, ada_w, ada_b, norm_g, w_in, mla_q_norm, mla_w_uq, mla_kv_norm, mla_w_ukv, diff_lambda, diff_subln, lru_conv_w, lru_conv_b, lru_wa, lru_ba, lru_wx, lru_bx, lru_lambda, gqa_q_norm, gqa_k_norm, w_out, final_norm):
    p = dict(norm_g=norm_g, w_in=w_in, mla_q_norm=mla_q_norm, mla_w_uq=mla_w_uq, mla_kv_norm=mla_kv_norm,
             mla_w_ukv=mla_w_ukv, diff_lambda=diff_lambda, diff_subln=diff_subln,
             lru_conv_w=lru_conv_w, lru_conv_b=lru_conv_b, lru_wa=lru_wa, lru_ba=lru_ba,
             lru_wx=lru_wx, lru_bx=lru_bx, lru_lambda=lru_lambda,
             gqa_q_norm=gqa_q_norm, gqa_k_norm=gqa_k_norm, w_out=w_out)
    lws = [_layer_weights(l, p) for l in range(DEPTH)]
    nb = x_prompt.shape[0]
    mod = _modulation(jnp.concatenate([c_prompt, c_sample], axis=0), ada_w.astype(BF16), ada_b)

    def mods_for(lo, hi):
        out = []
        for l in range(DEPTH):
            m = mod[l, lo:hi]
            out.append(tuple(m[:, D_MODEL * j:D_MODEL * (j + 1)][:, None, :] for j in range(3)))
        return out

    fn = final_norm[None]
    y_prompt = _trunk(x_prompt, mods_for(0, nb), lws, fn)
    y_sample = _trunk(x_sample, mods_for(nb, nb + x_sample.shape[0]), lws, fn)
    return (y_prompt, y_sample)
```

```python
import functools
import math

import jax
import jax.numpy as jnp
import numpy as np
from jax import lax
from jax.experimental import pallas as pl
from jax.experimental.pallas import tpu as pltpu

F32 = jnp.float32
BF16 = jnp.bfloat16

D_MODEL = 1024
DEPTH = 2
W_GROUP = 256
H_A, NOPE_A, ROPE_A, V_A = 4, 64, 32, 64
Q_LORA, KV_LORA = 192, 128
H_B, DK_B, DV_B = 4, 32, 64
H_C, BW_C, CONV_W = 4, 64, 4
CONV_PAD_L = CONV_W // 2
LRU_C = 8.0
H_D, KV_H_D, HD_D = 4, 2, 64
DV = 64
assert V_A == DV_B == HD_D == DV
GRID_W = 64
ROPE_THETA = 500000.0
MLA_ROPE_THETA = 10000.0
AXIAL_THETA = 10000.0
EPS = 1e-6
LOG2E = 1.4426950408889634

IN_SIZES = (Q_LORA, KV_LORA, ROPE_A, W_GROUP,
            H_B * 2 * DK_B, H_B * 2 * DK_B, H_B * DV_B, W_GROUP,
            W_GROUP, W_GROUP,
            H_D * HD_D, KV_H_D * HD_D, KV_H_D * HD_D, W_GROUP)
_OFF = np.concatenate([[0], np.cumsum(IN_SIZES)]).tolist()
(O_QA, O_KVA, O_KRA, O_GA, O_QB, O_KB, O_VB, O_GB, O_XC, O_GC, O_QD, O_KD, O_VD, O_GD, _) = _OFF

LANES = 128
SUBLANES = 8

TOK_TILE = 512
POST_TILE = 1024
Q_TILE = 512
UNITS_PER_STEP = 128
LRU_CHUNK = 512
DENOM_LOG2_RANGE = 80
SUB_KEYS = 256
SCORE_LEAD = 2
VMEM_LIMIT = 56 * 1024 * 1024

ZS_KVL, ZS_KPE, ZS_KPE_SW, ZS_KB, ZS_KD, ZS_XC, ZS_G, ZS_END = (0, 128, 256, 384, 640, 768, 1024, 1280)
ZT_QA, ZT_QB, ZT_VB, ZT_QD, ZT_VD, ZT_G, ZT_END = (0, 192, 448, 704, 960, 1088, 1856)


def _partner(width, block, rot, half):
    idx = np.arange(width)
    d = idx % block
    out = idx.copy()
    in_rot = d < rot
    first = in_rot & ((d % (2 * half)) < half)
    second = in_rot & ~((d % (2 * half)) < half)
    out[first] = idx[first] + half
    out[second] = idx[second] - half
    return out


P_MLA = _partner(ROPE_A, ROPE_A, ROPE_A, ROPE_A // 2)


NORM_ROWS = 2 * SUBLANES


def _indicator(width, seg):
    m = np.zeros((NORM_ROWS, width), np.float32)
    m[np.arange(width) // seg, np.arange(width)] = 1.0
    return m


IND_MLA = _indicator(H_A * LANES, LANES)
IND_DIFF = _indicator(2 * H_B * DK_B, DK_B)
IND_GQA = _indicator(KV_H_D * HD_D, HD_D)


def _cparams(n_axes):
    return pltpu.CompilerParams(dimension_semantics=("arbitrary",) * n_axes,
                                vmem_limit_bytes=VMEM_LIMIT)


def _mod_kernel(c_ref, w_ref, b_ref, out_ref):
    c = c_ref[...]
    sc = (c * jax.nn.sigmoid(c)).astype(BF16)
    out_ref[0] = jnp.dot(sc, w_ref[0], preferred_element_type=F32) + b_ref[0]


def _modulation(c_all, ada_w_bf, ada_b):
    n = c_all.shape[0]
    return pl.pallas_call(
        _mod_kernel,
        out_shape=jax.ShapeDtypeStruct((DEPTH, n, 3 * D_MODEL), F32),
        grid=(DEPTH, 3),
        in_specs=[
            pl.BlockSpec((n, D_MODEL), lambda l, j: (0, 0)),
            pl.BlockSpec((1, D_MODEL, D_MODEL), lambda l, j: (l, 0, j)),
            pl.BlockSpec((1, 1, D_MODEL), lambda l, j: (l, 0, j)),
        ],
        out_specs=pl.BlockSpec((1, n, D_MODEL), lambda l, j: (l, 0, j)),
        compiler_params=_cparams(2),
        name="adaln_mod",
    )(c_all, ada_w_bf, ada_b.reshape(DEPTH, 1, 3 * D_MODEL))


def _pre_kernel(x_ref, shift_ref, scale_ref, ng_ref, wstd_ref, wt_ref, *rest):
    z, zt = _pre_project(x_ref[0], shift_ref, scale_ref, ng_ref, wstd_ref, wt_ref)
    _pre_finish(z, zt, *rest)


def _post_pre_kernel(x_ref, oa_ref, ob_ref, oc_ref, od_ref, gt_in_ref, gc_in_ref, gate_ref, wout_ref,
                     shift_ref, scale_ref, ng_ref, wstd_ref, wt_ref, *rest):
    xn = _post_update(x_ref, (oa_ref, ob_ref, oc_ref, od_ref), gt_in_ref, gc_in_ref, gate_ref, wout_ref)
    rest[-1][0] = xn
    z, zt = _pre_project(xn, shift_ref, scale_ref, ng_ref, wstd_ref, wt_ref)
    _pre_finish(z, zt, *rest[:-1])


def _pre_project(x, shift_ref, scale_ref, ng_ref, wstd_ref, wt_ref):
    ms = jnp.mean(x * x, axis=-1, keepdims=True)
    gain = ng_ref[...] * (1.0 + scale_ref[0])
    hb = ((x * lax.rsqrt(ms + EPS)) * gain + shift_ref[0]).astype(BF16)
    z = jnp.dot(hb, wstd_ref[...], preferred_element_type=F32)
    zt = lax.dot_general(wt_ref[...], hb, (((1,), (1,)), ((), ())),
                         preferred_element_type=F32)
    return z, zt


def _pre_finish(z, zt,
              gqa_col_ref, wuqT_ref, gkv_row_ref, wukp_ref, wuvT_ref,
              gq_col_ref, gk_row_ref,
              mkc_ref, mks_ref, mqc_ref, mqs_ref,
              dkc_ref, dks_ref, dqc_ref, dqs_ref,
              gkc_ref, gks_ref, gqc_ref, gqs_ref,
              ind_mla_ref, ind_diff_ref, ind_gqa_ref,
              qt_mla_ref, k_mla_ref, vt_mla_ref,
              qt_diff_ref, k_diff_ref, vt_diff_ref,
              qt_gqa_ref, k_gqa_ref, vt_gqa_ref,
              xc_ref, gc_ref, kn_mla_ref, kn_diff_ref, kn_gqa_ref, gt_ref):
    ts = z.shape[0]

    def key_norms(kb, ind_ref, out_ref):
        kf = kb.astype(F32)
        r = lax.dot_general(ind_ref[...], (kf * kf).astype(BF16), (((1,), (1,)), ((), ())),
                            preferred_element_type=F32)
        out_ref[0, 0] = jnp.broadcast_to(jnp.max(r, axis=1, keepdims=True), (NORM_ROWS, LANES))

    qlt = zt[ZT_QA:ZT_QA + Q_LORA]
    msq = jnp.mean(qlt * qlt, axis=0, keepdims=True)
    qnb = ((qlt * lax.rsqrt(msq + EPS)) * gqa_col_ref[...]).astype(BF16)
    qat = jnp.dot(wuqT_ref[...], qnb, preferred_element_type=F32)
    sc_mla = (NOPE_A + ROPE_A) ** -0.5 * LOG2E
    dq = NOPE_A + ROPE_A
    pad = jnp.zeros((LANES - dq, ts), F32)
    for hh in range(H_A):
        nope = qat[dq * hh:dq * hh + NOPE_A]
        pe = qat[dq * hh + NOPE_A:dq * (hh + 1)]
        sw = jnp.concatenate([pe[ROPE_A // 2:], pe[:ROPE_A // 2]], axis=0)
        pe = pe * mqc_ref[...] + sw * mqs_ref[...]
        blk = jnp.concatenate([nope, pe, pad], axis=0) * sc_mla
        qt_mla_ref[0, LANES * hh:LANES * (hh + 1), :] = blk.astype(BF16)

    kvl = z[:, ZS_KVL:ZS_KVL + KV_LORA]
    msk = jnp.mean(kvl * kvl, axis=-1, keepdims=True)
    kvn_f = (kvl * lax.rsqrt(msk + EPS)) * gkv_row_ref[...]
    kvn = kvn_f.astype(BF16)
    knope = jnp.dot(kvn, wukp_ref[...], preferred_element_type=F32)
    kpe = z[:, ZS_KPE:ZS_KPE + LANES] * mkc_ref[...] + z[:, ZS_KPE_SW:ZS_KPE_SW + LANES] * mks_ref[...]
    kb = jnp.concatenate([(knope[:, LANES * hh:LANES * (hh + 1)] + kpe).astype(BF16) for hh in range(H_A)],
                         axis=1)
    k_mla_ref[0] = kb
    key_norms(kb, ind_mla_ref, kn_mla_ref)
    vt_mla_ref[0, 0] = jnp.dot(wuvT_ref[...], kvn_f.T.astype(BF16),
                               preferred_element_type=F32).astype(BF16)

    kd = z[:, ZS_KB:ZS_KB + 256]
    lane_d = lax.broadcasted_iota(jnp.int32, (ts, 256), 1)
    half_d = DK_B // 8
    kd_sw = jnp.where(lane_d % (2 * half_d) < half_d, pltpu.roll(kd, 256 - half_d, 1), pltpu.roll(kd, half_d, 1))
    kb = (kd * dkc_ref[...] + kd_sw * dks_ref[...]).astype(BF16)
    k_diff_ref[0] = kb
    key_norms(kb, ind_diff_ref, kn_diff_ref)
    sc_diff = DK_B ** -0.5 * LOG2E
    rot = DK_B // 4
    for m in range(2 * H_B):
        q = zt[ZT_QB + DK_B * m:ZT_QB + DK_B * (m + 1)]
        q8 = q[0:rot]
        q8 = q8 * dqc_ref[0:rot, :] + pltpu.roll(q8, rot // 2, 0) * dqs_ref[0:rot, :]
        qq = jnp.concatenate([q8, q[rot:]], axis=0) * sc_diff
        qt_diff_ref[0, DK_B * m:DK_B * (m + 1), :] = qq.astype(BF16)
    vt_diff_ref[0, 0] = zt[ZT_VB:ZT_VB + 256].astype(BF16)

    kg = z[:, ZS_KD:ZS_KD + LANES]
    k2 = kg * kg
    lo = lax.broadcasted_iota(jnp.int32, (ts, LANES), 1) < HD_D
    s_lo = jnp.sum(jnp.where(lo, k2, 0.0), axis=-1, keepdims=True)
    s_hi = jnp.sum(jnp.where(lo, 0.0, k2), axis=-1, keepdims=True)
    rk = lax.rsqrt(jnp.where(lo, s_lo, s_hi) / HD_D + EPS)
    kn = (kg * rk) * gk_row_ref[...]
    half_g = HD_D // 4
    lane_g = lax.broadcasted_iota(jnp.int32, (ts, LANES), 1)
    knsw = jnp.where(lane_g % (2 * half_g) < half_g, pltpu.roll(kn, LANES - half_g, 1), pltpu.roll(kn, half_g, 1))
    kb = (kn * gkc_ref[...] + knsw * gks_ref[...]).astype(BF16)
    k_gqa_ref[0] = kb
    key_norms(kb, ind_gqa_ref, kn_gqa_ref)
    sc_gqa = HD_D ** -0.5 * LOG2E
    for hh in range(H_D):
        q = zt[ZT_QD + HD_D * hh:ZT_QD + HD_D * (hh + 1)]
        r = lax.rsqrt(jnp.mean(q * q, axis=0, keepdims=True) + EPS)
        qn = (q * r) * gq_col_ref[...]
        qsw = jnp.concatenate([qn[half_g:2 * half_g], qn[0:half_g], qn[3 * half_g:], qn[2 * half_g:3 * half_g]],
                              axis=0)
        qq = (qn * gqc_ref[...] + qsw * gqs_ref[...]) * sc_gqa
        qt_gqa_ref[0, HD_D * hh:HD_D * (hh + 1), :] = qq.astype(BF16)
    vt_gqa_ref[0, 0] = zt[ZT_VD:ZT_VD + 128].astype(BF16)

    xc_ref[0] = z[:, ZS_XC:ZS_XC + W_GROUP]
    g = z[:, ZS_G:ZS_END]
    gc_ref[0] = (g * jax.nn.sigmoid(g)).astype(BF16)
    g = zt[ZT_G:ZT_END]
    gt_ref[0] = (g * jax.nn.sigmoid(g)).astype(BF16)


def _pre_call(x, shift, scale, lw, tb, prev=None):
    b, s, d = x.shape
    ts = TOK_TILE
    nt = s // ts
    full2 = lambda a: pl.BlockSpec(a.shape, lambda i, bb: (0, 0))
    tok_rows = lambda w: pl.BlockSpec((ts, w), lambda i, bb: (i, 0))
    tok_cols = lambda r: pl.BlockSpec((r, ts), lambda i, bb: (0, i))
    tok3 = lambda w: pl.BlockSpec((1, ts, w), lambda i, bb: (bb, i, 0))
    vec3 = pl.BlockSpec((1, 1, d), lambda i, bb: (bb, 0, 0))
    params = [lw["norm_g"], lw["w_std"], lw["w_t"],
              lw["gqa_col"], lw["w_uqT"], lw["gkv_row"], lw["w_ukp"], lw["w_uvT"],
              lw["gq_col"], lw["gk_row"]]
    in_specs = [tok3(d)]
    head = [x]
    if prev is not None:
        feat3 = lambda r: pl.BlockSpec((1, r, ts), lambda i, bb: (bb, 0, i))
        in_specs += [feat3(W_GROUP), feat3(W_GROUP), tok3(W_GROUP), feat3(W_GROUP),
                     feat3(3 * W_GROUP), tok3(W_GROUP), vec3, full2(prev[7])]
        head += list(prev)
    in_specs += [vec3, vec3]
    in_specs += [full2(p) for p in params]
    in_specs += [tok_rows(128), tok_rows(128), tok_cols(32), tok_cols(32),
                 tok_rows(256), tok_rows(256), tok_cols(32), tok_cols(32),
                 tok_rows(128), tok_rows(128), tok_cols(64), tok_cols(64)]
    tables = [tb["mkc"], tb["mks"], tb["mqc"], tb["mqs"],
              tb["dkc"], tb["dks"], tb["dqc"], tb["dqs"],
              tb["gkc"], tb["gks"], tb["gqc"], tb["gqs"]]
    inds = [jnp.asarray(IND_MLA, BF16), jnp.asarray(IND_DIFF, BF16), jnp.asarray(IND_GQA, BF16)]
    in_specs += [full2(a) for a in inds]
    kn_shape = jax.ShapeDtypeStruct((b, nt, NORM_ROWS, LANES), F32)
    kn_spec = pl.BlockSpec((1, 1, NORM_ROWS, LANES), lambda i, bb: (bb, i, 0, 0))

    def qt_spec(rows):
        return pl.BlockSpec((1, rows, ts), lambda i, bb: (bb, 0, i))

    def k_spec(w):
        return pl.BlockSpec((1, ts, w), lambda i, bb: (bb, i, 0))

    def vt_spec(rows):
        return pl.BlockSpec((1, 1, rows, ts), lambda i, bb: (bb, i, 0, 0))

    out_shape = [
        jax.ShapeDtypeStruct((b, 512, s), BF16), jax.ShapeDtypeStruct((b, s, 512), BF16),
        jax.ShapeDtypeStruct((b, nt, 256, ts), BF16),
        jax.ShapeDtypeStruct((b, 256, s), BF16), jax.ShapeDtypeStruct((b, s, 256), BF16),
        jax.ShapeDtypeStruct((b, nt, 256, ts), BF16),
        jax.ShapeDtypeStruct((b, 256, s), BF16), jax.ShapeDtypeStruct((b, s, 128), BF16),
        jax.ShapeDtypeStruct((b, nt, 128, ts), BF16),
        jax.ShapeDtypeStruct((b, s, 256), F32), jax.ShapeDtypeStruct((b, s, 256), BF16),
        kn_shape, kn_shape, kn_shape, jax.ShapeDtypeStruct((b, 3 * W_GROUP, s), BF16),
    ]
    out_specs = [qt_spec(512), k_spec(512), vt_spec(256),
                 qt_spec(256), k_spec(256), vt_spec(256),
                 qt_spec(256), k_spec(128), vt_spec(128),
                 k_spec(256), k_spec(256), kn_spec, kn_spec, kn_spec, qt_spec(3 * W_GROUP)]
    if prev is not None:
        out_shape.append(jax.ShapeDtypeStruct((b, s, d), F32))
        out_specs.append(tok3(d))
    return pl.pallas_call(
        _pre_kernel if prev is None else _post_pre_kernel,
        out_shape=out_shape, grid=(nt, b), in_specs=in_specs, out_specs=out_specs,
        compiler_params=_cparams(2), name="pre_proj" if prev is None else "post_pre_proj",
    )(*head, shift, scale, *params, *tables, *inds)


def _attn_kernel(*refs, maps, n_chunks, tk, tq, kind, lam_init):
    if kind == "diff":
        qt_ref, k_ref, vt_ref, kn_ref, lam_ref, subln_ref, out_ref, w_scr = refs
    else:
        qt_ref, k_ref, vt_ref, kn_ref, out_ref, w_scr = refs
        lam_ref = subln_ref = None
    n_tiles = qt_ref.shape[2] // tq
    dv = DV
    nm = len(maps)
    finish = functools.partial(_attn_finish, out_ref=out_ref, lam_ref=lam_ref, subln_ref=subln_ref,
                               kind=kind, lam_init=lam_init, tq=tq)

    for t in range(n_tiles):
        for i, (kg, q0, qn, woff, v0, _) in enumerate(maps):
            q = qt_ref[0, q0:q0 + qn, tq * t:tq * (t + 1)]
            pieces = []
            if woff > 0:
                pieces.append(jnp.zeros((woff, tq), BF16))
            pieces.append(q)
            if LANES - woff - qn > 0:
                pieces.append(jnp.zeros((LANES - woff - qn, tq), BF16))
            w_scr[t, i] = pieces[0] if len(pieces) == 1 else jnp.concatenate(pieces, axis=0)

    def scores(t, i, c, r1=0, rows=tk):
        kg = maps[i][0]
        r0 = c * tk + r1 if isinstance(c, int) else pl.multiple_of(c * tk, tk)
        return jnp.dot(k_ref[0, pl.ds(r0, rows), LANES * kg:LANES * (kg + 1)], w_scr[t, i],
                       preferred_element_type=F32)

    def pv(i, c, p, r1=0, rows=tk):
        v0 = maps[i][4]
        return jnp.dot(vt_ref[0, c, v0:v0 + dv, r1:r1 + rows], p.astype(BF16),
                       preferred_element_type=F32)

    kn2 = jnp.max(kn_ref[0], axis=0)
    kmax = [jnp.sqrt(kn2[maps[i][5]:maps[i][5] + 1, 0:1]) for i in range(nm)]
    bounds = []
    for t in range(n_tiles):
        for i in range(nm):
            w = w_scr[t, i].astype(F32)
            bounds.append(jnp.sqrt(jnp.sum(w * w, axis=0, keepdims=True)) * kmax[i])
    units = [(t, i, c, r1) for t in range(n_tiles) for c in range(n_chunks)
             for r1 in range(0, tk, SUB_KEYS) for i in range(nm)]
    per_tile = len(units) // n_tiles
    accs, ls = [None] * (n_tiles * nm), [None] * (n_tiles * nm)
    pending = []
    lo, hi = 2.0 ** -DENOM_LOG2_RANGE, 2.0 ** DENOM_LOG2_RANGE
    for n in range(len(units) + SCORE_LEAD):
        if n < len(units):
            t, i, c, r1 = units[n]
            pending.append(scores(t, i, c, r1, SUB_KEYS))
        if n >= SCORE_LEAD:
            t, i, c, r1 = units[n - SCORE_LEAD]
            j = t * nm + i
            p = jnp.exp2(pending.pop(0) - bounds[j])
            part, psum = pv(i, c, p, r1, SUB_KEYS), jnp.sum(p, axis=0, keepdims=True)
            accs[j] = part if accs[j] is None else accs[j] + part
            ls[j] = psum if ls[j] is None else ls[j] + psum
            if (n - SCORE_LEAD + 1) % per_tile == 0:
                finish(t, accs[t * nm:(t + 1) * nm], ls[t * nm:(t + 1) * nm])

    for t in range(n_tiles):
        bad = None
        for l in ls[t * nm:(t + 1) * nm]:
            b = jnp.where((l > lo) & (l < hi), 0.0, 1.0)
            bad = b if bad is None else jnp.maximum(bad, b)

        @pl.when(jnp.max(bad) > 0.0)
        def _(t=t):
            def safe_body(c, carry):
                new = []
                for i in range(nm):
                    m, l, acc = carry[i]
                    s = scores(t, i, c)
                    mn = jnp.maximum(m, jnp.max(s, axis=0, keepdims=True))
                    p, alpha = jnp.exp2(s - mn), jnp.exp2(m - mn)
                    new.append((mn, alpha * l + jnp.sum(p, axis=0, keepdims=True), alpha * acc + pv(i, c, p)))
                return tuple(new)

            init = tuple((jnp.full((1, tq), -jnp.inf, F32), jnp.zeros((1, tq), F32), jnp.zeros((dv, tq), F32))
                         for _ in range(nm))
            fs = lax.fori_loop(0, n_chunks, safe_body, init)
            finish(t, [f[2] for f in fs], [f[1] for f in fs])


def _attn_finish(t, accs, ls, out_ref, lam_ref, subln_ref, kind, lam_init, tq):
    os_ = [acc / l for acc, l in zip(accs, ls)]

    if kind == "diff":
        lf = lam_ref[...]
        s1 = jnp.sum(lf[0:1] * lf[1:2], axis=-1, keepdims=True)
        s2 = jnp.sum(lf[2:3] * lf[3:4], axis=-1, keepdims=True)
        lam_full = jnp.exp(s1) - jnp.exp(s2) + lam_init
        heads = []
        for hh in range(H_B):
            o = os_[2 * hh] - lam_full * os_[2 * hh + 1]
            r = lax.rsqrt(jnp.mean(o * o, axis=0, keepdims=True) + EPS)
            heads.append(((o * r) * subln_ref[...]) * (1.0 - lam_init))
        ot = jnp.concatenate(heads, axis=0)
    else:
        ot = jnp.concatenate(os_, axis=0)
    out_ref[0, :, tq * t:tq * (t + 1)] = ot.astype(out_ref.dtype)


def _attn_call(qt, k, vt, kn, maps, kind, extra=(), lam_init=0.0, name="attn"):
    b, fq, s = qt.shape
    fk = k.shape[2]
    _, nt, fv, tk = vt.shape
    tq = Q_TILE
    n_tiles = 1 if kind == "diff" else max(1, min(s // tq, (UNITS_PER_STEP * tk) // (len(maps) * s)))
    tqs = tq * n_tiles
    in_specs = [pl.BlockSpec((1, fq, tqs), lambda bb, qi: (bb, 0, qi)),
                pl.BlockSpec((1, s, fk), lambda bb, qi: (bb, 0, 0)),
                pl.BlockSpec((1, nt, fv, tk), lambda bb, qi: (bb, 0, 0, 0)),
                pl.BlockSpec((1, nt, NORM_ROWS, LANES), lambda bb, qi: (bb, 0, 0, 0))]
    in_specs += [pl.BlockSpec(e.shape, lambda bb, qi: (0, 0)) for e in extra]
    kern = functools.partial(_attn_kernel, maps=tuple(maps), n_chunks=nt, tk=tk, tq=tq, kind=kind,
                             lam_init=lam_init)
    return pl.pallas_call(
        kern, out_shape=jax.ShapeDtypeStruct((b, W_GROUP, s), BF16),
        grid=(b, s // tqs), in_specs=in_specs,
        out_specs=pl.BlockSpec((1, W_GROUP, tqs), lambda bb, qi: (bb, 0, qi)),
        scratch_shapes=[pltpu.VMEM((n_tiles, len(maps), LANES, tq), BF16)],
        compiler_params=_cparams(2), name=name,
    )(qt, k, vt, kn, *extra)


MAPS_MLA = [(hh, LANES * hh, LANES, 0, V_A * hh, hh) for hh in range(H_A)]
MAPS_DIFF = [(m // 4, DK_B * m, DK_B, DK_B * (m % 4), DV_B * (m // 2), m) for m in range(2 * H_B)]
MAPS_GQA = [(0, HD_D * hh, HD_D, HD_D * (hh // (H_D // KV_H_D)), HD_D * (hh // (H_D // KV_H_D)),
             hh // (H_D // KV_H_D)) for hh in range(H_D)]


def _softplus(x):
    return jnp.maximum(x, 0.0) + jnp.log1p(jnp.exp(-jnp.abs(x)))


def _lru_kernel(xc_ref, cw_ref, cb_ref, wbd_ref, bias_ref, lam_ref, out_ref,
                xpad, af, gf, ar, gr, hfl, pfl, hrl, prl, *, s, tc):
    tseg = s // SUBLANES
    tl_chunk = tc // SUBLANES
    xpad[0:tseg, :] = jnp.zeros((tseg, LANES), F32)
    xpad[tseg + s:2 * tseg + s, :] = jnp.zeros((tseg, LANES), F32)
    xpad[tseg:tseg + s, :] = xc_ref[0]
    lam = lam_ref[0]
    sp_f = _softplus(-lam[0:1])
    sp_r = _softplus(-lam[1:2])
    cw = cw_ref[...]
    cb = cb_ref[...]
    wbd = wbd_ref[0]
    bias = bias_ref[0]

    def gates(half_pre_r, half_pre_i, sp, half_xconv):
        log_a = (jnp.tanh(half_pre_r) + 1.0) * ((-0.5 * LRU_C) * sp)
        a = jnp.exp(log_a)
        v = -jnp.tanh(log_a) * (a * a + 1.0)
        gx = jnp.where(v > 0.0, v * lax.rsqrt(v), 0.0) * ((jnp.tanh(half_pre_i) + 1.0) * half_xconv)
        return a, gx

    def chunk(ci, carry):
        tl0 = ci * tl_chunk
        tiles = [xpad[pl.ds(tseg + tl0 + k, SUBLANES, stride=tseg), :]
                 for k in range(-CONV_PAD_L, tl_chunk + CONV_W - 1 - CONV_PAD_L)]
        xconv = cb
        for j in range(CONV_W):
            xconv = xconv + jnp.concatenate(tiles[j:j + tl_chunk], axis=0) * cw[j:j + 1]
        pre = jnp.dot(xconv.astype(BF16), wbd, preferred_element_type=F32) + bias
        half_xconv = 0.5 * xconv
        r0 = pl.multiple_of(ci * tc, tc)
        a, gx = gates(pre[:, 0:128], pre[:, 128:256], sp_f, half_xconv)
        af[pl.ds(r0, tc), :] = a
        gf[pl.ds(r0, tc), :] = gx
        a, gx = gates(pre[:, 256:384], pre[:, 384:512], sp_r, half_xconv)
        ar[pl.ds(r0, tc), :] = a
        gr[pl.ds(r0, tc), :] = gx
        return carry

    lax.fori_loop(0, s // tc, chunk, 0)

    def scan_body(t, carry):
        hf, pf, hr, pr = carry
        r0 = pl.multiple_of(t * SUBLANES, SUBLANES)
        a = af[pl.ds(r0, SUBLANES), :]
        hf = a * hf + gf[pl.ds(r0, SUBLANES), :]
        pf = a * pf
        hfl[pl.ds(r0, SUBLANES), :] = hf
        pfl[pl.ds(r0, SUBLANES), :] = pf
        r1 = pl.multiple_of((tseg - 1 - t) * SUBLANES, SUBLANES)
        a = ar[pl.ds(r1, SUBLANES), :]
        hr = a * hr + gr[pl.ds(r1, SUBLANES), :]
        pr = a * pr
        hrl[pl.ds(r1, SUBLANES), :] = hr
        prl[pl.ds(r1, SUBLANES), :] = pr
        return hf, pf, hr, pr

    zero8 = jnp.zeros((SUBLANES, LANES), F32)
    one8 = jnp.ones((SUBLANES, LANES), F32)
    h_end, p_end, h_start, p_start = lax.fori_loop(0, tseg, scan_body, (zero8, one8, zero8, one8), unroll=8)

    row = lax.broadcasted_iota(jnp.int32, (SUBLANES, LANES), 0)
    cf, cr = zero8, zero8
    for _ in range(SUBLANES - 1):
        cf = jnp.where(row >= 1, pltpu.roll(h_end + p_end * cf, 1, 0), 0.0)
        cr = jnp.where(row < SUBLANES - 1, pltpu.roll(h_start + p_start * cr, SUBLANES - 1, 0), 0.0)

    ho = af

    def patch(ci, carry):
        r0 = pl.multiple_of(ci * tc, tc)
        cft = jnp.concatenate([cf] * tl_chunk, axis=0)
        crt = jnp.concatenate([cr] * tl_chunk, axis=0)
        ho[pl.ds(r0, tc), :] = (hfl[pl.ds(r0, tc), :] + pfl[pl.ds(r0, tc), :] * cft) + (
            hrl[pl.ds(r0, tc), :] + prl[pl.ds(r0, tc), :] * crt)
        return carry

    lax.fori_loop(0, s // tc, patch, 0)

    def untangle(u, carry):
        for j in range(SUBLANES):
            base = u * (2 * SUBLANES * SUBLANES) + j
            blk = jnp.concatenate([ho[pl.ds(base, SUBLANES, stride=SUBLANES), :],
                                   ho[pl.ds(base + SUBLANES * SUBLANES, SUBLANES, stride=SUBLANES), :]], axis=0)
            out_ref[0, pl.ds(pl.multiple_of(j * tseg + u * 2 * SUBLANES, 2 * SUBLANES), 2 * SUBLANES), :] = (
                blk.astype(out_ref.dtype))
        return carry

    lax.fori_loop(0, tseg // (2 * SUBLANES), untangle, 0)


def _lru_call(xc, lw):
    b, s, _ = xc.shape
    kern = functools.partial(_lru_kernel, s=s, tc=LRU_CHUNK)
    seq = pltpu.VMEM((s, LANES), F32)
    return pl.pallas_call(
        kern, out_shape=jax.ShapeDtypeStruct((b, s, W_GROUP), BF16),
        grid=(b, W_GROUP // LANES),
        in_specs=[pl.BlockSpec((1, s, LANES), lambda bb, hf: (bb, 0, hf)),
                  pl.BlockSpec((CONV_W, LANES), lambda bb, hf: (0, hf)),
                  pl.BlockSpec((1, LANES), lambda bb, hf: (0, hf)),
                  pl.BlockSpec((1, LANES, 4 * LANES), lambda bb, hf: (hf, 0, 0)),
                  pl.BlockSpec((1, 1, 4 * LANES), lambda bb, hf: (hf, 0, 0)),
                  pl.BlockSpec((1, 2, LANES), lambda bb, hf: (hf, 0, 0))],
        out_specs=pl.BlockSpec((1, s, LANES), lambda bb, hf: (bb, 0, hf)),
        scratch_shapes=[pltpu.VMEM((s + 2 * (s // SUBLANES), LANES), F32)] + [seq] * 8,
        compiler_params=_cparams(2), name="rglru",
    )(xc, lw["conv_w"], lw["conv_b"], lw["lru_wbd"], lw["lru_bias"], lw["lru_lam"])


def _post_update(x_ref, o_refs, gt_ref, gc_ref, gate_ref, wout_ref):
    y = None
    row = 0
    for i, o_ref in enumerate(o_refs):
        w_blk = wout_ref[W_GROUP * i:W_GROUP * (i + 1), :]
        if i == 2:
            og = (o_ref[0].astype(F32) * gc_ref[0].astype(F32)).astype(BF16)
            part = jnp.dot(og, w_blk, preferred_element_type=F32)
        else:
            og = (o_ref[0].astype(F32) * gt_ref[0, row:row + W_GROUP, :].astype(F32)).astype(BF16)
            row += W_GROUP
            part = lax.dot_general(og, w_blk, (((0,), (0,)), ((), ())), preferred_element_type=F32)
        y = part if y is None else y + part
    return x_ref[0] + gate_ref[0] * y


def _post_kernel(x_ref, oa_ref, ob_ref, oc_ref, od_ref, gt_ref, gc_ref, gate_ref, wout_ref, fn_ref, out_ref):
    xn = _post_update(x_ref, (oa_ref, ob_ref, oc_ref, od_ref), gt_ref, gc_ref, gate_ref, wout_ref)
    ms = jnp.mean(xn * xn, axis=-1, keepdims=True)
    out_ref[0] = (xn * lax.rsqrt(ms + EPS)) * fn_ref[...]


def _post_call(x, oa, ob, oc, od, gt, gc, gate, w_out_bf, final_norm):
    b, s, d = x.shape
    ts = POST_TILE
    tok = lambda w: pl.BlockSpec((1, ts, w), lambda bb, i: (bb, i, 0))
    feat = lambda r: pl.BlockSpec((1, r, ts), lambda bb, i: (bb, 0, i))
    return pl.pallas_call(
        _post_kernel,
        out_shape=jax.ShapeDtypeStruct((b, s, d), F32),
        grid=(b, s // ts),
        in_specs=[tok(d), feat(W_GROUP), feat(W_GROUP), tok(W_GROUP), feat(W_GROUP), feat(3 * W_GROUP), tok(W_GROUP),
                  pl.BlockSpec((1, 1, d), lambda bb, i: (bb, 0, 0)),
                  pl.BlockSpec((d, d), lambda bb, i: (0, 0)),
                  pl.BlockSpec((1, d), lambda bb, i: (0, 0))],
        out_specs=tok(d),
        compiler_params=_cparams(2), name="post_proj",
    )(x, oa, ob, oc, od, gt, gc, gate, w_out_bf, final_norm)


def _angles(pos, half, theta):
    inv = jnp.power(jnp.float32(theta), -jnp.arange(half, dtype=F32) / half)
    return pos[:, None] * inv[None, :]


def _tables(s):
    pos = jnp.arange(s, dtype=F32)
    rows = s // GRID_W
    row = jnp.repeat(jnp.arange(rows, dtype=F32), GRID_W)
    col = jnp.tile(jnp.arange(GRID_W, dtype=F32), rows)
    a = _angles(pos, ROPE_A // 2, MLA_ROPE_THETA)
    c, sn = jnp.cos(a), jnp.sin(a)
    z64, z32 = jnp.zeros((s, 64), F32), jnp.zeros((s, 32), F32)
    tb = {"mkc": jnp.concatenate([z64, c, c, z32], axis=1),
          "mks": jnp.concatenate([z64, -sn, sn, z32], axis=1),
          "mqc": jnp.concatenate([c, c], axis=1).T,
          "mqs": jnp.concatenate([-sn, sn], axis=1).T}
    a = _angles(pos, DK_B // 8, ROPE_THETA)
    c, sn = jnp.cos(a), jnp.sin(a)
    c32 = jnp.concatenate([c, c, jnp.ones((s, DK_B - DK_B // 4), F32)], axis=1)
    s32 = jnp.concatenate([-sn, sn, jnp.zeros((s, DK_B - DK_B // 4), F32)], axis=1)
    tb.update({"dkc": jnp.tile(c32, (1, 2 * H_B)), "dks": jnp.tile(s32, (1, 2 * H_B)),
               "dqc": c32.T, "dqs": s32.T})
    ar_ = _angles(row, HD_D // 4, AXIAL_THETA)
    ac_ = _angles(col, HD_D // 4, AXIAL_THETA)
    cr, sr, cc, sc = jnp.cos(ar_), jnp.sin(ar_), jnp.cos(ac_), jnp.sin(ac_)
    c64 = jnp.concatenate([cr, cr, cc, cc], axis=1)
    s64 = jnp.concatenate([-sr, sr, -sc, sc], axis=1)
    tb.update({"gkc": jnp.tile(c64, (1, KV_H_D)), "gks": jnp.tile(s64, (1, KV_H_D)),
               "gqc": c64.T, "gqs": s64.T})
    return tb


def _blockdiag_pair(w, half):
    z = jnp.zeros((BW_C, BW_C), F32)
    top = jnp.concatenate([w[2 * half], z], axis=1)
    bot = jnp.concatenate([z, w[2 * half + 1]], axis=1)
    return jnp.concatenate([top, bot], axis=0)


def _layer_weights(l, p):
    w = p["w_in"][l]
    col = lambda o, n: w[:, o:o + n]
    z64 = jnp.zeros((D_MODEL, 64), F32)
    z32 = jnp.zeros((D_MODEL, 32), F32)
    kra = col(O_KRA, ROPE_A)
    w_std = jnp.concatenate([
        col(O_KVA, KV_LORA),
        z64, kra, z32,
        z64, kra[:, P_MLA], z32,
        col(O_KB, 256), col(O_KD, 128),
        col(O_XC, W_GROUP), col(O_GC, W_GROUP)], axis=1)
    w_t = jnp.concatenate([
        col(O_QA, Q_LORA),
        col(O_QB, 256), col(O_VB, 256),
        col(O_QD, 256), col(O_VD, 128),
        col(O_GA, W_GROUP), col(O_GB, W_GROUP), col(O_GD, W_GROUP)], axis=1).T
    wuq = p["mla_w_uq"][l]
    wukv = p["mla_w_ukv"][l]
    dkv = NOPE_A + V_A
    z_k = jnp.zeros((KV_LORA, LANES - NOPE_A), F32)
    w_ukp = jnp.concatenate(
        [jnp.concatenate([wukv[:, dkv * h:dkv * h + NOPE_A], z_k], axis=1) for h in range(H_A)], axis=1)
    w_uv = jnp.concatenate([wukv[:, dkv * h + NOPE_A:dkv * (h + 1)] for h in range(H_A)], axis=1)
    gq, gk = p["gqa_q_norm"][l], p["gqa_k_norm"][l]
    wa, wx = p["lru_wa"][l], p["lru_wx"][l]
    ba, bx = p["lru_ba"][l], p["lru_bx"][l]
    wbd = jnp.stack([jnp.concatenate([_blockdiag_pair(wa[0], hf), _blockdiag_pair(wx[0], hf),
                                      _blockdiag_pair(wa[1], hf), _blockdiag_pair(wx[1], hf)], axis=1)
                     for hf in range(2)])
    sl = lambda v, hf: v[LANES * hf:LANES * (hf + 1)]
    bias = jnp.stack([jnp.concatenate([sl(ba[0], hf), sl(bx[0], hf), sl(ba[1], hf), sl(bx[1], hf)])[None]
                      for hf in range(2)])
    lam = p["lru_lambda"][l]
    lam_h = jnp.stack([jnp.stack([sl(lam[0], hf), sl(lam[1], hf)]) for hf in range(2)])
    return {
        "norm_g": p["norm_g"][l][None], "w_std": w_std.astype(BF16), "w_t": w_t.astype(BF16),
        "gqa_col": p["mla_q_norm"][l][:, None], "w_uqT": wuq.T.astype(BF16),
        "gkv_row": p["mla_kv_norm"][l][None],
        "w_ukp": w_ukp.astype(BF16), "w_uvT": w_uv.T.astype(BF16),
        "gq_col": gq[:, None], "gk_row": jnp.tile(gk, KV_H_D)[None],
        "diff_lambda": p["diff_lambda"][l], "subln_col": p["diff_subln"][l][:, None],
        "conv_w": p["lru_conv_w"][l], "conv_b": p["lru_conv_b"][l][None],
        "lru_wbd": (0.5 * wbd).astype(BF16), "lru_bias": 0.5 * bias, "lru_lam": lam_h,
        "w_out": p["w_out"][l].astype(BF16),
    }


def _trunk(x, mods, lws, final_norm):
    b, s, d = x.shape
    tb = _tables(s)
    prev = None
    for l in range(DEPTH):
        lw = lws[l]
        shift, scale, gate = mods[l]
        outs = _pre_call(x, shift, scale, lw, tb, prev)
        if prev is not None:
            x = outs[-1]
        (qt_mla, k_mla, vt_mla, qt_diff, k_diff, vt_diff, qt_gqa, k_gqa, vt_gqa, xc, gc,
         kn_mla, kn_diff, kn_gqa, gt) = outs[:15]
        lam_init = 0.8 - 0.6 * math.exp(-0.3 * l)
        oa = _attn_call(qt_mla, k_mla, vt_mla, kn_mla, MAPS_MLA, "mla", name="attn_mla")
        ob = _attn_call(qt_diff, k_diff, vt_diff, kn_diff, MAPS_DIFF, "diff",
                        extra=(lw["diff_lambda"], lw["subln_col"]), lam_init=lam_init, name="attn_diff")
        od = _attn_call(qt_gqa, k_gqa, vt_gqa, kn_gqa, MAPS_GQA, "gqa", name="attn_gqa")
        oc = _lru_call(xc, lw)
        prev = (oa, ob, oc, od, gt, gc, gate, lw["w_out"])
    return _post_call(x, *prev, final_norm)


def kernel(x_prompt, x_sample, c_prompt, c_sample, ada_w, ada_b, norm_g, w_in, mla_q_norm, mla_w_uq, mla_kv_norm, mla_w_ukv, diff_lambda, diff_subln, lru_conv_w, lru_conv_b, lru_wa, lru_ba, lru_wx, lru_bx, lru_lambda, gqa_q_norm, gqa_k_norm, w_out, final_norm):
    p = dict(norm_g=norm_g, w_in=w_in, mla_q_norm=mla_q_norm, mla_w_uq=mla_w_uq, mla_kv_norm=mla_kv_norm,
             mla_w_ukv=mla_w_ukv, diff_lambda=diff_lambda, diff_subln=diff_subln,
             lru_conv_w=lru_conv_w, lru_conv_b=lru_conv_b, lru_wa=lru_wa, lru_ba=lru_ba,
             lru_wx=lru_wx, lru_bx=lru_bx, lru_lambda=lru_lambda,
             gqa_q_norm=gqa_q_norm, gqa_k_norm=gqa_k_norm, w_out=w_out)
    lws = [_layer_weights(l, p) for l in range(DEPTH)]
    nb = x_prompt.shape[0]
    mod = _modulation(jnp.concatenate([c_prompt, c_sample], axis=0), ada_w.astype(BF16), ada_b)

    def mods_for(lo, hi):
        out = []
        for l in range(DEPTH):
            m = mod[l, lo:hi]
            out.append(tuple(m[:, D_MODEL * j:D_MODEL * (j + 1)][:, None, :] for j in range(3)))
        return out

    fn = final_norm[None]
    y_prompt = _trunk(x_prompt, mods_for(0, nb), lws, fn)
    y_sample = _trunk(x_sample, mods_for(nb, nb + x_sample.shape[0]), lws, fn)
    return (y_prompt, y_sample)
```

```python
import functools
import math

import jax
import jax.numpy as jnp
import numpy as np
from jax import lax
from jax.experimental import pallas as pl
from jax.experimental.pallas import tpu as pltpu

F32 = jnp.float32
BF16 = jnp.bfloat16

D_MODEL = 1024
DEPTH = 2
W_GROUP = 256
H_A, NOPE_A, ROPE_A, V_A = 4, 64, 32, 64
Q_LORA, KV_LORA = 192, 128
H_B, DK_B, DV_B = 4, 32, 64
H_C, BW_C, CONV_W = 4, 64, 4
CONV_PAD_L = CONV_W // 2
LRU_C = 8.0
H_D, KV_H_D, HD_D = 4, 2, 64
DV = 64
assert V_A == DV_B == HD_D == DV
GRID_W = 64
ROPE_THETA = 500000.0
MLA_ROPE_THETA = 10000.0
AXIAL_THETA = 10000.0
EPS = 1e-6
LOG2E = 1.4426950408889634

IN_SIZES = (Q_LORA, KV_LORA, ROPE_A, W_GROUP,
            H_B * 2 * DK_B, H_B * 2 * DK_B, H_B * DV_B, W_GROUP,
            W_GROUP, W_GROUP,
            H_D * HD_D, KV_H_D * HD_D, KV_H_D * HD_D, W_GROUP)
_OFF = np.concatenate([[0], np.cumsum(IN_SIZES)]).tolist()
(O_QA, O_KVA, O_KRA, O_GA, O_QB, O_KB, O_VB, O_GB, O_XC, O_GC, O_QD, O_KD, O_VD, O_GD, _) = _OFF

LANES = 128
SUBLANES = 8

TOK_TILE = 512
POST_TILE = 1024
Q_TILE = 512
UNITS_PER_STEP = 64
LRU_CHUNK = 1024
DENOM_LOG2_RANGE = 80
SUB_KEYS = 256
SCORE_LEAD = 2
VMEM_LIMIT = 56 * 1024 * 1024

ZS_KVL, ZS_KPE, ZS_KPE_SW, ZS_KB, ZS_KD, ZS_XC, ZS_G, ZS_END = (0, 128, 256, 384, 640, 768, 1024, 1280)
ZT_QA, ZT_QB, ZT_VB, ZT_QD, ZT_VD, ZT_G, ZT_END = (0, 192, 448, 704, 960, 1088, 1856)


def _partner(width, block, rot, half):
    idx = np.arange(width)
    d = idx % block
    out = idx.copy()
    in_rot = d < rot
    first = in_rot & ((d % (2 * half)) < half)
    second = in_rot & ~((d % (2 * half)) < half)
    out[first] = idx[first] + half
    out[second] = idx[second] - half
    return out


P_MLA = _partner(ROPE_A, ROPE_A, ROPE_A, ROPE_A // 2)


NORM_ROWS = 2 * SUBLANES


def _indicator(width, seg):
    m = np.zeros((NORM_ROWS, width), np.float32)
    m[np.arange(width) // seg, np.arange(width)] = 1.0
    return m


IND_MLA = _indicator(H_A * LANES, LANES)
IND_DIFF = _indicator(2 * H_B * DK_B, DK_B)
IND_GQA = _indicator(KV_H_D * HD_D, HD_D)


def _cparams(n_axes):
    return pltpu.CompilerParams(dimension_semantics=("arbitrary",) * n_axes,
                                vmem_limit_bytes=VMEM_LIMIT)


def _mod_kernel(c_ref, w_ref, b_ref, out_ref):
    c = c_ref[...]
    sc = (c * jax.nn.sigmoid(c)).astype(BF16)
    out_ref[0] = jnp.dot(sc, w_ref[0], preferred_element_type=F32) + b_ref[0]


def _modulation(c_all, ada_w_bf, ada_b):
    n = c_all.shape[0]
    return pl.pallas_call(
        _mod_kernel,
        out_shape=jax.ShapeDtypeStruct((DEPTH, n, 3 * D_MODEL), F32),
        grid=(DEPTH, 3),
        in_specs=[
            pl.BlockSpec((n, D_MODEL), lambda l, j: (0, 0)),
            pl.BlockSpec((1, D_MODEL, D_MODEL), lambda l, j: (l, 0, j)),
            pl.BlockSpec((1, 1, D_MODEL), lambda l, j: (l, 0, j)),
        ],
        out_specs=pl.BlockSpec((1, n, D_MODEL), lambda l, j: (l, 0, j)),
        compiler_params=_cparams(2),
        name="adaln_mod",
    )(c_all, ada_w_bf, ada_b.reshape(DEPTH, 1, 3 * D_MODEL))


def _pre_kernel(x_ref, shift_ref, scale_ref, ng_ref, wstd_ref, wt_ref, *rest):
    z, zt = _pre_project(x_ref[0], shift_ref, scale_ref, ng_ref, wstd_ref, wt_ref)
    _pre_finish(z, zt, *rest)


def _post_pre_kernel(x_ref, oa_ref, ob_ref, oc_ref, od_ref, gt_in_ref, gc_in_ref, gate_ref, wout_ref,
                     shift_ref, scale_ref, ng_ref, wstd_ref, wt_ref, *rest):
    xn = _post_update(x_ref, (oa_ref, ob_ref, oc_ref, od_ref), gt_in_ref, gc_in_ref, gate_ref, wout_ref)
    rest[-1][0] = xn
    z, zt = _pre_project(xn, shift_ref, scale_ref, ng_ref, wstd_ref, wt_ref)
    _pre_finish(z, zt, *rest[:-1])


def _pre_project(x, shift_ref, scale_ref, ng_ref, wstd_ref, wt_ref):
    ms = jnp.mean(x * x, axis=-1, keepdims=True)
    gain = ng_ref[...] * (1.0 + scale_ref[0])
    hb = ((x * lax.rsqrt(ms + EPS)) * gain + shift_ref[0]).astype(BF16)
    z = jnp.dot(hb, wstd_ref[...], preferred_element_type=F32)
    zt = lax.dot_general(wt_ref[...], hb, (((1,), (1,)), ((), ())),
                         preferred_element_type=F32)
    return z, zt


def _pre_finish(z, zt,
              gqa_col_ref, wuqT_ref, gkv_row_ref, wukp_ref, wuvT_ref,
              gq_col_ref, gk_row_ref,
              mkc_ref, mks_ref, mqc_ref, mqs_ref,
              dkc_ref, dks_ref, dqc_ref, dqs_ref,
              gkc_ref, gks_ref, gqc_ref, gqs_ref,
              ind_mla_ref, ind_diff_ref, ind_gqa_ref,
              qt_mla_ref, k_mla_ref, vt_mla_ref,
              qt_diff_ref, k_diff_ref, vt_diff_ref,
              qt_gqa_ref, k_gqa_ref, vt_gqa_ref,
              xc_ref, gc_ref, kn_mla_ref, kn_diff_ref, kn_gqa_ref, gt_ref):
    ts = z.shape[0]

    def key_norms(kb, ind_ref, out_ref):
        kf = kb.astype(F32)
        r = lax.dot_general(ind_ref[...], (kf * kf).astype(BF16), (((1,), (1,)), ((), ())),
                            preferred_element_type=F32)
        out_ref[0, 0] = jnp.broadcast_to(jnp.max(r, axis=1, keepdims=True), (NORM_ROWS, LANES))

    qlt = zt[ZT_QA:ZT_QA + Q_LORA]
    msq = jnp.mean(qlt * qlt, axis=0, keepdims=True)
    qnb = ((qlt * lax.rsqrt(msq + EPS)) * gqa_col_ref[...]).astype(BF16)
    qat = jnp.dot(wuqT_ref[...], qnb, preferred_element_type=F32)
    sc_mla = (NOPE_A + ROPE_A) ** -0.5 * LOG2E
    dq = NOPE_A + ROPE_A
    pad = jnp.zeros((LANES - dq, ts), F32)
    for hh in range(H_A):
        nope = qat[dq * hh:dq * hh + NOPE_A]
        pe = qat[dq * hh + NOPE_A:dq * (hh + 1)]
        sw = jnp.concatenate([pe[ROPE_A // 2:], pe[:ROPE_A // 2]], axis=0)
        pe = pe * mqc_ref[...] + sw * mqs_ref[...]
        blk = jnp.concatenate([nope, pe, pad], axis=0) * sc_mla
        qt_mla_ref[0, LANES * hh:LANES * (hh + 1), :] = blk.astype(BF16)

    kvl = z[:, ZS_KVL:ZS_KVL + KV_LORA]
    msk = jnp.mean(kvl * kvl, axis=-1, keepdims=True)
    kvn_f = (kvl * lax.rsqrt(msk + EPS)) * gkv_row_ref[...]
    kvn = kvn_f.astype(BF16)
    knope = jnp.dot(kvn, wukp_ref[...], preferred_element_type=F32)
    kpe = z[:, ZS_KPE:ZS_KPE + LANES] * mkc_ref[...] + z[:, ZS_KPE_SW:ZS_KPE_SW + LANES] * mks_ref[...]
    kb = jnp.concatenate([(knope[:, LANES * hh:LANES * (hh + 1)] + kpe).astype(BF16) for hh in range(H_A)],
                         axis=1)
    k_mla_ref[0] = kb
    key_norms(kb, ind_mla_ref, kn_mla_ref)
    vt_mla_ref[0, 0] = jnp.dot(wuvT_ref[...], kvn_f.T.astype(BF16),
                               preferred_element_type=F32).astype(BF16)

    kd = z[:, ZS_KB:ZS_KB + 256]
    lane_d = lax.broadcasted_iota(jnp.int32, (ts, 256), 1)
    half_d = DK_B // 8
    kd_sw = jnp.where(lane_d % (2 * half_d) < half_d, pltpu.roll(kd, 256 - half_d, 1), pltpu.roll(kd, half_d, 1))
    kb = (kd * dkc_ref[...] + kd_sw * dks_ref[...]).astype(BF16)
    k_diff_ref[0] = kb
    key_norms(kb, ind_diff_ref, kn_diff_ref)
    sc_diff = DK_B ** -0.5 * LOG2E
    rot = DK_B // 4
    for m in range(2 * H_B):
        q = zt[ZT_QB + DK_B * m:ZT_QB + DK_B * (m + 1)]
        q8 = q[0:rot]
        q8 = q8 * dqc_ref[0:rot, :] + pltpu.roll(q8, rot // 2, 0) * dqs_ref[0:rot, :]
        qq = jnp.concatenate([q8, q[rot:]], axis=0) * sc_diff
        qt_diff_ref[0, DK_B * m:DK_B * (m + 1), :] = qq.astype(BF16)
    vt_diff_ref[0, 0] = zt[ZT_VB:ZT_VB + 256].astype(BF16)

    kg = z[:, ZS_KD:ZS_KD + LANES]
    k2 = kg * kg
    lo = lax.broadcasted_iota(jnp.int32, (ts, LANES), 1) < HD_D
    s_lo = jnp.sum(jnp.where(lo, k2, 0.0), axis=-1, keepdims=True)
    s_hi = jnp.sum(jnp.where(lo, 0.0, k2), axis=-1, keepdims=True)
    rk = lax.rsqrt(jnp.where(lo, s_lo, s_hi) / HD_D + EPS)
    kn = (kg * rk) * gk_row_ref[...]
    half_g = HD_D // 4
    lane_g = lax.broadcasted_iota(jnp.int32, (ts, LANES), 1)
    knsw = jnp.where(lane_g % (2 * half_g) < half_g, pltpu.roll(kn, LANES - half_g, 1), pltpu.roll(kn, half_g, 1))
    kb = (kn * gkc_ref[...] + knsw * gks_ref[...]).astype(BF16)
    k_gqa_ref[0] = kb
    key_norms(kb, ind_gqa_ref, kn_gqa_ref)
    sc_gqa = HD_D ** -0.5 * LOG2E
    for hh in range(H_D):
        q = zt[ZT_QD + HD_D * hh:ZT_QD + HD_D * (hh + 1)]
        r = lax.rsqrt(jnp.mean(q * q, axis=0, keepdims=True) + EPS)
        qn = (q * r) * gq_col_ref[...]
        qsw = jnp.concatenate([qn[half_g:2 * half_g], qn[0:half_g], qn[3 * half_g:], qn[2 * half_g:3 * half_g]],
                              axis=0)
        qq = (qn * gqc_ref[...] + qsw * gqs_ref[...]) * sc_gqa
        qt_gqa_ref[0, HD_D * hh:HD_D * (hh + 1), :] = qq.astype(BF16)
    vt_gqa_ref[0, 0] = zt[ZT_VD:ZT_VD + 128].astype(BF16)

    xc_ref[0] = z[:, ZS_XC:ZS_XC + W_GROUP]
    g = z[:, ZS_G:ZS_END]
    gc_ref[0] = (g * jax.nn.sigmoid(g)).astype(BF16)
    g = zt[ZT_G:ZT_END]
    gt_ref[0] = (g * jax.nn.sigmoid(g)).astype(BF16)


def _pre_call(x, shift, scale, lw, tb, prev=None):
    b, s, d = x.shape
    ts = TOK_TILE
    nt = s // ts
    full2 = lambda a: pl.BlockSpec(a.shape, lambda i, bb: (0, 0))
    tok_rows = lambda w: pl.BlockSpec((ts, w), lambda i, bb: (i, 0))
    tok_cols = lambda r: pl.BlockSpec((r, ts), lambda i, bb: (0, i))
    tok3 = lambda w: pl.BlockSpec((1, ts, w), lambda i, bb: (bb, i, 0))
    vec3 = pl.BlockSpec((1, 1, d), lambda i, bb: (bb, 0, 0))
    params = [lw["norm_g"], lw["w_std"], lw["w_t"],
              lw["gqa_col"], lw["w_uqT"], lw["gkv_row"], lw["w_ukp"], lw["w_uvT"],
              lw["gq_col"], lw["gk_row"]]
    in_specs = [tok3(d)]
    head = [x]
    if prev is not None:
        feat3 = lambda r: pl.BlockSpec((1, r, ts), lambda i, bb: (bb, 0, i))
        in_specs += [feat3(W_GROUP), feat3(W_GROUP), tok3(W_GROUP), feat3(W_GROUP),
                     feat3(3 * W_GROUP), tok3(W_GROUP), vec3, full2(prev[7])]
        head += list(prev)
    in_specs += [vec3, vec3]
    in_specs += [full2(p) for p in params]
    in_specs += [tok_rows(128), tok_rows(128), tok_cols(32), tok_cols(32),
                 tok_rows(256), tok_rows(256), tok_cols(32), tok_cols(32),
                 tok_rows(128), tok_rows(128), tok_cols(64), tok_cols(64)]
    tables = [tb["mkc"], tb["mks"], tb["mqc"], tb["mqs"],
              tb["dkc"], tb["dks"], tb["dqc"], tb["dqs"],
              tb["gkc"], tb["gks"], tb["gqc"], tb["gqs"]]
    inds = [jnp.asarray(IND_MLA, BF16), jnp.asarray(IND_DIFF, BF16), jnp.asarray(IND_GQA, BF16)]
    in_specs += [full2(a) for a in inds]
    kn_shape = jax.ShapeDtypeStruct((b, nt, NORM_ROWS, LANES), F32)
    kn_spec = pl.BlockSpec((1, 1, NORM_ROWS, LANES), lambda i, bb: (bb, i, 0, 0))

    def qt_spec(rows):
        return pl.BlockSpec((1, rows, ts), lambda i, bb: (bb, 0, i))

    def k_spec(w):
        return pl.BlockSpec((1, ts, w), lambda i, bb: (bb, i, 0))

    def vt_spec(rows):
        return pl.BlockSpec((1, 1, rows, ts), lambda i, bb: (bb, i, 0, 0))

    out_shape = [
        jax.ShapeDtypeStruct((b, 512, s), BF16), jax.ShapeDtypeStruct((b, s, 512), BF16),
        jax.ShapeDtypeStruct((b, nt, 256, ts), BF16),
        jax.ShapeDtypeStruct((b, 256, s), BF16), jax.ShapeDtypeStruct((b, s, 256), BF16),
        jax.ShapeDtypeStruct((b, nt, 256, ts), BF16),
        jax.ShapeDtypeStruct((b, 256, s), BF16), jax.ShapeDtypeStruct((b, s, 128), BF16),
        jax.ShapeDtypeStruct((b, nt, 128, ts), BF16),
        jax.ShapeDtypeStruct((b, s, 256), F32), jax.ShapeDtypeStruct((b, s, 256), BF16),
        kn_shape, kn_shape, kn_shape, jax.ShapeDtypeStruct((b, 3 * W_GROUP, s), BF16),
    ]
    out_specs = [qt_spec(512), k_spec(512), vt_spec(256),
                 qt_spec(256), k_spec(256), vt_spec(256),
                 qt_spec(256), k_spec(128), vt_spec(128),
                 k_spec(256), k_spec(256), kn_spec, kn_spec, kn_spec, qt_spec(3 * W_GROUP)]
    if prev is not None:
        out_shape.append(jax.ShapeDtypeStruct((b, s, d), F32))
        out_specs.append(tok3(d))
    return pl.pallas_call(
        _pre_kernel if prev is None else _post_pre_kernel,
        out_shape=out_shape, grid=(nt, b), in_specs=in_specs, out_specs=out_specs,
        compiler_params=_cparams(2), name="pre_proj" if prev is None else "post_pre_proj",
    )(*head, shift, scale, *params, *tables, *inds)


def _attn_kernel(*refs, maps, n_chunks, tk, tq, kind, lam_init):
    if kind == "diff":
        qt_ref, k_ref, vt_ref, kn_ref, lam_ref, subln_ref, out_ref, w_scr = refs
    else:
        qt_ref, k_ref, vt_ref, kn_ref, out_ref, w_scr = refs
        lam_ref = subln_ref = None
    n_tiles = qt_ref.shape[2] // tq
    dv = DV
    nm = len(maps)
    finish = functools.partial(_attn_finish, out_ref=out_ref, lam_ref=lam_ref, subln_ref=subln_ref,
                               kind=kind, lam_init=lam_init, tq=tq)

    for t in range(n_tiles):
        for i, (kg, q0, qn, woff, v0, _) in enumerate(maps):
            q = qt_ref[0, q0:q0 + qn, tq * t:tq * (t + 1)]
            pieces = []
            if woff > 0:
                pieces.append(jnp.zeros((woff, tq), BF16))
            pieces.append(q)
            if LANES - woff - qn > 0:
                pieces.append(jnp.zeros((LANES - woff - qn, tq), BF16))
            w_scr[t, i] = pieces[0] if len(pieces) == 1 else jnp.concatenate(pieces, axis=0)

    def scores(t, i, c, r1=0, rows=tk):
        kg = maps[i][0]
        r0 = c * tk + r1 if isinstance(c, int) else pl.multiple_of(c * tk, tk)
        return jnp.dot(k_ref[0, pl.ds(r0, rows), LANES * kg:LANES * (kg + 1)], w_scr[t, i],
                       preferred_element_type=F32)

    def pv(i, c, p, r1=0, rows=tk):
        v0 = maps[i][4]
        return jnp.dot(vt_ref[0, c, v0:v0 + dv, r1:r1 + rows], p.astype(BF16),
                       preferred_element_type=F32)

    kn2 = jnp.max(kn_ref[0], axis=0)
    kmax = [jnp.sqrt(kn2[maps[i][5]:maps[i][5] + 1, 0:1]) for i in range(nm)]
    bounds = []
    for t in range(n_tiles):
        for i in range(nm):
            w = w_scr[t, i].astype(F32)
            bounds.append(jnp.sqrt(jnp.sum(w * w, axis=0, keepdims=True)) * kmax[i])
    units = [(t, i, c, r1) for t in range(n_tiles) for c in range(n_chunks)
             for r1 in range(0, tk, SUB_KEYS) for i in range(nm)]
    per_tile = len(units) // n_tiles
    accs, ls = [None] * (n_tiles * nm), [None] * (n_tiles * nm)
    pending = []
    lo, hi = 2.0 ** -DENOM_LOG2_RANGE, 2.0 ** DENOM_LOG2_RANGE
    for n in range(len(units) + SCORE_LEAD):
        if n < len(units):
            t, i, c, r1 = units[n]
            pending.append(scores(t, i, c, r1, SUB_KEYS))
        if n >= SCORE_LEAD:
            t, i, c, r1 = units[n - SCORE_LEAD]
            j = t * nm + i
            p = jnp.exp2(pending.pop(0) - bounds[j])
            part, psum = pv(i, c, p, r1, SUB_KEYS), jnp.sum(p, axis=0, keepdims=True)
            accs[j] = part if accs[j] is None else accs[j] + part
            ls[j] = psum if ls[j] is None else ls[j] + psum
            if (n - SCORE_LEAD + 1) % per_tile == 0:
                finish(t, accs[t * nm:(t + 1) * nm], ls[t * nm:(t + 1) * nm])

    for t in range(n_tiles):
        bad = None
        for l in ls[t * nm:(t + 1) * nm]:
            b = jnp.where((l > lo) & (l < hi), 0.0, 1.0)
            bad = b if bad is None else jnp.maximum(bad, b)

        @pl.when(jnp.max(bad) > 0.0)
        def _(t=t):
            def safe_body(c, carry):
                new = []
                for i in range(nm):
                    m, l, acc = carry[i]
                    s = scores(t, i, c)
                    mn = jnp.maximum(m, jnp.max(s, axis=0, keepdims=True))
                    p, alpha = jnp.exp2(s - mn), jnp.exp2(m - mn)
                    new.append((mn, alpha * l + jnp.sum(p, axis=0, keepdims=True), alpha * acc + pv(i, c, p)))
                return tuple(new)

            init = tuple((jnp.full((1, tq), -jnp.inf, F32), jnp.zeros((1, tq), F32), jnp.zeros((dv, tq), F32))
                         for _ in range(nm))
            fs = lax.fori_loop(0, n_chunks, safe_body, init)
            finish(t, [f[2] for f in fs], [f[1] for f in fs])


def _attn_finish(t, accs, ls, out_ref, lam_ref, subln_ref, kind, lam_init, tq):
    os_ = [acc / l for acc, l in zip(accs, ls)]

    if kind == "diff":
        lf = lam_ref[...]
        s1 = jnp.sum(lf[0:1] * lf[1:2], axis=-1, keepdims=True)
        s2 = jnp.sum(lf[2:3] * lf[3:4], axis=-1, keepdims=True)
        lam_full = jnp.exp(s1) - jnp.exp(s2) + lam_init
        heads = []
        for hh in range(H_B):
            o = os_[2 * hh] - lam_full * os_[2 * hh + 1]
            r = lax.rsqrt(jnp.mean(o * o, axis=0, keepdims=True) + EPS)
            heads.append(((o * r) * subln_ref[...]) * (1.0 - lam_init))
        ot = jnp.concatenate(heads, axis=0)
    else:
        ot = jnp.concatenate(os_, axis=0)
    out_ref[0, :, tq * t:tq * (t + 1)] = ot.astype(out_ref.dtype)


def _attn_call(qt, k, vt, kn, maps, kind, extra=(), lam_init=0.0, name="attn"):
    b, fq, s = qt.shape
    fk = k.shape[2]
    _, nt, fv, tk = vt.shape
    tq = Q_TILE
    n_tiles = 1 if kind == "diff" else max(1, min(s // tq, (UNITS_PER_STEP * tk) // (len(maps) * s)))
    tqs = tq * n_tiles
    in_specs = [pl.BlockSpec((1, fq, tqs), lambda bb, qi: (bb, 0, qi)),
                pl.BlockSpec((1, s, fk), lambda bb, qi: (bb, 0, 0)),
                pl.BlockSpec((1, nt, fv, tk), lambda bb, qi: (bb, 0, 0, 0)),
                pl.BlockSpec((1, nt, NORM_ROWS, LANES), lambda bb, qi: (bb, 0, 0, 0))]
    in_specs += [pl.BlockSpec(e.shape, lambda bb, qi: (0, 0)) for e in extra]
    kern = functools.partial(_attn_kernel, maps=tuple(maps), n_chunks=nt, tk=tk, tq=tq, kind=kind,
                             lam_init=lam_init)
    return pl.pallas_call(
        kern, out_shape=jax.ShapeDtypeStruct((b, W_GROUP, s), BF16),
        grid=(b, s // tqs), in_specs=in_specs,
        out_specs=pl.BlockSpec((1, W_GROUP, tqs), lambda bb, qi: (bb, 0, qi)),
        scratch_shapes=[pltpu.VMEM((n_tiles, len(maps), LANES, tq), BF16)],
        compiler_params=_cparams(2), name=name,
    )(qt, k, vt, kn, *extra)


MAPS_MLA = [(hh, LANES * hh, LANES, 0, V_A * hh, hh) for hh in range(H_A)]
MAPS_DIFF = [(m // 4, DK_B * m, DK_B, DK_B * (m % 4), DV_B * (m // 2), m) for m in range(2 * H_B)]
MAPS_GQA = [(0, HD_D * hh, HD_D, HD_D * (hh // (H_D // KV_H_D)), HD_D * (hh // (H_D // KV_H_D)),
             hh // (H_D // KV_H_D)) for hh in range(H_D)]


def _softplus(x):
    return jnp.maximum(x, 0.0) + jnp.log1p(jnp.exp(-jnp.abs(x)))


def _lru_kernel(xc_ref, cw_ref, cb_ref, wbd_ref, bias_ref, lam_ref, out_ref,
                xpad, af, gf, ar, gr, hfl, pfl, hrl, prl, *, s, tc):
    tseg = s // SUBLANES
    tl_chunk = tc // SUBLANES
    xpad[0:tseg, :] = jnp.zeros((tseg, LANES), F32)
    xpad[tseg + s:2 * tseg + s, :] = jnp.zeros((tseg, LANES), F32)
    xpad[tseg:tseg + s, :] = xc_ref[0]
    lam = lam_ref[0]
    sp_f = _softplus(-lam[0:1])
    sp_r = _softplus(-lam[1:2])
    cw = cw_ref[...]
    cb = cb_ref[...]
    wbd = wbd_ref[0]
    bias = bias_ref[0]

    def gates(half_pre_r, half_pre_i, sp, half_xconv):
        log_a = (jnp.tanh(half_pre_r) + 1.0) * ((-0.5 * LRU_C) * sp)
        a = jnp.exp(log_a)
        v = -jnp.tanh(log_a) * (a * a + 1.0)
        gx = jnp.where(v > 0.0, v * lax.rsqrt(v), 0.0) * ((jnp.tanh(half_pre_i) + 1.0) * half_xconv)
        return a, gx

    def chunk(ci, carry):
        tl0 = ci * tl_chunk
        tiles = [xpad[pl.ds(tseg + tl0 + k, SUBLANES, stride=tseg), :]
                 for k in range(-CONV_PAD_L, tl_chunk + CONV_W - 1 - CONV_PAD_L)]
        xconv = cb
        for j in range(CONV_W):
            xconv = xconv + jnp.concatenate(tiles[j:j + tl_chunk], axis=0) * cw[j:j + 1]
        pre = jnp.dot(xconv.astype(BF16), wbd, preferred_element_type=F32) + bias
        half_xconv = 0.5 * xconv
        r0 = pl.multiple_of(ci * tc, tc)
        a, gx = gates(pre[:, 0:128], pre[:, 128:256], sp_f, half_xconv)
        af[pl.ds(r0, tc), :] = a
        gf[pl.ds(r0, tc), :] = gx
        a, gx = gates(pre[:, 256:384], pre[:, 384:512], sp_r, half_xconv)
        ar[pl.ds(r0, tc), :] = a
        gr[pl.ds(r0, tc), :] = gx
        return carry

    lax.fori_loop(0, s // tc, chunk, 0)

    def scan_body(t, carry):
        hf, pf, hr, pr = carry
        r0 = pl.multiple_of(t * SUBLANES, SUBLANES)
        a = af[pl.ds(r0, SUBLANES), :]
        hf = a * hf + gf[pl.ds(r0, SUBLANES), :]
        pf = a * pf
        hfl[pl.ds(r0, SUBLANES), :] = hf
        pfl[pl.ds(r0, SUBLANES), :] = pf
        r1 = pl.multiple_of((tseg - 1 - t) * SUBLANES, SUBLANES)
        a = ar[pl.ds(r1, SUBLANES), :]
        hr = a * hr + gr[pl.ds(r1, SUBLANES), :]
        pr = a * pr
        hrl[pl.ds(r1, SUBLANES), :] = hr
        prl[pl.ds(r1, SUBLANES), :] = pr
        return hf, pf, hr, pr

    zero8 = jnp.zeros((SUBLANES, LANES), F32)
    one8 = jnp.ones((SUBLANES, LANES), F32)
    h_end, p_end, h_start, p_start = lax.fori_loop(0, tseg, scan_body, (zero8, one8, zero8, one8), unroll=8)

    row = lax.broadcasted_iota(jnp.int32, (SUBLANES, LANES), 0)
    cf, cr = zero8, zero8
    for _ in range(SUBLANES - 1):
        cf = jnp.where(row >= 1, pltpu.roll(h_end + p_end * cf, 1, 0), 0.0)
        cr = jnp.where(row < SUBLANES - 1, pltpu.roll(h_start + p_start * cr, SUBLANES - 1, 0), 0.0)

    ho = af

    def patch(ci, carry):
        r0 = pl.multiple_of(ci * tc, tc)
        cft = jnp.concatenate([cf] * tl_chunk, axis=0)
        crt = jnp.concatenate([cr] * tl_chunk, axis=0)
        ho[pl.ds(r0, tc), :] = (hfl[pl.ds(r0, tc), :] + pfl[pl.ds(r0, tc), :] * cft) + (
            hrl[pl.ds(r0, tc), :] + prl[pl.ds(r0, tc), :] * crt)
        return carry

    lax.fori_loop(0, s // tc, patch, 0)

    def untangle(u, carry):
        for j in range(SUBLANES):
            base = u * (2 * SUBLANES * SUBLANES) + j
            blk = jnp.concatenate([ho[pl.ds(base, SUBLANES, stride=SUBLANES), :],
                                   ho[pl.ds(base + SUBLANES * SUBLANES, SUBLANES, stride=SUBLANES), :]], axis=0)
            out_ref[0, pl.ds(pl.multiple_of(j * tseg + u * 2 * SUBLANES, 2 * SUBLANES), 2 * SUBLANES), :] = (
                blk.astype(out_ref.dtype))
        return carry

    lax.fori_loop(0, tseg // (2 * SUBLANES), untangle, 0)


def _lru_call(xc, lw):
    b, s, _ = xc.shape
    kern = functools.partial(_lru_kernel, s=s, tc=LRU_CHUNK)
    seq = pltpu.VMEM((s, LANES), F32)
    return pl.pallas_call(
        kern, out_shape=jax.ShapeDtypeStruct((b, s, W_GROUP), BF16),
        grid=(b, W_GROUP // LANES),
        in_specs=[pl.BlockSpec((1, s, LANES), lambda bb, hf: (bb, 0, hf)),
                  pl.BlockSpec((CONV_W, LANES), lambda bb, hf: (0, hf)),
                  pl.BlockSpec((1, LANES), lambda bb, hf: (0, hf)),
                  pl.BlockSpec((1, LANES, 4 * LANES), lambda bb, hf: (hf, 0, 0)),
                  pl.BlockSpec((1, 1, 4 * LANES), lambda bb, hf: (hf, 0, 0)),
                  pl.BlockSpec((1, 2, LANES), lambda bb, hf: (hf, 0, 0))],
        out_specs=pl.BlockSpec((1, s, LANES), lambda bb, hf: (bb, 0, hf)),
        scratch_shapes=[pltpu.VMEM((s + 2 * (s // SUBLANES), LANES), F32)] + [seq] * 8,
        compiler_params=_cparams(2), name="rglru",
    )(xc, lw["conv_w"], lw["conv_b"], lw["lru_wbd"], lw["lru_bias"], lw["lru_lam"])


def _post_update(x_ref, o_refs, gt_ref, gc_ref, gate_ref, wout_ref):
    y = None
    row = 0
    for i, o_ref in enumerate(o_refs):
        w_blk = wout_ref[W_GROUP * i:W_GROUP * (i + 1), :]
        if i == 2:
            og = (o_ref[0].astype(F32) * gc_ref[0].astype(F32)).astype(BF16)
            part = jnp.dot(og, w_blk, preferred_element_type=F32)
        else:
            og = (o_ref[0].astype(F32) * gt_ref[0, row:row + W_GROUP, :].astype(F32)).astype(BF16)
            row += W_GROUP
            part = lax.dot_general(og, w_blk, (((0,), (0,)), ((), ())), preferred_element_type=F32)
        y = part if y is None else y + part
    return x_ref[0] + gate_ref[0] * y


def _post_kernel(x_ref, oa_ref, ob_ref, oc_ref, od_ref, gt_ref, gc_ref, gate_ref, wout_ref, fn_ref, out_ref):
    xn = _post_update(x_ref, (oa_ref, ob_ref, oc_ref, od_ref), gt_ref, gc_ref, gate_ref, wout_ref)
    ms = jnp.mean(xn * xn, axis=-1, keepdims=True)
    out_ref[0] = (xn * lax.rsqrt(ms + EPS)) * fn_ref[...]


def _post_call(x, oa, ob, oc, od, gt, gc, gate, w_out_bf, final_norm):
    b, s, d = x.shape
    ts = POST_TILE
    tok = lambda w: pl.BlockSpec((1, ts, w), lambda bb, i: (bb, i, 0))
    feat = lambda r: pl.BlockSpec((1, r, ts), lambda bb, i: (bb, 0, i))
    return pl.pallas_call(
        _post_kernel,
        out_shape=jax.ShapeDtypeStruct((b, s, d), F32),
        grid=(b, s // ts),
        in_specs=[tok(d), feat(W_GROUP), feat(W_GROUP), tok(W_GROUP), feat(W_GROUP), feat(3 * W_GROUP), tok(W_GROUP),
                  pl.BlockSpec((1, 1, d), lambda bb, i: (bb, 0, 0)),
                  pl.BlockSpec((d, d), lambda bb, i: (0, 0)),
                  pl.BlockSpec((1, d), lambda bb, i: (0, 0))],
        out_specs=tok(d),
        compiler_params=_cparams(2), name="post_proj",
    )(x, oa, ob, oc, od, gt, gc, gate, w_out_bf, final_norm)


def _angles(pos, half, theta):
    inv = jnp.power(jnp.float32(theta), -jnp.arange(half, dtype=F32) / half)
    return pos[:, None] * inv[None, :]


def _tables(s):
    pos = jnp.arange(s, dtype=F32)
    rows = s // GRID_W
    row = jnp.repeat(jnp.arange(rows, dtype=F32), GRID_W)
    col = jnp.tile(jnp.arange(GRID_W, dtype=F32), rows)
    a = _angles(pos, ROPE_A // 2, MLA_ROPE_THETA)
    c, sn = jnp.cos(a), jnp.sin(a)
    z64, z32 = jnp.zeros((s, 64), F32), jnp.zeros((s, 32), F32)
    tb = {"mkc": jnp.concatenate([z64, c, c, z32], axis=1),
          "mks": jnp.concatenate([z64, -sn, sn, z32], axis=1),
          "mqc": jnp.concatenate([c, c], axis=1).T,
          "mqs": jnp.concatenate([-sn, sn], axis=1).T}
    a = _angles(pos, DK_B // 8, ROPE_THETA)
    c, sn = jnp.cos(a), jnp.sin(a)
    c32 = jnp.concatenate([c, c, jnp.ones((s, DK_B - DK_B // 4), F32)], axis=1)
    s32 = jnp.concatenate([-sn, sn, jnp.zeros((s, DK_B - DK_B // 4), F32)], axis=1)
    tb.update({"dkc": jnp.tile(c32, (1, 2 * H_B)), "dks": jnp.tile(s32, (1, 2 * H_B)),
               "dqc": c32.T, "dqs": s32.T})
    ar_ = _angles(row, HD_D // 4, AXIAL_THETA)
    ac_ = _angles(col, HD_D // 4, AXIAL_THETA)
    cr, sr, cc, sc = jnp.cos(ar_), jnp.sin(ar_), jnp.cos(ac_), jnp.sin(ac_)
    c64 = jnp.concatenate([cr, cr, cc, cc], axis=1)
    s64 = jnp.concatenate([-sr, sr, -sc, sc], axis=1)
    tb.update({"gkc": jnp.tile(c64, (1, KV_H_D)), "gks": jnp.tile(s64, (1, KV_H_D)),
               "gqc": c64.T, "gqs": s64.T})
    return tb


def _blockdiag_pair(w, half):
    z = jnp.zeros((BW_C, BW_C), F32)
    top = jnp.concatenate([w[2 * half], z], axis=1)
    bot = jnp.concatenate([z, w[2 * half + 1]], axis=1)
    return jnp.concatenate([top, bot], axis=0)


def _layer_weights(l, p):
    w = p["w_in"][l]
    col = lambda o, n: w[:, o:o + n]
    z64 = jnp.zeros((D_MODEL, 64), F32)
    z32 = jnp.zeros((D_MODEL, 32), F32)
    kra = col(O_KRA, ROPE_A)
    w_std = jnp.concatenate([
        col(O_KVA, KV_LORA),
        z64, kra, z32,
        z64, kra[:, P_MLA], z32,
        col(O_KB, 256), col(O_KD, 128),
        col(O_XC, W_GROUP), col(O_GC, W_GROUP)], axis=1)
    w_t = jnp.concatenate([
        col(O_QA, Q_LORA),
        col(O_QB, 256), col(O_VB, 256),
        col(O_QD, 256), col(O_VD, 128),
        col(O_GA, W_GROUP), col(O_GB, W_GROUP), col(O_GD, W_GROUP)], axis=1).T
    wuq = p["mla_w_uq"][l]
    wukv = p["mla_w_ukv"][l]
    dkv = NOPE_A + V_A
    z_k = jnp.zeros((KV_LORA, LANES - NOPE_A), F32)
    w_ukp = jnp.concatenate(
        [jnp.concatenate([wukv[:, dkv * h:dkv * h + NOPE_A], z_k], axis=1) for h in range(H_A)], axis=1)
    w_uv = jnp.concatenate([wukv[:, dkv * h + NOPE_A:dkv * (h + 1)] for h in range(H_A)], axis=1)
    gq, gk = p["gqa_q_norm"][l], p["gqa_k_norm"][l]
    wa, wx = p["lru_wa"][l], p["lru_wx"][l]
    ba, bx = p["lru_ba"][l], p["lru_bx"][l]
    wbd = jnp.stack([jnp.concatenate([_blockdiag_pair(wa[0], hf), _blockdiag_pair(wx[0], hf),
                                      _blockdiag_pair(wa[1], hf), _blockdiag_pair(wx[1], hf)], axis=1)
                     for hf in range(2)])
    sl = lambda v, hf: v[LANES * hf:LANES * (hf + 1)]
    bias = jnp.stack([jnp.concatenate([sl(ba[0], hf), sl(bx[0], hf), sl(ba[1], hf), sl(bx[1], hf)])[None]
                      for hf in range(2)])
    lam = p["lru_lambda"][l]
    lam_h = jnp.stack([jnp.stack([sl(lam[0], hf), sl(lam[1], hf)]) for hf in range(2)])
    return {
        "norm_g": p["norm_g"][l][None], "w_std": w_std.astype(BF16), "w_t": w_t.astype(BF16),
        "gqa_col": p["mla_q_norm"][l][:, None], "w_uqT": wuq.T.astype(BF16),
        "gkv_row": p["mla_kv_norm"][l][None],
        "w_ukp": w_ukp.astype(BF16), "w_uvT": w_uv.T.astype(BF16),
        "gq_col": gq[:, None], "gk_row": jnp.tile(gk, KV_H_D)[None],
        "diff_lambda": p["diff_lambda"][l], "subln_col": p["diff_subln"][l][:, None],
        "conv_w": p["lru_conv_w"][l], "conv_b": p["lru_conv_b"][l][None],
        "lru_wbd": (0.5 * wbd).astype(BF16), "lru_bias": 0.5 * bias, "lru_lam": lam_h,
        "w_out": p["w_out"][l].astype(BF16),
    }


def _trunk(x, mods, lws, final_norm):
    b, s, d = x.shape
    tb = _tables(s)
    prev = None
    for l in range(DEPTH):
        lw = lws[l]
        shift, scale, gate = mods[l]
        outs = _pre_call(x, shift, scale, lw, tb, prev)
        if prev is not None:
            x = outs[-1]
        (qt_mla, k_mla, vt_mla, qt_diff, k_diff, vt_diff, qt_gqa, k_gqa, vt_gqa, xc, gc,
         kn_mla, kn_diff, kn_gqa, gt) = outs[:15]
        lam_init = 0.8 - 0.6 * math.exp(-0.3 * l)
        oa = _attn_call(qt_mla, k_mla, vt_mla, kn_mla, MAPS_MLA, "mla", name="attn_mla")
        ob = _attn_call(qt_diff, k_diff, vt_diff, kn_diff, MAPS_DIFF, "diff",
                        extra=(lw["diff_lambda"], lw["subln_col"]), lam_init=lam_init, name="attn_diff")
        od = _attn_call(qt_gqa, k_gqa, vt_gqa, kn_gqa, MAPS_GQA, "gqa", name="attn_gqa")
        oc = _lru_call(xc, lw)
        prev = (oa, ob, oc, od, gt, gc, gate, lw["w_out"])
    return _post_call(x, *prev, final_norm)


def kernel(x_prompt, x_sample, c_prompt, c_sample, ada_w, ada_b, norm_g, w_in, mla_q_norm, mla_w_uq, mla_kv_norm, mla_w_ukv, diff_lambda, diff_subln, lru_conv_w, lru_conv_b, lru_wa, lru_ba, lru_wx, lru_bx, lru_lambda, gqa_q_norm, gqa_k_norm, w_out, final_norm):
    p = dict(norm_g=norm_g, w_in=w_in, mla_q_norm=mla_q_norm, mla_w_uq=mla_w_uq, mla_kv_norm=mla_kv_norm,
             mla_w_ukv=mla_w_ukv, diff_lambda=diff_lambda, diff_subln=diff_subln,
             lru_conv_w=lru_conv_w, lru_conv_b=lru_conv_b, lru_wa=lru_wa, lru_ba=lru_ba,
             lru_wx=lru_wx, lru_bx=lru_bx, lru_lambda=lru_lambda,
             gqa_q_norm=gqa_q_norm, gqa_k_norm=gqa_k_norm, w_out=w_out)
    lws = [_layer_weights(l, p) for l in range(DEPTH)]
    nb = x_prompt.shape[0]
    mod = _modulation(jnp.concatenate([c_prompt, c_sample], axis=0), ada_w.astype(BF16), ada_b)

    def mods_for(lo, hi):
        out = []
        for l in range(DEPTH):
            m = mod[l, lo:hi]
            out.append(tuple(m[:, D_MODEL * j:D_MODEL * (j + 1)][:, None, :] for j in range(3)))
        return out

    fn = final_norm[None]
    y_prompt = _trunk(x_prompt, mods_for(0, nb), lws, fn)
    y_sample = _trunk(x_sample, mods_for(nb, nb + x_sample.shape[0]), lws, fn)
    return (y_prompt, y_sample)
```

```python
import functools
import math

import jax
import jax.numpy as jnp
import numpy as np
from jax import lax
from jax.experimental import pallas as pl
from jax.experimental.pallas import tpu as pltpu

F32 = jnp.float32
BF16 = jnp.bfloat16

D_MODEL = 1024
DEPTH = 2
W_GROUP = 256
H_A, NOPE_A, ROPE_A, V_A = 4, 64, 32, 64
Q_LORA, KV_LORA = 192, 128
H_B, DK_B, DV_B = 4, 32, 64
H_C, BW_C, CONV_W = 4, 64, 4
CONV_PAD_L = CONV_W // 2
LRU_C = 8.0
H_D, KV_H_D, HD_D = 4, 2, 64
DV = 64
assert V_A == DV_B == HD_D == DV
GRID_W = 64
ROPE_THETA = 500000.0
MLA_ROPE_THETA = 10000.0
AXIAL_THETA = 10000.0
EPS = 1e-6
LOG2E = 1.4426950408889634

IN_SIZES = (Q_LORA, KV_LORA, ROPE_A, W_GROUP,
            H_B * 2 * DK_B, H_B * 2 * DK_B, H_B * DV_B, W_GROUP,
            W_GROUP, W_GROUP,
            H_D * HD_D, KV_H_D * HD_D, KV_H_D * HD_D, W_GROUP)
_OFF = np.concatenate([[0], np.cumsum(IN_SIZES)]).tolist()
(O_QA, O_KVA, O_KRA, O_GA, O_QB, O_KB, O_VB, O_GB, O_XC, O_GC, O_QD, O_KD, O_VD, O_GD, _) = _OFF

LANES = 128
SUBLANES = 8

TOK_TILE = 512
POST_TILE = 1024
Q_TILE = 512
UNITS_PER_STEP = 64
LRU_CHUNK = 2048
DENOM_LOG2_RANGE = 80
SUB_KEYS = 256
SCORE_LEAD = 2
VMEM_LIMIT = 56 * 1024 * 1024

ZS_KVL, ZS_KPE, ZS_KPE_SW, ZS_KB, ZS_KD, ZS_XC, ZS_G, ZS_END = (0, 128, 256, 384, 640, 768, 1024, 1280)
ZT_QA, ZT_QB, ZT_VB, ZT_QD, ZT_VD, ZT_G, ZT_END = (0, 192, 448, 704, 960, 1088, 1856)


def _partner(width, block, rot, half):
    idx = np.arange(width)
    d = idx % block
    out = idx.copy()
    in_rot = d < rot
    first = in_rot & ((d % (2 * half)) < half)
    second = in_rot & ~((d % (2 * half)) < half)
    out[first] = idx[first] + half
    out[second] = idx[second] - half
    return out


P_MLA = _partner(ROPE_A, ROPE_A, ROPE_A, ROPE_A // 2)


NORM_ROWS = 2 * SUBLANES


def _indicator(width, seg):
    m = np.zeros((NORM_ROWS, width), np.float32)
    m[np.arange(width) // seg, np.arange(width)] = 1.0
    return m


IND_MLA = _indicator(H_A * LANES, LANES)
IND_DIFF = _indicator(2 * H_B * DK_B, DK_B)
IND_GQA = _indicator(KV_H_D * HD_D, HD_D)


def _cparams(n_axes):
    return pltpu.CompilerParams(dimension_semantics=("arbitrary",) * n_axes,
                                vmem_limit_bytes=VMEM_LIMIT)


def _mod_kernel(c_ref, w_ref, b_ref, out_ref):
    c = c_ref[...]
    sc = (c * jax.nn.sigmoid(c)).astype(BF16)
    out_ref[0] = jnp.dot(sc, w_ref[0], preferred_element_type=F32) + b_ref[0]


def _modulation(c_all, ada_w_bf, ada_b):
    n = c_all.shape[0]
    return pl.pallas_call(
        _mod_kernel,
        out_shape=jax.ShapeDtypeStruct((DEPTH, n, 3 * D_MODEL), F32),
        grid=(DEPTH, 3),
        in_specs=[
            pl.BlockSpec((n, D_MODEL), lambda l, j: (0, 0)),
            pl.BlockSpec((1, D_MODEL, D_MODEL), lambda l, j: (l, 0, j)),
            pl.BlockSpec((1, 1, D_MODEL), lambda l, j: (l, 0, j)),
        ],
        out_specs=pl.BlockSpec((1, n, D_MODEL), lambda l, j: (l, 0, j)),
        compiler_params=_cparams(2),
        name="adaln_mod",
    )(c_all, ada_w_bf, ada_b.reshape(DEPTH, 1, 3 * D_MODEL))


def _pre_kernel(x_ref, shift_ref, scale_ref, ng_ref, wstd_ref, wt_ref, *rest):
    z, zt = _pre_project(x_ref[0], shift_ref, scale_ref, ng_ref, wstd_ref, wt_ref)
    _pre_finish(z, zt, *rest)


def _post_pre_kernel(x_ref, oa_ref, ob_ref, oc_ref, od_ref, gt_in_ref, gc_in_ref, gate_ref, wout_ref,
                     shift_ref, scale_ref, ng_ref, wstd_ref, wt_ref, *rest):
    xn = _post_update(x_ref, (oa_ref, ob_ref, oc_ref, od_ref), gt_in_ref, gc_in_ref, gate_ref, wout_ref)
    rest[-1][0] = xn
    z, zt = _pre_project(xn, shift_ref, scale_ref, ng_ref, wstd_ref, wt_ref)
    _pre_finish(z, zt, *rest[:-1])


def _pre_project(x, shift_ref, scale_ref, ng_ref, wstd_ref, wt_ref):
    ms = jnp.mean(x * x, axis=-1, keepdims=True)
    gain = ng_ref[...] * (1.0 + scale_ref[0])
    hb = ((x * lax.rsqrt(ms + EPS)) * gain + shift_ref[0]).astype(BF16)
    z = jnp.dot(hb, wstd_ref[...], preferred_element_type=F32)
    zt = lax.dot_general(wt_ref[...], hb, (((1,), (1,)), ((), ())),
                         preferred_element_type=F32)
    return z, zt


def _pre_finish(z, zt,
              gqa_col_ref, wuqT_ref, gkv_row_ref, wukp_ref, wuvT_ref,
              gq_col_ref, gk_row_ref,
              mkc_ref, mks_ref, mqc_ref, mqs_ref,
              dkc_ref, dks_ref, dqc_ref, dqs_ref,
              gkc_ref, gks_ref, gqc_ref, gqs_ref,
              ind_mla_ref, ind_diff_ref, ind_gqa_ref,
              qt_mla_ref, k_mla_ref, vt_mla_ref,
              qt_diff_ref, k_diff_ref, vt_diff_ref,
              qt_gqa_ref, k_gqa_ref, vt_gqa_ref,
              xc_ref, gc_ref, kn_mla_ref, kn_diff_ref, kn_gqa_ref, gt_ref):
    ts = z.shape[0]

    def key_norms(kb, ind_ref, out_ref):
        kf = kb.astype(F32)
        r = lax.dot_general(ind_ref[...], (kf * kf).astype(BF16), (((1,), (1,)), ((), ())),
                            preferred_element_type=F32)
        out_ref[0, 0] = jnp.broadcast_to(jnp.max(r, axis=1, keepdims=True), (NORM_ROWS, LANES))

    qlt = zt[ZT_QA:ZT_QA + Q_LORA]
    msq = jnp.mean(qlt * qlt, axis=0, keepdims=True)
    qnb = ((qlt * lax.rsqrt(msq + EPS)) * gqa_col_ref[...]).astype(BF16)
    qat = jnp.dot(wuqT_ref[...], qnb, preferred_element_type=F32)
    sc_mla = (NOPE_A + ROPE_A) ** -0.5 * LOG2E
    dq = NOPE_A + ROPE_A
    pad = jnp.zeros((LANES - dq, ts), F32)
    for hh in range(H_A):
        nope = qat[dq * hh:dq * hh + NOPE_A]
        pe = qat[dq * hh + NOPE_A:dq * (hh + 1)]
        sw = jnp.concatenate([pe[ROPE_A // 2:], pe[:ROPE_A // 2]], axis=0)
        pe = pe * mqc_ref[...] + sw * mqs_ref[...]
        blk = jnp.concatenate([nope, pe, pad], axis=0) * sc_mla
        qt_mla_ref[0, LANES * hh:LANES * (hh + 1), :] = blk.astype(BF16)

    kvl = z[:, ZS_KVL:ZS_KVL + KV_LORA]
    msk = jnp.mean(kvl * kvl, axis=-1, keepdims=True)
    kvn_f = (kvl * lax.rsqrt(msk + EPS)) * gkv_row_ref[...]
    kvn = kvn_f.astype(BF16)
    knope = jnp.dot(kvn, wukp_ref[...], preferred_element_type=F32)
    kpe = z[:, ZS_KPE:ZS_KPE + LANES] * mkc_ref[...] + z[:, ZS_KPE_SW:ZS_KPE_SW + LANES] * mks_ref[...]
    kb = jnp.concatenate([(knope[:, LANES * hh:LANES * (hh + 1)] + kpe).astype(BF16) for hh in range(H_A)],
                         axis=1)
    k_mla_ref[0] = kb
    key_norms(kb, ind_mla_ref, kn_mla_ref)
    vt_mla_ref[0, 0] = jnp.dot(wuvT_ref[...], kvn_f.T.astype(BF16),
                               preferred_element_type=F32).astype(BF16)

    kd = z[:, ZS_KB:ZS_KB + 256]
    lane_d = lax.broadcasted_iota(jnp.int32, (ts, 256), 1)
    half_d = DK_B // 8
    kd_sw = jnp.where(lane_d % (2 * half_d) < half_d, pltpu.roll(kd, 256 - half_d, 1), pltpu.roll(kd, half_d, 1))
    kb = (kd * dkc_ref[...] + kd_sw * dks_ref[...]).astype(BF16)
    k_diff_ref[0] = kb
    key_norms(kb, ind_diff_ref, kn_diff_ref)
    sc_diff = DK_B ** -0.5 * LOG2E
    rot = DK_B // 4
    for m in range(2 * H_B):
        q = zt[ZT_QB + DK_B * m:ZT_QB + DK_B * (m + 1)]
        q8 = q[0:rot]
        q8 = q8 * dqc_ref[0:rot, :] + pltpu.roll(q8, rot // 2, 0) * dqs_ref[0:rot, :]
        qq = jnp.concatenate([q8, q[rot:]], axis=0) * sc_diff
        qt_diff_ref[0, DK_B * m:DK_B * (m + 1), :] = qq.astype(BF16)
    vt_diff_ref[0, 0] = zt[ZT_VB:ZT_VB + 256].astype(BF16)

    kg = z[:, ZS_KD:ZS_KD + LANES]
    k2 = kg * kg
    lo = lax.broadcasted_iota(jnp.int32, (ts, LANES), 1) < HD_D
    s_lo = jnp.sum(jnp.where(lo, k2, 0.0), axis=-1, keepdims=True)
    s_hi = jnp.sum(jnp.where(lo, 0.0, k2), axis=-1, keepdims=True)
    rk = lax.rsqrt(jnp.where(lo, s_lo, s_hi) / HD_D + EPS)
    kn = (kg * rk) * gk_row_ref[...]
    half_g = HD_D // 4
    lane_g = lax.broadcasted_iota(jnp.int32, (ts, LANES), 1)
    knsw = jnp.where(lane_g % (2 * half_g) < half_g, pltpu.roll(kn, LANES - half_g, 1), pltpu.roll(kn, half_g, 1))
    kb = (kn * gkc_ref[...] + knsw * gks_ref[...]).astype(BF16)
    k_gqa_ref[0] = kb
    key_norms(kb, ind_gqa_ref, kn_gqa_ref)
    sc_gqa = HD_D ** -0.5 * LOG2E
    for hh in range(H_D):
        q = zt[ZT_QD + HD_D * hh:ZT_QD + HD_D * (hh + 1)]
        r = lax.rsqrt(jnp.mean(q * q, axis=0, keepdims=True) + EPS)
        qn = (q * r) * gq_col_ref[...]
        qsw = jnp.concatenate([qn[half_g:2 * half_g], qn[0:half_g], qn[3 * half_g:], qn[2 * half_g:3 * half_g]],
                              axis=0)
        qq = (qn * gqc_ref[...] + qsw * gqs_ref[...]) * sc_gqa
        qt_gqa_ref[0, HD_D * hh:HD_D * (hh + 1), :] = qq.astype(BF16)
    vt_gqa_ref[0, 0] = zt[ZT_VD:ZT_VD + 128].astype(BF16)

    xc_ref[0] = z[:, ZS_XC:ZS_XC + W_GROUP]
    g = z[:, ZS_G:ZS_END]
    gc_ref[0] = (g * jax.nn.sigmoid(g)).astype(BF16)
    g = zt[ZT_G:ZT_END]
    gt_ref[0] = (g * jax.nn.sigmoid(g)).astype(BF16)


def _pre_call(x, shift, scale, lw, tb, prev=None):
    b, s, d = x.shape
    ts = TOK_TILE
    nt = s // ts
    full2 = lambda a: pl.BlockSpec(a.shape, lambda i, bb: (0, 0))
    tok_rows = lambda w: pl.BlockSpec((ts, w), lambda i, bb: (i, 0))
    tok_cols = lambda r: pl.BlockSpec((r, ts), lambda i, bb: (0, i))
    tok3 = lambda w: pl.BlockSpec((1, ts, w), lambda i, bb: (bb, i, 0))
    vec3 = pl.BlockSpec((1, 1, d), lambda i, bb: (bb, 0, 0))
    params = [lw["norm_g"], lw["w_std"], lw["w_t"],
              lw["gqa_col"], lw["w_uqT"], lw["gkv_row"], lw["w_ukp"], lw["w_uvT"],
              lw["gq_col"], lw["gk_row"]]
    in_specs = [tok3(d)]
    head = [x]
    if prev is not None:
        feat3 = lambda r: pl.BlockSpec((1, r, ts), lambda i, bb: (bb, 0, i))
        in_specs += [feat3(W_GROUP), feat3(W_GROUP), tok3(W_GROUP), feat3(W_GROUP),
                     feat3(3 * W_GROUP), tok3(W_GROUP), vec3, full2(prev[7])]
        head += list(prev)
    in_specs += [vec3, vec3]
    in_specs += [full2(p) for p in params]
    in_specs += [tok_rows(128), tok_rows(128), tok_cols(32), tok_cols(32),
                 tok_rows(256), tok_rows(256), tok_cols(32), tok_cols(32),
                 tok_rows(128), tok_rows(128), tok_cols(64), tok_cols(64)]
    tables = [tb["mkc"], tb["mks"], tb["mqc"], tb["mqs"],
              tb["dkc"], tb["dks"], tb["dqc"], tb["dqs"],
              tb["gkc"], tb["gks"], tb["gqc"], tb["gqs"]]
    inds = [jnp.asarray(IND_MLA, BF16), jnp.asarray(IND_DIFF, BF16), jnp.asarray(IND_GQA, BF16)]
    in_specs += [full2(a) for a in inds]
    kn_shape = jax.ShapeDtypeStruct((b, nt, NORM_ROWS, LANES), F32)
    kn_spec = pl.BlockSpec((1, 1, NORM_ROWS, LANES), lambda i, bb: (bb, i, 0, 0))

    def qt_spec(rows):
        return pl.BlockSpec((1, rows, ts), lambda i, bb: (bb, 0, i))

    def k_spec(w):
        return pl.BlockSpec((1, ts, w), lambda i, bb: (bb, i, 0))

    def vt_spec(rows):
        return pl.BlockSpec((1, 1, rows, ts), lambda i, bb: (bb, i, 0, 0))

    out_shape = [
        jax.ShapeDtypeStruct((b, 512, s), BF16), jax.ShapeDtypeStruct((b, s, 512), BF16),
        jax.ShapeDtypeStruct((b, nt, 256, ts), BF16),
        jax.ShapeDtypeStruct((b, 256, s), BF16), jax.ShapeDtypeStruct((b, s, 256), BF16),
        jax.ShapeDtypeStruct((b, nt, 256, ts), BF16),
        jax.ShapeDtypeStruct((b, 256, s), BF16), jax.ShapeDtypeStruct((b, s, 128), BF16),
        jax.ShapeDtypeStruct((b, nt, 128, ts), BF16),
        jax.ShapeDtypeStruct((b, s, 256), F32), jax.ShapeDtypeStruct((b, s, 256), BF16),
        kn_shape, kn_shape, kn_shape, jax.ShapeDtypeStruct((b, 3 * W_GROUP, s), BF16),
    ]
    out_specs = [qt_spec(512), k_spec(512), vt_spec(256),
                 qt_spec(256), k_spec(256), vt_spec(256),
                 qt_spec(256), k_spec(128), vt_spec(128),
                 k_spec(256), k_spec(256), kn_spec, kn_spec, kn_spec, qt_spec(3 * W_GROUP)]
    if prev is not None:
        out_shape.append(jax.ShapeDtypeStruct((b, s, d), F32))
        out_specs.append(tok3(d))
    return pl.pallas_call(
        _pre_kernel if prev is None else _post_pre_kernel,
        out_shape=out_shape, grid=(nt, b), in_specs=in_specs, out_specs=out_specs,
        compiler_params=_cparams(2), name="pre_proj" if prev is None else "post_pre_proj",
    )(*head, shift, scale, *params, *tables, *inds)


def _attn_kernel(*refs, maps, n_chunks, tk, tq, kind, lam_init):
    if kind == "diff":
        qt_ref, k_ref, vt_ref, kn_ref, lam_ref, subln_ref, out_ref, w_scr = refs
    else:
        qt_ref, k_ref, vt_ref, kn_ref, out_ref, w_scr = refs
        lam_ref = subln_ref = None
    n_tiles = qt_ref.shape[2] // tq
    dv = DV
    nm = len(maps)
    finish = functools.partial(_attn_finish, out_ref=out_ref, lam_ref=lam_ref, subln_ref=subln_ref,
                               kind=kind, lam_init=lam_init, tq=tq)

    for t in range(n_tiles):
        for i, (kg, q0, qn, woff, v0, _) in enumerate(maps):
            q = qt_ref[0, q0:q0 + qn, tq * t:tq * (t + 1)]
            pieces = []
            if woff > 0:
                pieces.append(jnp.zeros((woff, tq), BF16))
            pieces.append(q)
            if LANES - woff - qn > 0:
                pieces.append(jnp.zeros((LANES - woff - qn, tq), BF16))
            w_scr[t, i] = pieces[0] if len(pieces) == 1 else jnp.concatenate(pieces, axis=0)

    def scores(t, i, c, r1=0, rows=tk):
        kg = maps[i][0]
        r0 = c * tk + r1 if isinstance(c, int) else pl.multiple_of(c * tk, tk)
        return jnp.dot(k_ref[0, pl.ds(r0, rows), LANES * kg:LANES * (kg + 1)], w_scr[t, i],
                       preferred_element_type=F32)

    def pv(i, c, p, r1=0, rows=tk):
        v0 = maps[i][4]
        return jnp.dot(vt_ref[0, c, v0:v0 + dv, r1:r1 + rows], p.astype(BF16),
                       preferred_element_type=F32)

    kn2 = jnp.max(kn_ref[0], axis=0)
    kmax = [jnp.sqrt(kn2[maps[i][5]:maps[i][5] + 1, 0:1]) for i in range(nm)]
    bounds = []
    for t in range(n_tiles):
        for i in range(nm):
            w = w_scr[t, i].astype(F32)
            bounds.append(jnp.sqrt(jnp.sum(w * w, axis=0, keepdims=True)) * kmax[i])
    units = [(t, i, c, r1) for t in range(n_tiles) for c in range(n_chunks)
             for r1 in range(0, tk, SUB_KEYS) for i in range(nm)]
    per_tile = len(units) // n_tiles
    accs, ls = [None] * (n_tiles * nm), [None] * (n_tiles * nm)
    pending = []
    lo, hi = 2.0 ** -DENOM_LOG2_RANGE, 2.0 ** DENOM_LOG2_RANGE
    for n in range(len(units) + SCORE_LEAD):
        if n < len(units):
            t, i, c, r1 = units[n]
            pending.append(scores(t, i, c, r1, SUB_KEYS))
        if n >= SCORE_LEAD:
            t, i, c, r1 = units[n - SCORE_LEAD]
            j = t * nm + i
            p = jnp.exp2(pending.pop(0) - bounds[j])
            part, psum = pv(i, c, p, r1, SUB_KEYS), jnp.sum(p, axis=0, keepdims=True)
            accs[j] = part if accs[j] is None else accs[j] + part
            ls[j] = psum if ls[j] is None else ls[j] + psum
            if (n - SCORE_LEAD + 1) % per_tile == 0:
                finish(t, accs[t * nm:(t + 1) * nm], ls[t * nm:(t + 1) * nm])

    for t in range(n_tiles):
        bad = None
        for l in ls[t * nm:(t + 1) * nm]:
            b = jnp.where((l > lo) & (l < hi), 0.0, 1.0)
            bad = b if bad is None else jnp.maximum(bad, b)

        @pl.when(jnp.max(bad) > 0.0)
        def _(t=t):
            def safe_body(c, carry):
                new = []
                for i in range(nm):
                    m, l, acc = carry[i]
                    s = scores(t, i, c)
                    mn = jnp.maximum(m, jnp.max(s, axis=0, keepdims=True))
                    p, alpha = jnp.exp2(s - mn), jnp.exp2(m - mn)
                    new.append((mn, alpha * l + jnp.sum(p, axis=0, keepdims=True), alpha * acc + pv(i, c, p)))
                return tuple(new)

            init = tuple((jnp.full((1, tq), -jnp.inf, F32), jnp.zeros((1, tq), F32), jnp.zeros((dv, tq), F32))
                         for _ in range(nm))
            fs = lax.fori_loop(0, n_chunks, safe_body, init)
            finish(t, [f[2] for f in fs], [f[1] for f in fs])


def _attn_finish(t, accs, ls, out_ref, lam_ref, subln_ref, kind, lam_init, tq):
    os_ = [acc / l for acc, l in zip(accs, ls)]

    if kind == "diff":
        lf = lam_ref[...]
        s1 = jnp.sum(lf[0:1] * lf[1:2], axis=-1, keepdims=True)
        s2 = jnp.sum(lf[2:3] * lf[3:4], axis=-1, keepdims=True)
        lam_full = jnp.exp(s1) - jnp.exp(s2) + lam_init
        heads = []
        for hh in range(H_B):
            o = os_[2 * hh] - lam_full * os_[2 * hh + 1]
            r = lax.rsqrt(jnp.mean(o * o, axis=0, keepdims=True) + EPS)
            heads.append(((o * r) * subln_ref[...]) * (1.0 - lam_init))
        ot = jnp.concatenate(heads, axis=0)
    else:
        ot = jnp.concatenate(os_, axis=0)
    out_ref[0, :, tq * t:tq * (t + 1)] = ot.astype(out_ref.dtype)


def _attn_call(qt, k, vt, kn, maps, kind, extra=(), lam_init=0.0, name="attn"):
    b, fq, s = qt.shape
    fk = k.shape[2]
    _, nt, fv, tk = vt.shape
    tq = Q_TILE
    n_tiles = 1 if kind == "diff" else max(1, min(s // tq, (UNITS_PER_STEP * tk) // (len(maps) * s)))
    tqs = tq * n_tiles
    in_specs = [pl.BlockSpec((1, fq, tqs), lambda bb, qi: (bb, 0, qi)),
                pl.BlockSpec((1, s, fk), lambda bb, qi: (bb, 0, 0)),
                pl.BlockSpec((1, nt, fv, tk), lambda bb, qi: (bb, 0, 0, 0)),
                pl.BlockSpec((1, nt, NORM_ROWS, LANES), lambda bb, qi: (bb, 0, 0, 0))]
    in_specs += [pl.BlockSpec(e.shape, lambda bb, qi: (0, 0)) for e in extra]
    kern = functools.partial(_attn_kernel, maps=tuple(maps), n_chunks=nt, tk=tk, tq=tq, kind=kind,
                             lam_init=lam_init)
    return pl.pallas_call(
        kern, out_shape=jax.ShapeDtypeStruct((b, W_GROUP, s), BF16),
        grid=(b, s // tqs), in_specs=in_specs,
        out_specs=pl.BlockSpec((1, W_GROUP, tqs), lambda bb, qi: (bb, 0, qi)),
        scratch_shapes=[pltpu.VMEM((n_tiles, len(maps), LANES, tq), BF16)],
        compiler_params=_cparams(2), name=name,
    )(qt, k, vt, kn, *extra)


MAPS_MLA = [(hh, LANES * hh, LANES, 0, V_A * hh, hh) for hh in range(H_A)]
MAPS_DIFF = [(m // 4, DK_B * m, DK_B, DK_B * (m % 4), DV_B * (m // 2), m) for m in range(2 * H_B)]
MAPS_GQA = [(0, HD_D * hh, HD_D, HD_D * (hh // (H_D // KV_H_D)), HD_D * (hh // (H_D // KV_H_D)),
             hh // (H_D // KV_H_D)) for hh in range(H_D)]


def _softplus(x):
    return jnp.maximum(x, 0.0) + jnp.log1p(jnp.exp(-jnp.abs(x)))


def _lru_kernel(xc_ref, cw_ref, cb_ref, wbd_ref, bias_ref, lam_ref, out_ref,
                xpad, af, gf, ar, gr, hfl, pfl, hrl, prl, *, s, tc):
    tseg = s // SUBLANES
    tl_chunk = tc // SUBLANES
    xpad[0:tseg, :] = jnp.zeros((tseg, LANES), F32)
    xpad[tseg + s:2 * tseg + s, :] = jnp.zeros((tseg, LANES), F32)
    xpad[tseg:tseg + s, :] = xc_ref[0]
    lam = lam_ref[0]
    sp_f = _softplus(-lam[0:1])
    sp_r = _softplus(-lam[1:2])
    cw = cw_ref[...]
    cb = cb_ref[...]
    wbd = wbd_ref[0]
    bias = bias_ref[0]

    def gates(half_pre_r, half_pre_i, sp, half_xconv):
        log_a = (jnp.tanh(half_pre_r) + 1.0) * ((-0.5 * LRU_C) * sp)
        a = jnp.exp(log_a)
        v = -jnp.tanh(log_a) * (a * a + 1.0)
        gx = jnp.where(v > 0.0, v * lax.rsqrt(v), 0.0) * ((jnp.tanh(half_pre_i) + 1.0) * half_xconv)
        return a, gx

    def chunk(ci, carry):
        tl0 = ci * tl_chunk
        tiles = [xpad[pl.ds(tseg + tl0 + k, SUBLANES, stride=tseg), :]
                 for k in range(-CONV_PAD_L, tl_chunk + CONV_W - 1 - CONV_PAD_L)]
        xconv = cb
        for j in range(CONV_W):
            xconv = xconv + jnp.concatenate(tiles[j:j + tl_chunk], axis=0) * cw[j:j + 1]
        pre = jnp.dot(xconv.astype(BF16), wbd, preferred_element_type=F32) + bias
        half_xconv = 0.5 * xconv
        r0 = pl.multiple_of(ci * tc, tc)
        a, gx = gates(pre[:, 0:128], pre[:, 128:256], sp_f, half_xconv)
        af[pl.ds(r0, tc), :] = a
        gf[pl.ds(r0, tc), :] = gx
        a, gx = gates(pre[:, 256:384], pre[:, 384:512], sp_r, half_xconv)
        ar[pl.ds(r0, tc), :] = a
        gr[pl.ds(r0, tc), :] = gx
        return carry

    lax.fori_loop(0, s // tc, chunk, 0)

    def scan_body(t, carry):
        hf, pf, hr, pr = carry
        r0 = pl.multiple_of(t * SUBLANES, SUBLANES)
        a = af[pl.ds(r0, SUBLANES), :]
        hf = a * hf + gf[pl.ds(r0, SUBLANES), :]
        pf = a * pf
        hfl[pl.ds(r0, SUBLANES), :] = hf
        pfl[pl.ds(r0, SUBLANES), :] = pf
        r1 = pl.multiple_of((tseg - 1 - t) * SUBLANES, SUBLANES)
        a = ar[pl.ds(r1, SUBLANES), :]
        hr = a * hr + gr[pl.ds(r1, SUBLANES), :]
        pr = a * pr
        hrl[pl.ds(r1, SUBLANES), :] = hr
        prl[pl.ds(r1, SUBLANES), :] = pr
        return hf, pf, hr, pr

    zero8 = jnp.zeros((SUBLANES, LANES), F32)
    one8 = jnp.ones((SUBLANES, LANES), F32)
    h_end, p_end, h_start, p_start = lax.fori_loop(0, tseg, scan_body, (zero8, one8, zero8, one8), unroll=8)

    row = lax.broadcasted_iota(jnp.int32, (SUBLANES, LANES), 0)
    cf, cr = zero8, zero8
    for _ in range(SUBLANES - 1):
        cf = jnp.where(row >= 1, pltpu.roll(h_end + p_end * cf, 1, 0), 0.0)
        cr = jnp.where(row < SUBLANES - 1, pltpu.roll(h_start + p_start * cr, SUBLANES - 1, 0), 0.0)

    ho = af

    def patch(ci, carry):
        r0 = pl.multiple_of(ci * tc, tc)
        cft = jnp.concatenate([cf] * tl_chunk, axis=0)
        crt = jnp.concatenate([cr] * tl_chunk, axis=0)
        ho[pl.ds(r0, tc), :] = (hfl[pl.ds(r0, tc), :] + pfl[pl.ds(r0, tc), :] * cft) + (
            hrl[pl.ds(r0, tc), :] + prl[pl.ds(r0, tc), :] * crt)
        return carry

    lax.fori_loop(0, s // tc, patch, 0)

    def untangle(u, carry):
        for j in range(SUBLANES):
            base = u * (2 * SUBLANES * SUBLANES) + j
            blk = jnp.concatenate([ho[pl.ds(base, SUBLANES, stride=SUBLANES), :],
                                   ho[pl.ds(base + SUBLANES * SUBLANES, SUBLANES, stride=SUBLANES), :]], axis=0)
            out_ref[0, pl.ds(pl.multiple_of(j * tseg + u * 2 * SUBLANES, 2 * SUBLANES), 2 * SUBLANES), :] = (
                blk.astype(out_ref.dtype))
        return carry

    lax.fori_loop(0, tseg // (2 * SUBLANES), untangle, 0)


def _lru_call(xc, lw):
    b, s, _ = xc.shape
    kern = functools.partial(_lru_kernel, s=s, tc=LRU_CHUNK)
    seq = pltpu.VMEM((s, LANES), F32)
    return pl.pallas_call(
        kern, out_shape=jax.ShapeDtypeStruct((b, s, W_GROUP), BF16),
        grid=(b, W_GROUP // LANES),
        in_specs=[pl.BlockSpec((1, s, LANES), lambda bb, hf: (bb, 0, hf)),
                  pl.BlockSpec((CONV_W, LANES), lambda bb, hf: (0, hf)),
                  pl.BlockSpec((1, LANES), lambda bb, hf: (0, hf)),
                  pl.BlockSpec((1, LANES, 4 * LANES), lambda bb, hf: (hf, 0, 0)),
                  pl.BlockSpec((1, 1, 4 * LANES), lambda bb, hf: (hf, 0, 0)),
                  pl.BlockSpec((1, 2, LANES), lambda bb, hf: (hf, 0, 0))],
        out_specs=pl.BlockSpec((1, s, LANES), lambda bb, hf: (bb, 0, hf)),
        scratch_shapes=[pltpu.VMEM((s + 2 * (s // SUBLANES), LANES), F32)] + [seq] * 8,
        compiler_params=_cparams(2), name="rglru",
    )(xc, lw["conv_w"], lw["conv_b"], lw["lru_wbd"], lw["lru_bias"], lw["lru_lam"])


def _post_update(x_ref, o_refs, gt_ref, gc_ref, gate_ref, wout_ref):
    y = None
    row = 0
    for i, o_ref in enumerate(o_refs):
        w_blk = wout_ref[W_GROUP * i:W_GROUP * (i + 1), :]
        if i == 2:
            og = (o_ref[0].astype(F32) * gc_ref[0].astype(F32)).astype(BF16)
            part = jnp.dot(og, w_blk, preferred_element_type=F32)
        else:
            og = (o_ref[0].astype(F32) * gt_ref[0, row:row + W_GROUP, :].astype(F32)).astype(BF16)
            row += W_GROUP
            part = lax.dot_general(og, w_blk, (((0,), (0,)), ((), ())), preferred_element_type=F32)
        y = part if y is None else y + part
    return x_ref[0] + gate_ref[0] * y


def _post_kernel(x_ref, oa_ref, ob_ref, oc_ref, od_ref, gt_ref, gc_ref, gate_ref, wout_ref, fn_ref, out_ref):
    xn = _post_update(x_ref, (oa_ref, ob_ref, oc_ref, od_ref), gt_ref, gc_ref, gate_ref, wout_ref)
    ms = jnp.mean(xn * xn, axis=-1, keepdims=True)
    out_ref[0] = (xn * lax.rsqrt(ms + EPS)) * fn_ref[...]


def _post_call(x, oa, ob, oc, od, gt, gc, gate, w_out_bf, final_norm):
    b, s, d = x.shape
    ts = POST_TILE
    tok = lambda w: pl.BlockSpec((1, ts, w), lambda bb, i: (bb, i, 0))
    feat = lambda r: pl.BlockSpec((1, r, ts), lambda bb, i: (bb, 0, i))
    return pl.pallas_call(
        _post_kernel,
        out_shape=jax.ShapeDtypeStruct((b, s, d), F32),
        grid=(b, s // ts),
        in_specs=[tok(d), feat(W_GROUP), feat(W_GROUP), tok(W_GROUP), feat(W_GROUP), feat(3 * W_GROUP), tok(W_GROUP),
                  pl.BlockSpec((1, 1, d), lambda bb, i: (bb, 0, 0)),
                  pl.BlockSpec((d, d), lambda bb, i: (0, 0)),
                  pl.BlockSpec((1, d), lambda bb, i: (0, 0))],
        out_specs=tok(d),
        compiler_params=_cparams(2), name="post_proj",
    )(x, oa, ob, oc, od, gt, gc, gate, w_out_bf, final_norm)


def _angles(pos, half, theta):
    inv = jnp.power(jnp.float32(theta), -jnp.arange(half, dtype=F32) / half)
    return pos[:, None] * inv[None, :]


def _tables(s):
    pos = jnp.arange(s, dtype=F32)
    rows = s // GRID_W
    row = jnp.repeat(jnp.arange(rows, dtype=F32), GRID_W)
    col = jnp.tile(jnp.arange(GRID_W, dtype=F32), rows)
    a = _angles(pos, ROPE_A // 2, MLA_ROPE_THETA)
    c, sn = jnp.cos(a), jnp.sin(a)
    z64, z32 = jnp.zeros((s, 64), F32), jnp.zeros((s, 32), F32)
    tb = {"mkc": jnp.concatenate([z64, c, c, z32], axis=1),
          "mks": jnp.concatenate([z64, -sn, sn, z32], axis=1),
          "mqc": jnp.concatenate([c, c], axis=1).T,
          "mqs": jnp.concatenate([-sn, sn], axis=1).T}
    a = _angles(pos, DK_B // 8, ROPE_THETA)
    c, sn = jnp.cos(a), jnp.sin(a)
    c32 = jnp.concatenate([c, c, jnp.ones((s, DK_B - DK_B // 4), F32)], axis=1)
    s32 = jnp.concatenate([-sn, sn, jnp.zeros((s, DK_B - DK_B // 4), F32)], axis=1)
    tb.update({"dkc": jnp.tile(c32, (1, 2 * H_B)), "dks": jnp.tile(s32, (1, 2 * H_B)),
               "dqc": c32.T, "dqs": s32.T})
    ar_ = _angles(row, HD_D // 4, AXIAL_THETA)
    ac_ = _angles(col, HD_D // 4, AXIAL_THETA)
    cr, sr, cc, sc = jnp.cos(ar_), jnp.sin(ar_), jnp.cos(ac_), jnp.sin(ac_)
    c64 = jnp.concatenate([cr, cr, cc, cc], axis=1)
    s64 = jnp.concatenate([-sr, sr, -sc, sc], axis=1)
    tb.update({"gkc": jnp.tile(c64, (1, KV_H_D)), "gks": jnp.tile(s64, (1, KV_H_D)),
               "gqc": c64.T, "gqs": s64.T})
    return tb


def _blockdiag_pair(w, half):
    z = jnp.zeros((BW_C, BW_C), F32)
    top = jnp.concatenate([w[2 * half], z], axis=1)
    bot = jnp.concatenate([z, w[2 * half + 1]], axis=1)
    return jnp.concatenate([top, bot], axis=0)


def _layer_weights(l, p):
    w = p["w_in"][l]
    col = lambda o, n: w[:, o:o + n]
    z64 = jnp.zeros((D_MODEL, 64), F32)
    z32 = jnp.zeros((D_MODEL, 32), F32)
    kra = col(O_KRA, ROPE_A)
    w_std = jnp.concatenate([
        col(O_KVA, KV_LORA),
        z64, kra, z32,
        z64, kra[:, P_MLA], z32,
        col(O_KB, 256), col(O_KD, 128),
        col(O_XC, W_GROUP), col(O_GC, W_GROUP)], axis=1)
    w_t = jnp.concatenate([
        col(O_QA, Q_LORA),
        col(O_QB, 256), col(O_VB, 256),
        col(O_QD, 256), col(O_VD, 128),
        col(O_GA, W_GROUP), col(O_GB, W_GROUP), col(O_GD, W_GROUP)], axis=1).T
    wuq = p["mla_w_uq"][l]
    wukv = p["mla_w_ukv"][l]
    dkv = NOPE_A + V_A
    z_k = jnp.zeros((KV_LORA, LANES - NOPE_A), F32)
    w_ukp = jnp.concatenate(
        [jnp.concatenate([wukv[:, dkv * h:dkv * h + NOPE_A], z_k], axis=1) for h in range(H_A)], axis=1)
    w_uv = jnp.concatenate([wukv[:, dkv * h + NOPE_A:dkv * (h + 1)] for h in range(H_A)], axis=1)
    gq, gk = p["gqa_q_norm"][l], p["gqa_k_norm"][l]
    wa, wx = p["lru_wa"][l], p["lru_wx"][l]
    ba, bx = p["lru_ba"][l], p["lru_bx"][l]
    wbd = jnp.stack([jnp.concatenate([_blockdiag_pair(wa[0], hf), _blockdiag_pair(wx[0], hf),
                                      _blockdiag_pair(wa[1], hf), _blockdiag_pair(wx[1], hf)], axis=1)
                     for hf in range(2)])
    sl = lambda v, hf: v[LANES * hf:LANES * (hf + 1)]
    bias = jnp.stack([jnp.concatenate([sl(ba[0], hf), sl(bx[0], hf), sl(ba[1], hf), sl(bx[1], hf)])[None]
                      for hf in range(2)])
    lam = p["lru_lambda"][l]
    lam_h = jnp.stack([jnp.stack([sl(lam[0], hf), sl(lam[1], hf)]) for hf in range(2)])
    return {
        "norm_g": p["norm_g"][l][None], "w_std": w_std.astype(BF16), "w_t": w_t.astype(BF16),
        "gqa_col": p["mla_q_norm"][l][:, None], "w_uqT": wuq.T.astype(BF16),
        "gkv_row": p["mla_kv_norm"][l][None],
        "w_ukp": w_ukp.astype(BF16), "w_uvT": w_uv.T.astype(BF16),
        "gq_col": gq[:, None], "gk_row": jnp.tile(gk, KV_H_D)[None],
        "diff_lambda": p["diff_lambda"][l], "subln_col": p["diff_subln"][l][:, None],
        "conv_w": p["lru_conv_w"][l], "conv_b": p["lru_conv_b"][l][None],
        "lru_wbd": (0.5 * wbd).astype(BF16), "lru_bias": 0.5 * bias, "lru_lam": lam_h,
        "w_out": p["w_out"][l].astype(BF16),
    }


def _trunk(x, mods, lws, final_norm):
    b, s, d = x.shape
    tb = _tables(s)
    prev = None
    for l in range(DEPTH):
        lw = lws[l]
        shift, scale, gate = mods[l]
        outs = _pre_call(x, shift, scale, lw, tb, prev)
        if prev is not None:
            x = outs[-1]
        (qt_mla, k_mla, vt_mla, qt_diff, k_diff, vt_diff, qt_gqa, k_gqa, vt_gqa, xc, gc,
         kn_mla, kn_diff, kn_gqa, gt) = outs[:15]
        lam_init = 0.8 - 0.6 * math.exp(-0.3 * l)
        oa = _attn_call(qt_mla, k_mla, vt_mla, kn_mla, MAPS_MLA, "mla", name="attn_mla")
        ob = _attn_call(qt_diff, k_diff, vt_diff, kn_diff, MAPS_DIFF, "diff",
                        extra=(lw["diff_lambda"], lw["subln_col"]), lam_init=lam_init, name="attn_diff")
        od = _attn_call(qt_gqa, k_gqa, vt_gqa, kn_gqa, MAPS_GQA, "gqa", name="attn_gqa")
        oc = _lru_call(xc, lw)
        prev = (oa, ob, oc, od, gt, gc, gate, lw["w_out"])
    return _post_call(x, *prev, final_norm)


def kernel(x_prompt, x_sample, c_prompt, c_sample, ada_w, ada_b, norm_g, w_in, mla_q_norm, mla_w_uq, mla_kv_norm, mla_w_ukv, diff_lambda, diff_subln, lru_conv_w, lru_conv_b, lru_wa, lru_ba, lru_wx, lru_bx, lru_lambda, gqa_q_norm, gqa_k_norm, w_out, final_norm):
    p = dict(norm_g=norm_g, w_in=w_in, mla_q_norm=mla_q_norm, mla_w_uq=mla_w_uq, mla_kv_norm=mla_kv_norm,
             mla_w_ukv=mla_w_ukv, diff_lambda=diff_lambda, diff_subln=diff_subln,
             lru_conv_w=lru_conv_w, lru_conv_b=lru_conv_b, lru_wa=lru_wa, lru_ba=lru_ba,
             lru_wx=lru_wx, lru_bx=lru_bx, lru_lambda=lru_lambda,
             gqa_q_norm=gqa_q_norm, gqa_k_norm=gqa_k_norm, w_out=w_out)
    lws = [_layer_weights(l, p) for l in range(DEPTH)]
    nb = x_prompt.shape[0]
    mod = _modulation(jnp.concatenate([c_prompt, c_sample], axis=0), ada_w.astype(BF16), ada_b)

    def mods_for(lo, hi):
        out = []
        for l in range(DEPTH):
            m = mod[l, lo:hi]
            out.append(tuple(m[:, D_MODEL * j:D_MODEL * (j + 1)][:, None, :] for j in range(3)))
        return out

    fn = final_norm[None]
    y_prompt = _trunk(x_prompt, mods_for(0, nb), lws, fn)
    y_sample = _trunk(x_sample, mods_for(nb, nb + x_sample.shape[0]), lws, fn)
    return (y_prompt, y_sample)
```
